```python
import math
import jax, jax.numpy as jnp
from jax import lax
import numpy as np

D_MODEL = 2048
BATCH = 2
SEQ = 4096
DEPTH = 4
DEC_BATCH = 8
DEC_SEQ = 1
PAST_LEN = 16384
PAGE_SIZE = 128

N_EVEN = (DEPTH + 1) // 2
N_ODD = DEPTH // 2
H_A = 8
DH_A = D_MODEL // 16
H_B = 4
DH_B = D_MODEL // 16
H_C = 8
DH_C = D_MODEL // 16
H_I = 16
D_I = D_MODEL // 32
TOPK_MAX = 256
H_D = 4
DK_D = D_MODEL // 16
DV_D = D_MODEL // 8
W_A = H_A * DH_A
W_B = H_B * 2 * DH_B
W_C = H_C * DH_C
W_D = H_D * DV_D
MIX_EVEN = W_A + W_B
MIX_ODD = W_C + W_D
EVEN_SPLITS = (W_A, W_A, W_A, W_A, H_A, H_A, W_B, W_B, W_B)
ODD_SPLITS = (W_C, W_C, W_C, H_I * D_I, D_I, H_I, H_D * DK_D, H_D * DK_D, W_D, W_D)
IN_EVEN = sum(EVEN_SPLITS)
IN_ODD = sum(ODD_SPLITS)
N_EXPERTS = 32
N_GROUPS = 4
EXP_PER_GROUP = N_EXPERTS // N_GROUPS
TOP_K = 2
D_EXPERT = D_MODEL // 4
MOE_BLOCK = 128
CHUNK = 128
Q_BLOCK = 128
ROPE_THETA = 10000.0
EPS = 1e-5
ALPHA = (2 * DEPTH) ** 0.25
BETA = (8 * DEPTH) ** -0.25

kernel_name = 'hybrid_mlstm_diffattn_dsa_retnet_moe_step'


def _split(z, sizes):
    cuts = np.cumsum(sizes)[:-1].tolist()
    return jnp.split(z, cuts, axis=-1)


def _chunk_len(t_len):
    return CHUNK if t_len % CHUNK == 0 else t_len


def layer_norm(x, g, b):
    xf = x.astype(jnp.float32)
    mu = xf.mean(-1, keepdims=True)
    var = jnp.square(xf - mu).mean(-1, keepdims=True)
    return ((xf - mu) * lax.rsqrt(var + EPS) * g + b).astype(x.dtype)


def rms_norm(x, g):
    xf = x.astype(jnp.float32)
    return xf * lax.rsqrt(jnp.square(xf).mean(-1, keepdims=True) + EPS) * g


def head_norm(x, g):
    xf = x.astype(jnp.float32)
    mu = xf.mean(-1, keepdims=True)
    var = jnp.square(xf - mu).mean(-1, keepdims=True)
    return (xf - mu) * lax.rsqrt(var + EPS) * g


def rope(x, pos):
    half = x.shape[-1] // 2
    inv = ROPE_THETA ** (-jnp.arange(half, dtype=jnp.float32) / half)
    ang = pos.astype(jnp.float32)[:, None] * inv[None, :]
    cos, sin = jnp.cos(ang)[:, None, :], jnp.sin(ang)[:, None, :]
    xf = x.astype(jnp.float32)
    x1, x2 = xf[..., :half], xf[..., half:]
    return jnp.concatenate([x1 * cos - x2 * sin, x1 * sin + x2 * cos], -1).astype(x.dtype)


def to_chunks(x, cl):
    b, t, h = x.shape[:3]
    x = x.reshape((b, t // cl, cl, h) + x.shape[3:])
    return x.transpose((1, 0, 3, 2) + tuple(range(4, x.ndim)))


def from_chunks(x):
    nc, b, h, cl, d = x.shape
    return x.transpose(1, 0, 3, 2, 4).reshape(b, nc * cl, h, d)


def q_blocks(x):
    b, t = x.shape[:2]
    return jnp.moveaxis(x.reshape((b, t // Q_BLOCK, Q_BLOCK) + x.shape[2:]), 1, 0)


def un_blocks(x):
    x = jnp.moveaxis(x, 0, 1)
    return x.reshape((x.shape[0], -1) + x.shape[3:])


def gather_pages(pool, page_table):
    g = pool[page_table]
    return g.reshape((g.shape[0], -1) + g.shape[3:])


def gather_paged_rows(pool, page_table, pos):
    page = (pos // PAGE_SIZE).reshape(pos.shape[0], -1)
    phys = jnp.take_along_axis(page_table, page, axis=1).reshape(pos.shape)
    return pool[phys, pos % PAGE_SIZE]


def mlstm_chunk(state, inp):
    c, n, m = state
    q, k, v, ig, lf = inp
    L = q.shape[2]
    a = jnp.cumsum(lf, axis=-1)
    causal = jnp.tril(jnp.ones((L, L), dtype=bool))
    log_d = jnp.where(causal, a[..., :, None] - a[..., None, :] + ig[..., None, :], -jnp.inf)
    inter = a + m[..., None]
    m_t = jnp.maximum(inter, log_d.max(-1))
    w_inter = jnp.exp(inter - m_t)
    s = jnp.einsum('bhtd,bhsd->bhts', q, k) * jnp.exp(log_d - m_t[..., None])
    num = w_inter[..., None] * jnp.einsum('bhtd,bhde->bhte', q, c) + jnp.einsum('bhts,bhse->bhte', s, v)
    den = w_inter * jnp.einsum('bhtd,bhd->bht', q, n) + s.sum(-1)
    h = num / jnp.maximum(jnp.abs(den), jnp.exp(-m_t))[..., None]
    m_new = m_t[..., -1]
    w_c = jnp.exp(a[..., -1] + m - m_new)
    w_s = jnp.exp(a[..., -1:] - a + ig - m_new[..., None])
    c_new = w_c[..., None, None] * c + jnp.einsum('bhs,bhsd,bhse->bhde', w_s, k, v)
    n_new = w_c[..., None] * n + jnp.einsum('bhs,bhsd->bhd', w_s, k)
    return (c_new, n_new, m_new), h


def retention_chunk(s_prev, inp, log_gamma):
    q, k, v = inp
    L = q.shape[2]
    t = jnp.arange(L, dtype=jnp.float32)
    diff = t[:, None] - t[None, :]
    causal = diff >= 0
    lg = log_gamma[:, None, None]
    decay = jnp.where(causal, jnp.exp(jnp.where(causal, diff, 0.0) * lg), 0.0)
    inner = jnp.einsum('bhtd,bhsd->bhts', q, k) * decay
    o = jnp.einsum('bhts,bhse->bhte', inner, v) + jnp.exp((t + 1.0) * log_gamma[:, None])[..., None] * jnp.einsum('bhtd,bhde->bhte', q, s_prev)
    w_s = jnp.exp((L - 1.0 - t) * log_gamma[:, None])
    s_new = jnp.exp(L * log_gamma)[:, None, None] * s_prev + jnp.einsum('hs,bhsd,bhse->bhde', w_s, k, v)
    return s_new, o


def diff_attend(q, k, v, q_pos, k_pos, lam):
    k = k.reshape(k.shape[:3] + (2, DH_B))
    s = jnp.einsum('bqhcd,bshcd->bhcqs', q, k).astype(jnp.float32) * DH_B ** -0.5
    s = jnp.where(k_pos[None, :] <= q_pos[:, None], s, -jnp.inf)
    p = jax.nn.softmax(s, axis=-1)
    a = p[:, :, 0] - lam * p[:, :, 1]
    return jnp.einsum('bhqs,bshe->bqhe', a.astype(v.dtype), v)


def indexer_topk(qi, wi, ki, q_pos, k_pos, topk):
    sc = jax.nn.relu(jnp.einsum('bqhd,bsd->bqhs', qi.astype(jnp.float32), ki.astype(jnp.float32)))
    score = jnp.einsum('bqh,bqhs->bqs', wi, sc)
    score = jnp.where(k_pos[None, None, :] <= q_pos[None, :, None], score, -jnp.inf)
    vals, idx = lax.top_k(score, topk)
    return idx, jnp.isfinite(vals)


def sparse_attend(q, k_sel, v_sel, valid):
    s = jnp.einsum('bqhd,bqkhd->bqhk', q, k_sel).astype(jnp.float32) * DH_C ** -0.5
    s = jnp.where(valid[:, :, None, :], s, -jnp.inf)
    p = jax.nn.softmax(s, axis=-1)
    return jnp.einsum('bqhk,bqkhd->bqhd', p.astype(v_sel.dtype), v_sel)


def even_mixer(x, pos, layer_idx, weights, mlstm_state, kv_past):
    w_in, w_out, b_ig, b_fg, g_mh, lq1, lk1, lq2, lk2, g_sub = weights
    f32 = jnp.float32
    bsz, t_len, _ = x.shape
    qa, ka, va, oa, ig, fg, qb, kb, vb = _split(x @ w_in, EVEN_SPLITS)
    hs = (bsz, t_len, H_A, DH_A)
    qa = qa.reshape(hs).astype(f32)
    ka = ka.reshape(hs).astype(f32) * DH_A ** -0.5
    va = va.reshape(hs).astype(f32)
    ig = (ig + b_ig).astype(f32)
    lf = jax.nn.log_sigmoid((fg + b_fg).astype(f32))
    cl = _chunk_len(t_len)
    state0 = tuple(s.astype(f32) for s in mlstm_state)
    (c, n, m), h = lax.scan(mlstm_chunk, state0, tuple(to_chunks(a, cl) for a in (qa, ka, va, ig, lf)))
    h = head_norm(from_chunks(h), g_mh) * jax.nn.sigmoid(oa.reshape(hs).astype(f32))
    lam_init = 0.8 - 0.6 * math.exp(-0.3 * layer_idx)
    lam = jnp.exp(jnp.sum(lq1 * lk1).astype(f32)) - jnp.exp(jnp.sum(lq2 * lk2).astype(f32)) + lam_init
    qb = rope(qb.reshape(bsz, t_len, 2 * H_B, DH_B), pos).reshape(bsz, t_len, H_B, 2, DH_B)
    kb = rope(kb.reshape(bsz, t_len, 2 * H_B, DH_B), pos).reshape(bsz, t_len, H_B, 2 * DH_B)
    vb = vb.reshape(bsz, t_len, H_B, 2 * DH_B)
    if kv_past is None:
        ob = un_blocks(lax.map(lambda a: diff_attend(a[0], kb, vb, a[1], pos, lam),
                               (q_blocks(qb), pos.reshape(-1, Q_BLOCK))))
    else:
        k_all = jnp.concatenate([kv_past[0].astype(kb.dtype), kb], axis=1)
        v_all = jnp.concatenate([kv_past[1].astype(vb.dtype), vb], axis=1)
        ob = diff_attend(qb, k_all, v_all, pos, jnp.arange(k_all.shape[1]), lam)
    ob = rms_norm(ob, g_sub) * (1.0 - lam_init)
    mixed = jnp.concatenate([h.reshape(bsz, t_len, W_A), ob.reshape(bsz, t_len, W_B)], -1).astype(x.dtype)
    return mixed @ w_out, (c, n, m, kb, vb)


def odd_mixer(x, pos, weights, ret_state, past):
    w_in, w_out, g_ret = weights
    f32 = jnp.float32
    bsz, t_len, _ = x.shape
    qc, kc, vc, qi, ki, wi, qd, kd, vd, gd = _split(x @ w_in, ODD_SPLITS)
    qc = rope(qc.reshape(bsz, t_len, H_C, DH_C), pos)
    kc = rope(kc.reshape(bsz, t_len, H_C, DH_C), pos)
    vc = vc.reshape(bsz, t_len, H_C, DH_C)
    qi = rope(qi.reshape(bsz, t_len, H_I, D_I), pos)
    ki = rope(ki.reshape(bsz, t_len, 1, D_I), pos)[:, :, 0]
    wi = wi.astype(f32) * (H_I * D_I) ** -0.5
    take = jax.vmap(lambda rows, idx: rows[idx])
    if past is None:
        topk = min(TOPK_MAX, t_len // 4)
        def block(args):
            qcb, qib, wib, qpos = args
            idx, valid = indexer_topk(qib, wib, ki, qpos, pos, topk)
            return sparse_attend(qcb, take(kc, idx), take(vc, idx), valid)
        oc = un_blocks(lax.map(block, (q_blocks(qc), q_blocks(qi), q_blocks(wi), pos.reshape(-1, Q_BLOCK))))
    else:
        pool_k, pool_v, pool_i, page_table = past
        past_len = page_table.shape[1] * PAGE_SIZE
        ki_all = jnp.concatenate([gather_pages(pool_i, page_table).astype(ki.dtype), ki], axis=1)
        topk = min(TOPK_MAX, (past_len + t_len) // 4)
        idx, valid = indexer_topk(qi, wi, ki_all, pos, jnp.arange(past_len + t_len), topk)
        in_past = (idx < past_len)[..., None, None]
        old = jnp.minimum(idx, past_len - 1)
        new = jnp.clip(idx - past_len, 0, t_len - 1)
        k_sel = jnp.where(in_past, gather_paged_rows(pool_k, page_table, old).astype(kc.dtype), take(kc, new))
        v_sel = jnp.where(in_past, gather_paged_rows(pool_v, page_table, old).astype(vc.dtype), take(vc, new))
        oc = sparse_attend(qc, k_sel, v_sel, valid)
    hs_k = (bsz, t_len, H_D, DK_D)
    qd = rope(qd.reshape(hs_k), pos).astype(f32)
    kd = rope(kd.reshape(hs_k), pos).astype(f32) * DK_D ** -0.5
    vd = vd.reshape(bsz, t_len, H_D, DV_D).astype(f32)
    log_gamma = jnp.log(1.0 - 2.0 ** (-5.0 - jnp.arange(H_D, dtype=f32)))
    cl = _chunk_len(t_len)
    s_new, od = lax.scan(lambda s, inp: retention_chunk(s, inp, log_gamma), ret_state.astype(f32),
                         tuple(to_chunks(a, cl) for a in (qd, kd, vd)))
    od = head_norm(from_chunks(od), g_ret) * jax.nn.silu(gd.reshape(bsz, t_len, H_D, DV_D).astype(f32))
    mixed = jnp.concatenate([oc.reshape(bsz, t_len, W_C).astype(x.dtype), od.reshape(bsz, t_len, W_D).astype(x.dtype)], -1)
    return mixed @ w_out, (kc, vc, ki, s_new)


def moe_ffn(x, weights):
    w_router, b_router, w_gate, w_up, w_down = weights
    shape = x.shape
    x = x.reshape(-1, shape[-1])
    n_tok, d = x.shape
    aff = jax.nn.sigmoid(jnp.dot(x.astype(jnp.float32), w_router.astype(jnp.float32)))
    sel = aff + b_router.astype(jnp.float32)
    grp = lax.top_k(sel.reshape(n_tok, N_GROUPS, EXP_PER_GROUP), TOP_K)[0].sum(-1)
    best = jnp.argmax(grp, axis=-1)
    in_grp = (jnp.arange(N_EXPERTS) // EXP_PER_GROUP)[None, :] == best[:, None]
    _, eidx = lax.top_k(jnp.where(in_grp, sel, -jnp.inf), TOP_K)
    gate = jnp.take_along_axis(aff, eidx, axis=1)
    gate = gate / gate.sum(-1, keepdims=True)
    m = n_tok * TOP_K
    blk = max(1, min(MOE_BLOCK, m // N_EXPERTS))
    n_blocks = -(-(m + N_EXPERTS * (blk - 1)) // blk)
    flat_e = eidx.reshape(-1).astype(jnp.int32)
    order = jnp.argsort(flat_e)
    sorted_e = flat_e[order]
    counts = jnp.bincount(flat_e, length=N_EXPERTS)
    padded = (counts + blk - 1) // blk * blk
    pad_end = jnp.cumsum(padded)
    pad_start = pad_end - padded
    start = jnp.cumsum(counts) - counts
    dest_sorted = (pad_start[sorted_e] + jnp.arange(m) - start[sorted_e]).astype(jnp.int32)
    dest = jnp.zeros((m,), jnp.int32).at[order].set(dest_sorted)
    tok = jnp.arange(m) // TOP_K
    buf = jnp.zeros((n_blocks * blk, d), x.dtype).at[dest].set(x[tok])
    blk_e = jnp.minimum(jnp.searchsorted(pad_end, jnp.arange(n_blocks) * blk, side='right'), N_EXPERTS - 1)
    def expert_block(args):
        xb, e = args
        hdn = jax.nn.silu(xb @ w_gate[e]) * (xb @ w_up[e])
        return hdn @ w_down[e]
    out = lax.map(expert_block, (buf.reshape(n_blocks, blk, d), blk_e)).reshape(-1, d)
    y = out[dest].reshape(n_tok, TOP_K, d)
    return jnp.einsum('nk,nkd->nd', gate.astype(x.dtype), y).reshape(shape)


def setup_inputs(seed: int = 0) -> dict:
    key = jax.random.key(seed)
    ks = iter(jax.random.split(key, 48))
    def nrm(shape, scale):
        return jax.random.normal(next(ks), shape, jnp.float32) * scale
    n_pages = PAST_LEN // PAGE_SIZE
    n_used = DEC_BATCH * n_pages
    n_pool = n_used + max(1, n_used // 4)
    inp = {}
    inp['x_prompt'] = nrm((BATCH, SEQ, D_MODEL), 1.0)
    inp['x_sample'] = nrm((DEC_BATCH, DEC_SEQ, D_MODEL), 1.0)
    inp['state_mlstm_c'] = nrm((N_EVEN, DEC_BATCH, H_A, DH_A, DH_A), 0.1)
    inp['state_mlstm_n'] = nrm((N_EVEN, DEC_BATCH, H_A, DH_A), 0.1)
    inp['state_mlstm_m'] = nrm((N_EVEN, DEC_BATCH, H_A), 0.5)
    inp['cache_diff_k'] = nrm((N_EVEN, n_pool, PAGE_SIZE, H_B, 2 * DH_B), 1.0)
    inp['cache_diff_v'] = nrm((N_EVEN, n_pool, PAGE_SIZE, H_B, 2 * DH_B), 1.0)
    inp['cache_dsa_k'] = nrm((N_ODD, n_pool, PAGE_SIZE, H_C, DH_C), 1.0)
    inp['cache_dsa_v'] = nrm((N_ODD, n_pool, PAGE_SIZE, H_C, DH_C), 1.0)
    inp['cache_idx_k'] = nrm((N_ODD, n_pool, PAGE_SIZE, D_I), 1.0)
    inp['state_ret'] = nrm((N_ODD, DEC_BATCH, H_D, DK_D, DV_D), 0.1)
    inp['page_table'] = jax.random.permutation(next(ks), n_pool)[:n_used].reshape(DEC_BATCH, n_pages).astype(jnp.int32)
    inp['w_in_even'] = nrm((N_EVEN, D_MODEL, IN_EVEN), D_MODEL ** -0.5)
    inp['w_out_even'] = nrm((N_EVEN, MIX_EVEN, D_MODEL), BETA * MIX_EVEN ** -0.5)
    inp['b_igate'] = nrm((N_EVEN, H_A), 0.1)
    inp['b_fgate'] = jnp.linspace(3.0, 6.0, H_A, dtype=jnp.float32)[None, :] + nrm((N_EVEN, H_A), 0.1)
    inp['g_mlstm'] = 1.0 + nrm((N_EVEN, H_A, DH_A), 0.02)
    inp['lam_q1'] = nrm((N_EVEN, DH_B), 0.1)
    inp['lam_k1'] = nrm((N_EVEN, DH_B), 0.1)
    inp['lam_q2'] = nrm((N_EVEN, DH_B), 0.1)
    inp['lam_k2'] = nrm((N_EVEN, DH_B), 0.1)
    inp['g_subln'] = 1.0 + nrm((N_EVEN, 2 * DH_B), 0.02)
    inp['w_in_odd'] = nrm((N_ODD, D_MODEL, IN_ODD), D_MODEL ** -0.5)
    inp['w_out_odd'] = nrm((N_ODD, MIX_ODD, D_MODEL), BETA * MIX_ODD ** -0.5)
    inp['g_ret'] = 1.0 + nrm((N_ODD, H_D, DV_D), 0.02)
    inp['ln_mix_g'] = 1.0 + nrm((DEPTH, D_MODEL), 0.02)
    inp['ln_mix_b'] = nrm((DEPTH, D_MODEL), 0.02)
    inp['ln_ffn_g'] = 1.0 + nrm((DEPTH, D_MODEL), 0.02)
    inp['ln_ffn_b'] = nrm((DEPTH, D_MODEL), 0.02)
    inp['w_router'] = nrm((D_MODEL, N_EXPERTS), D_MODEL ** -0.5)
    inp['b_router'] = nrm((N_EXPERTS,), 0.01)
    inp['w_gate'] = nrm((DEPTH, N_EXPERTS, D_MODEL, D_EXPERT), D_MODEL ** -0.5)
    inp['w_up'] = nrm((DEPTH, N_EXPERTS, D_MODEL, D_EXPERT), D_MODEL ** -0.5)
    inp['w_down'] = nrm((DEPTH, N_EXPERTS, D_EXPERT, D_MODEL), BETA * D_EXPERT ** -0.5)
    return inp


def reference(x_prompt, x_sample, state_mlstm_c, state_mlstm_n, state_mlstm_m, cache_diff_k, cache_diff_v,
              cache_dsa_k, cache_dsa_v, cache_idx_k, state_ret, page_table, w_in_even, w_out_even, b_igate,
              b_fgate, g_mlstm, lam_q1, lam_k1, lam_q2, lam_k2, g_subln, w_in_odd, w_out_odd, g_ret,
              ln_mix_g, ln_mix_b, ln_ffn_g, ln_ffn_b, w_router, b_router, w_gate, w_up, w_down):
    f32 = jnp.float32
    bsz, seq = x_prompt.shape[:2]
    dec_seq = x_sample.shape[1]
    past_len = page_table.shape[1] * PAGE_SIZE
    pos_p = jnp.arange(seq)
    pos_s = past_len + jnp.arange(dec_seq)
    names = ('mlstm_c', 'mlstm_n', 'mlstm_m', 'diff_k', 'diff_v', 'dsa_k', 'dsa_v', 'idx_k', 'ret')
    new_p = {k: [] for k in names}
    new_s = {k: [] for k in names}
    xp, xs = x_prompt, x_sample
    for l in range(DEPTH):
        j = l // 2
        if l % 2 == 0:
            wts = (w_in_even[j], w_out_even[j], b_igate[j], b_fgate[j], g_mlstm[j],
                   lam_q1[j], lam_k1[j], lam_q2[j], lam_k2[j], g_subln[j])
            init = (jnp.zeros((bsz, H_A, DH_A, DH_A), f32), jnp.zeros((bsz, H_A, DH_A), f32), jnp.zeros((bsz, H_A), f32))
            yp, st_p = even_mixer(xp, pos_p, l, wts, init, None)
            past = (gather_pages(cache_diff_k[j], page_table), gather_pages(cache_diff_v[j], page_table))
            ys, st_s = even_mixer(xs, pos_s, l, wts, (state_mlstm_c[j], state_mlstm_n[j], state_mlstm_m[j]), past)
            keys = names[:5]
        else:
            wts = (w_in_odd[j], w_out_odd[j], g_ret[j])
            yp, st_p = odd_mixer(xp, pos_p, wts, jnp.zeros((bsz, H_D, DK_D, DV_D), f32), None)
            ys, st_s = odd_mixer(xs, pos_s, wts, state_ret[j], (cache_dsa_k[j], cache_dsa_v[j], cache_idx_k[j], page_table))
            keys = names[5:]
        for k, a, b in zip(keys, st_p, st_s):
            new_p[k].append(a)
            new_s[k].append(b)
        xp = layer_norm(ALPHA * xp + yp, ln_mix_g[l], ln_mix_b[l])
        xs = layer_norm(ALPHA * xs + ys, ln_mix_g[l], ln_mix_b[l])
        moe_w = (w_router, b_router, w_gate[l], w_up[l], w_down[l])
        xp = layer_norm(ALPHA * xp + moe_ffn(xp, moe_w), ln_ffn_g[l], ln_ffn_b[l])
        xs = layer_norm(ALPHA * xs + moe_ffn(xs, moe_w), ln_ffn_g[l], ln_ffn_b[l])
    return (xp, xs,
            jnp.stack(new_p['mlstm_c']), jnp.stack(new_s['mlstm_c']),
            jnp.stack(new_p['mlstm_n']), jnp.stack(new_s['mlstm_n']),
            jnp.stack(new_p['mlstm_m']), jnp.stack(new_s['mlstm_m']),
            jnp.stack(new_p['diff_k']), jnp.stack(new_s['diff_k']),
            jnp.stack(new_p['diff_v']), jnp.stack(new_s['diff_v']),
            jnp.stack(new_p['dsa_k']), jnp.stack(new_s['dsa_k']),
            jnp.stack(new_p['dsa_v']), jnp.stack(new_s['dsa_v']),
            jnp.stack(new_p['idx_k']), jnp.stack(new_s['idx_k']),
            jnp.stack(new_p['ret']), jnp.stack(new_s['ret']))
```

```python
import functools
import math

import jax
import jax.numpy as jnp
from jax import lax
from jax.experimental import pallas as pl
from jax.experimental.pallas import tpu as pltpu

F32 = jnp.float32
I32 = jnp.int32
MXU_DTYPE = jnp.bfloat16

N_GROUPS = 4
TOP_K = 2
TOPK_MAX = 256
CHUNK = 128
ROPE_THETA = 10000.0
EPS = 1e-5

LANES = 128
SUBLANES = 8
VMEM_LIMIT_BYTES = 52 * 1024 * 1024

NEG_INF = float("-inf")


def _mx(x):
    return x.astype(MXU_DTYPE)


def _rnd(x):
    return x.astype(MXU_DTYPE).astype(F32)


def _dot(a, b):
    return jnp.dot(_mx(a), _mx(b), preferred_element_type=F32)


def _dot_nt(a, b):
    return lax.dot_general(_mx(a), _mx(b), (((1,), (1,)), ((), ())), preferred_element_type=F32)


def _params(sem, vmem=VMEM_LIMIT_BYTES):
    return pltpu.CompilerParams(dimension_semantics=sem, vmem_limit_bytes=vmem)


def _iota(shape, axis):
    return lax.broadcasted_iota(I32, shape, axis)


def _sigmoid(x):
    return 1.0 / (1.0 + jnp.exp(-x))


def _row_to_col(row, n):
    eye = _iota((n, n), 0) == _iota((n, n), 1)
    return jnp.sum(jnp.where(eye, jnp.broadcast_to(row, (n, n)), 0.0), axis=1, keepdims=True)


def _head_norm(x, g):
    mu = jnp.mean(x, axis=-1, keepdims=True)
    xc = x - mu
    var = jnp.mean(xc * xc, axis=-1, keepdims=True)
    return xc * lax.rsqrt(var + EPS) * g


def _layer_norm(z, g, b):
    mu = jnp.mean(z, axis=-1, keepdims=True)
    zc = z - mu
    var = jnp.mean(zc * zc, axis=-1, keepdims=True)
    return zc * lax.rsqrt(var + EPS) * g + b


def _rope_apply(a, cos, sin, half):
    if 2 * half == LANES:
        r = pltpu.roll(a, half, 1)
    else:
        lane = _iota(a.shape, 1)
        r = jnp.where((lane % (2 * half)) < half, pltpu.roll(a, LANES - half, 1), pltpu.roll(a, half, 1))
    return a * cos + r * sin


def _mm_kernel(x_ref, w_ref, *rest, rope_half):
    o_ref = rest[-1]
    acc = jnp.dot(x_ref[...], w_ref[...], preferred_element_type=F32)
    if rope_half:
        cos = rest[0][...]
        sin = rest[1][...]
        n_out = o_ref.shape[1]
        for j in range(acc.shape[1] // LANES):
            res = _rope_apply(acc[:, j * LANES:(j + 1) * LANES], cos, sin, rope_half)
            nw = min(LANES, n_out - j * LANES)
            o_ref[:, j * LANES:j * LANES + nw] = res[:, :nw].astype(o_ref.dtype)
    else:
        o_ref[...] = acc.astype(o_ref.dtype)


def _mm(x, w, rope=None, n_out=None, out_dtype=F32):
    m, k = x.shape
    n = w.shape[1]
    n_out = n if n_out is None else n_out
    tm = min(m, 512)
    tn = min(n, 512)
    assert m % tm == 0 and n % tn == 0 and (n_out == n or n == tn)
    in_specs = [pl.BlockSpec((tm, k), lambda i, j: (i, 0)), pl.BlockSpec((k, tn), lambda i, j: (0, j))]
    args = [x, w]
    half = 0
    if rope is not None:
        cos, sin, half = rope
        nt = cos.shape[0] // tm
        assert cos.shape[0] % tm == 0
        in_specs += [pl.BlockSpec((tm, LANES), lambda i, j: (i % nt, 0))] * 2
        args += [cos, sin]
    return pl.pallas_call(
        functools.partial(_mm_kernel, rope_half=half),
        grid=(m // tm, n // tn),
        in_specs=in_specs,
        out_specs=pl.BlockSpec((tm, min(tn, n_out)), lambda i, j: (i, j)),
        out_shape=jax.ShapeDtypeStruct((m, n_out), out_dtype),
        compiler_params=_params(("parallel", "parallel")),
        name="proj",
    )(*args)


def _mm_nt_kernel(w_ref, x_ref, o_ref):
    o_ref[...] = lax.dot_general(w_ref[...], x_ref[...], (((1,), (1,)), ((), ())), preferred_element_type=F32)


def _mm_nt(w_t, x):
    n, k = w_t.shape
    m = x.shape[0]
    tm = min(m, 512)
    return pl.pallas_call(
        _mm_nt_kernel,
        grid=(m // tm,),
        in_specs=[pl.BlockSpec((n, k), lambda i: (0, 0)), pl.BlockSpec((tm, k), lambda i: (i, 0))],
        out_specs=pl.BlockSpec((n, tm), lambda i: (0, i)),
        out_shape=jax.ShapeDtypeStruct((n, m), F32),
        compiler_params=_params(("parallel",)),
        name="proj_t",
    )(w_t, x)


def _proj_ln_kernel(a_ref, b_ref, wa_ref, wb_ref, x_ref, g_ref, bt_ref, xo_ref, xb_ref, *, alpha):
    y = jnp.dot(a_ref[...], wa_ref[...], preferred_element_type=F32)
    y = y + jnp.dot(b_ref[...], wb_ref[...], preferred_element_type=F32)
    out = _layer_norm(alpha * x_ref[...] + y, g_ref[...], bt_ref[...])
    xo_ref[...] = out
    xb_ref[...] = out.astype(xb_ref.dtype)


def _proj_ln(a, b, w_out, x, g, bt, alpha):
    m, d = x.shape
    ka = a.shape[1]
    tm = min(m, 256)
    wa, wb = w_out[:ka], w_out[ka:]
    row = lambda i: (i, 0)
    fixed = lambda i: (0, 0)
    return pl.pallas_call(
        functools.partial(_proj_ln_kernel, alpha=alpha),
        grid=(m // tm,),
        in_specs=[pl.BlockSpec((tm, ka), row), pl.BlockSpec((tm, b.shape[1]), row),
                  pl.BlockSpec(wa.shape, fixed), pl.BlockSpec(wb.shape, fixed),
                  pl.BlockSpec((tm, d), row), pl.BlockSpec((1, d), fixed), pl.BlockSpec((1, d), fixed)],
        out_specs=[pl.BlockSpec((tm, d), row), pl.BlockSpec((tm, d), row)],
        out_shape=[jax.ShapeDtypeStruct((m, d), F32), jax.ShapeDtypeStruct((m, d), MXU_DTYPE)],
        compiler_params=_params(("parallel",)),
        name="out_proj_ln",
    )(a, b, wa, wb, x, g.reshape(1, d), bt.reshape(1, d))


def _lane_scan(x, op, fill):
    n = x.shape[1]
    lane = _iota(x.shape, 1)
    s = 1
    while s < n:
        x = op(x, jnp.where(lane >= s, pltpu.roll(x, s, 1), fill))
        s *= 2
    return x


def _mlstm_kernel(q_ref, k_ref, v_ref, o_ref, ig_ref, fg_ref, big_ref, bfg_ref, g_ref, c0_ref, n0_ref, m0_ref,
                  h_ref, c_ref, n_ref, m_ref, c_s, n_s, m_s, *, valid_len, scale):
    ci = pl.program_id(2)

    @pl.when(ci == 0)
    def _():
        c_s[...] = c0_ref[...]
        n_s[...] = n0_ref[...]
        m_s[...] = m0_ref[...]

    L = q_ref.shape[0]
    q = q_ref[...]
    k = k_ref[...] * scale
    v = v_ref[...]
    ig = ig_ref[...] + big_ref[...]
    fpre = fg_ref[...] + bfg_ref[...]
    lf = jnp.minimum(fpre, 0.0) - jnp.log1p(jnp.exp(-jnp.abs(fpre)))
    if valid_len < L:
        live = _iota((1, L), 1) < valid_len
        ig = jnp.where(live, ig, NEG_INF)
        lf = jnp.where(live, lf, 0.0)
    m_prev = m_s[...]
    a_row = _lane_scan(lf, jnp.add, 0.0)
    b_row = ig - a_row
    run_row = jnp.maximum(_lane_scan(b_row, jnp.maximum, NEG_INF), m_prev)
    run_col = _row_to_col(run_row, L)
    causal = _iota((L, L), 1) <= _iota((L, L), 0)
    dmat = jnp.where(causal, jnp.exp(b_row - run_col), 0.0)
    w_inter = jnp.exp(m_prev - run_col)
    s = _dot_nt(q, k) * dmat
    c_prev = c_s[...]
    n_prev = n_s[...]
    num = w_inter * _dot(q, c_prev) + _dot(s, v)
    qn = jnp.sum(_rnd(q) * _rnd(n_prev), axis=1, keepdims=True)
    den = w_inter * qn + jnp.sum(s, axis=1, keepdims=True)
    a_col = _row_to_col(a_row, L)
    m_t = a_col + run_col
    h = num / jnp.maximum(jnp.abs(den), jnp.exp(-m_t))
    h = _head_norm(h, g_ref[...]) * _sigmoid(o_ref[...])
    h_ref[...] = h.astype(h_ref.dtype)
    a_last = a_row[:, L - 1:L]
    m_new = a_last + run_row[:, L - 1:L]
    w_c = jnp.exp(a_last + m_prev - m_new)
    ws_row = jnp.exp(a_last - a_row + ig - m_new)
    ws_col = _row_to_col(ws_row, L)
    kw = k * ws_col
    c_new = w_c * c_prev + _dot(kw.T, v)
    n_new = w_c * n_prev + jnp.sum(_rnd(ws_col) * _rnd(k), axis=0, keepdims=True)
    c_s[...] = c_new
    n_s[...] = n_new
    m_s[...] = m_new

    @pl.when(ci == pl.num_programs(2) - 1)
    def _():
        c_ref[...] = c_new
        n_ref[...] = n_new
        m_ref[...] = m_new


def _mlstm(z, gates_t, b_ig, b_fg, g_mh, c0, n0, m0, valid_len):
    bsz, t, _ = z.shape
    nh, dh = g_mh.shape
    L = CHUNK
    nc = t // L
    assert dh == LANES and t % L == 0
    col = lambda off: pl.BlockSpec((None, L, dh), lambda b, h, c: (b, c, off + h))
    gate = lambda off: pl.BlockSpec((None, None, 1, L), lambda b, h, c: (b, off + h, 0, c))
    per_head = lambda shp: pl.BlockSpec((None,) + shp, lambda b, h, c: (h, 0, 0))
    state = lambda shp: pl.BlockSpec((None, None) + shp, lambda b, h, c: (b, h, 0, 0))
    return pl.pallas_call(
        functools.partial(_mlstm_kernel, valid_len=valid_len, scale=dh ** -0.5),
        grid=(bsz, nh, nc),
        in_specs=[col(0), col(nh), col(2 * nh), col(3 * nh), gate(0), gate(nh),
                  per_head((1, 1)), per_head((1, 1)), per_head((1, dh)),
                  state((dh, dh)), state((1, dh)), state((1, 1))],
        out_specs=[pl.BlockSpec((None, L, dh), lambda b, h, c: (b, c, h)),
                   state((dh, dh)), state((1, dh)), state((1, 1))],
        out_shape=[jax.ShapeDtypeStruct((bsz, t, nh * dh), MXU_DTYPE),
                   jax.ShapeDtypeStruct((bsz, nh, dh, dh), F32),
                   jax.ShapeDtypeStruct((bsz, nh, 1, dh), F32),
                   jax.ShapeDtypeStruct((bsz, nh, 1, 1), F32)],
        scratch_shapes=[pltpu.VMEM((dh, dh), F32), pltpu.VMEM((1, dh), F32), pltpu.VMEM((1, 1), F32)],
        compiler_params=_params(("parallel", "parallel", "arbitrary")),
        name="mlstm",
    )(z, z, z, z, gates_t, gates_t, b_ig.reshape(nh, 1, 1), b_fg.reshape(nh, 1, 1), g_mh.reshape(nh, 1, dh),
      c0, n0.reshape(bsz, nh, 1, dh), m0.reshape(bsz, nh, 1, 1))


def _ret_kernel(q_ref, k_ref, v_ref, gd_ref, lg_ref, g_ref, s0_ref, o_ref, s_ref, s_s, *, true_len, scale):
    ci = pl.program_id(2)

    @pl.when(ci == 0)
    def _():
        s_s[...] = s0_ref[...]

    L = q_ref.shape[0]
    q = q_ref[...]
    k = k_ref[...] * scale
    v = v_ref[...]
    lg = lg_ref[...]
    t_col = _iota((L, 1), 0).astype(F32)
    diff = (_iota((L, L), 0) - _iota((L, L), 1)).astype(F32)
    causal = diff >= 0.0
    decay = jnp.where(causal, jnp.exp(jnp.where(causal, diff, 0.0) * lg), 0.0)
    s_prev = s_s[...]
    inner = _dot_nt(q, k) * decay
    o = _dot(inner, v) + jnp.exp((t_col + 1.0) * lg) * _dot(q, s_prev)
    chunk_len = float(min(L, true_len))
    ws_col = jnp.where(t_col < chunk_len, jnp.exp((chunk_len - 1.0 - t_col) * lg), 0.0)
    s_new = jnp.exp(chunk_len * lg) * s_prev + _dot((k * ws_col).T, v)
    s_s[...] = s_new
    gd = gd_ref[...]
    o_ref[...] = (_head_norm(o, g_ref[...]) * (gd * _sigmoid(gd))).astype(o_ref.dtype)

    @pl.when(ci == pl.num_programs(2) - 1)
    def _():
        s_ref[...] = s_new


def _retention(qk, vg, log_gamma, g_ret, s0, true_len):
    bsz, t, _ = qk.shape
    nh, dv = g_ret.shape
    dk = s0.shape[2]
    L = CHUNK
    nc = t // L
    assert dk % LANES == 0 and dv % LANES == 0
    return pl.pallas_call(
        functools.partial(_ret_kernel, true_len=true_len, scale=dk ** -0.5),
        grid=(bsz, nh, nc),
        in_specs=[pl.BlockSpec((None, L, dk), lambda b, h, c: (b, c, h)),
                  pl.BlockSpec((None, L, dk), lambda b, h, c: (b, c, nh + h)),
                  pl.BlockSpec((None, L, dv), lambda b, h, c: (b, c, h)),
                  pl.BlockSpec((None, L, dv), lambda b, h, c: (b, c, nh + h)),
                  pl.BlockSpec((None, 1, 1), lambda b, h, c: (h, 0, 0)),
                  pl.BlockSpec((None, 1, dv), lambda b, h, c: (h, 0, 0)),
                  pl.BlockSpec((None, None, dk, dv), lambda b, h, c: (b, h, 0, 0))],
        out_specs=[pl.BlockSpec((None, L, dv), lambda b, h, c: (b, c, h)),
                   pl.BlockSpec((None, None, dk, dv), lambda b, h, c: (b, h, 0, 0))],
        out_shape=[jax.ShapeDtypeStruct((bsz, t, nh * dv), MXU_DTYPE),
                   jax.ShapeDtypeStruct((bsz, nh, dk, dv), F32)],
        scratch_shapes=[pltpu.VMEM((dk, dv), F32)],
        compiler_params=_params(("parallel", "parallel", "arbitrary")),
        name="retention",
    )(qk, qk, vg, vg, log_gamma.reshape(nh, 1, 1), g_ret.reshape(nh, 1, dv), s0)


def _lambda(lq1, lk1, lq2, lk2, lam_init):
    s1 = jnp.sum(lq1 * lk1, axis=1, keepdims=True)
    s2 = jnp.sum(lq2 * lk2, axis=1, keepdims=True)
    return jnp.exp(s1) - jnp.exp(s2) + lam_init


def _rms_sub(o, g, lam_init):
    return o * lax.rsqrt(jnp.mean(o * o, axis=-1, keepdims=True) + EPS) * g * (1.0 - lam_init)


def _diff_kernel(q_ref, k_ref, v_ref, lq1_ref, lk1_ref, lq2_ref, lk2_ref, g_ref, o_ref, m_s, l_s, acc_s,
                 *, lam_init, scale):
    qi = pl.program_id(2)
    kj = pl.program_id(3)
    tq, dv = q_ref.shape
    tk = k_ref.shape[0]
    dh = dv // 2

    @pl.when(kj == 0)
    def _():
        m_s[...] = jnp.full(m_s.shape, NEG_INF, F32)
        l_s[...] = jnp.zeros(l_s.shape, F32)
        acc_s[...] = jnp.zeros(acc_s.shape, F32)

    @pl.when(kj <= qi)
    def _():
        qpos = qi * tq + _iota((tq, tk), 0)
        kpos = kj * tk + _iota((tq, tk), 1)
        ok = kpos <= qpos
        vb = _mx(v_ref[...])
        for c in range(2):
            s = _dot_nt(q_ref[:, c * dh:(c + 1) * dh], k_ref[:, c * dh:(c + 1) * dh]) * scale
            s = jnp.where(ok, s, NEG_INF)
            m_old = m_s[c]
            m_new = jnp.maximum(m_old, jnp.max(s, axis=1, keepdims=True))
            alpha = jnp.exp(m_old - m_new)
            p = jnp.exp(s - m_new)
            l_s[c] = alpha * l_s[c] + jnp.sum(p, axis=1, keepdims=True)
            acc_s[c] = alpha * acc_s[c] + jnp.dot(_mx(p), vb, preferred_element_type=F32)
            m_s[c] = m_new

    @pl.when(kj == qi)
    def _():
        lam = _lambda(lq1_ref[...], lk1_ref[...], lq2_ref[...], lk2_ref[...], lam_init)
        o = acc_s[0] / l_s[0] - lam * (acc_s[1] / l_s[1])
        o_ref[...] = _rms_sub(o, g_ref[...], lam_init).astype(o_ref.dtype)


def _diff_attn(qb, kb, vb, lams, g_sub, lam_init, nh):
    bsz, t, w = qb.shape
    dv = w // nh
    dh = dv // 2
    tq = tk = min(t, 256)
    nq = t // tq
    vec = pl.BlockSpec((1, dh), lambda b, h, i, j: (0, 0))
    kv = pl.BlockSpec((None, tk, dv), lambda b, h, i, j: (b, jnp.minimum(j, i), h))
    return pl.pallas_call(
        functools.partial(_diff_kernel, lam_init=lam_init, scale=dh ** -0.5),
        grid=(bsz, nh, nq, nq),
        in_specs=[pl.BlockSpec((None, tq, dv), lambda b, h, i, j: (b, i, h)), kv, kv,
                  vec, vec, vec, vec, pl.BlockSpec((1, dv), lambda b, h, i, j: (0, 0))],
        out_specs=pl.BlockSpec((None, tq, dv), lambda b, h, i, j: (b, i, h)),
        out_shape=jax.ShapeDtypeStruct((bsz, t, w), MXU_DTYPE),
        scratch_shapes=[pltpu.VMEM((2, tq, 1), F32), pltpu.VMEM((2, tq, 1), F32), pltpu.VMEM((2, tq, dv), F32)],
        compiler_params=_params(("parallel", "parallel", "parallel", "arbitrary")),
        name="diff_attn",
    )(qb, kb, vb, *[x.reshape(1, dh) for x in lams], g_sub.reshape(1, dv))


def _f32_key(x):
    i = lax.bitcast_convert_type(x, I32)
    return i ^ ((i >> 31) & 0x7FFFFFFF)


KEY_NEG_INF = -2139095041
I32_MIN = -2 ** 31


def _strict_upper(n):
    return jnp.where(_iota((n, n), 0) < _iota((n, n), 1), 1.0, 0.0).astype(MXU_DTYPE)


def _dsa_kernel(qc_ref, kc_ref, vc_ref, qi_ref, ki_ref, wi_ref, o_ref, key_s, bias_s,
                *, topk, n_idx_heads, scale, wi_scale):
    qb = pl.program_id(1)
    h = pl.program_id(2)
    tq, dh = qc_ref.shape
    ck = key_s.shape[2]
    d_i = ki_ref.shape[1]
    q0 = qb * tq
    n_chunks = (q0 + tq + ck - 1) // ck

    @pl.when(h == 0)
    def _select():
        qi = qi_ref[...]
        wi = wi_ref[...] * wi_scale
        qpos = q0 + _iota((tq, ck), 0)

        def score_chunk(c, carry):
            start = pl.multiple_of(c * ck, ck)
            kic = ki_ref[pl.ds(start, ck), :]
            acc = jnp.zeros((tq, ck), F32)
            for hh in range(n_idx_heads):
                sc = jnp.maximum(_dot_nt(qi[:, hh * d_i:(hh + 1) * d_i], kic), 0.0)
                acc = acc + _rnd(wi[:, hh:hh + 1]) * _rnd(sc)
            acc = acc + 0.0
            kpos = c * ck + _iota((tq, ck), 1)
            key_s[c] = jnp.where(kpos <= qpos, _f32_key(acc), KEY_NEG_INF)
            return carry

        lax.fori_loop(0, n_chunks, score_chunk, 0)

        def count(pred):
            def body(c, acc):
                hit = jnp.where(pred(key_s[c]), 1.0, 0.0)
                part = hit[:, 0:LANES]
                for j in range(1, ck // LANES):
                    part = part + hit[:, j * LANES:(j + 1) * LANES]
                return acc + part
            acc = lax.fori_loop(0, n_chunks, body, jnp.zeros((tq, LANES), F32))
            return jnp.sum(acc, axis=1, keepdims=True)

        def bit_step(it, prefix):
            bit = jnp.left_shift(jnp.int32(1), 31 - it)
            cand = (prefix | bit) ^ I32_MIN
            cnt = count(lambda kk: kk >= cand)
            return jnp.where(cnt >= float(topk), prefix | bit, prefix)

        prefix = lax.fori_loop(0, 32, bit_step, jnp.zeros((tq, 1), I32))
        thr = prefix ^ I32_MIN
        need = float(topk) - count(lambda kk: kk > thr)
        upper = _strict_upper(LANES)

        def mask_chunk(c, taken):
            kk = key_s[c]
            cols = []
            for j in range(ck // LANES):
                kj = kk[:, j * LANES:(j + 1) * LANES]
                eq = kj == thr
                eqf = jnp.where(eq, 1.0, 0.0)
                before = taken + jnp.dot(_mx(eqf), upper, preferred_element_type=F32)
                sel = ((kj > thr) | (eq & (before < need))) & (kj > KEY_NEG_INF)
                cols.append(jnp.where(sel, 0.0, NEG_INF))
                taken = taken + jnp.sum(eqf, axis=1, keepdims=True)
            bias_s[c] = jnp.concatenate(cols, axis=1)
            return taken

        lax.fori_loop(0, n_chunks, mask_chunk, jnp.zeros((tq, 1), F32))

    q = _mx(qc_ref[...])

    def attend(c, carry):
        m_old, l_old, acc = carry
        start = pl.multiple_of(c * ck, ck)
        s = _dot_nt(q, kc_ref[pl.ds(start, ck), :]) * scale + bias_s[c]
        m_new = jnp.maximum(m_old, jnp.max(s, axis=1, keepdims=True))
        m_safe = jnp.where(m_new == NEG_INF, 0.0, m_new)
        alpha = jnp.exp(m_old - m_safe)
        p = jnp.exp(s - m_safe)
        l_new = alpha * l_old + jnp.sum(p, axis=1, keepdims=True)
        acc = alpha * acc + _dot(p, vc_ref[pl.ds(start, ck), :])
        return m_new, l_new, acc

    init = (jnp.full((tq, 1), NEG_INF, F32), jnp.zeros((tq, 1), F32), jnp.zeros((tq, dh), F32))
    _, l_fin, acc = lax.fori_loop(0, n_chunks, attend, init)
    o_ref[...] = (acc / l_fin).astype(o_ref.dtype)


def _dsa_attn(qc, kc, vc, qi, ki, wi, nh, n_idx_heads):
    bsz, t, w = qc.shape
    dh = w // nh
    d_i = ki.shape[2]
    tq = min(t, 128)
    ck = min(t, 512)
    topk = min(TOPK_MAX, t // 4)
    return pl.pallas_call(
        functools.partial(_dsa_kernel, topk=topk, n_idx_heads=n_idx_heads, scale=dh ** -0.5,
                          wi_scale=(n_idx_heads * d_i) ** -0.5),
        grid=(bsz, t // tq, nh),
        in_specs=[pl.BlockSpec((None, tq, dh), lambda b, i, h: (b, i, h)),
                  pl.BlockSpec((None, t, dh), lambda b, i, h: (b, 0, h)),
                  pl.BlockSpec((None, t, dh), lambda b, i, h: (b, 0, h)),
                  pl.BlockSpec((None, tq, qi.shape[2]), lambda b, i, h: (b, i, 0)),
                  pl.BlockSpec((None, t, d_i), lambda b, i, h: (b, 0, 0)),
                  pl.BlockSpec((None, tq, wi.shape[2]), lambda b, i, h: (b, i, 0))],
        out_specs=pl.BlockSpec((None, tq, dh), lambda b, i, h: (b, i, h)),
        out_shape=jax.ShapeDtypeStruct((bsz, t, w), MXU_DTYPE),
        scratch_shapes=[pltpu.VMEM((t // ck, tq, ck), I32), pltpu.VMEM((t // ck, tq, ck), F32)],
        compiler_params=_params(("parallel", "parallel", "arbitrary")),
        name="dsa_attn",
    )(qc, kc, vc, qi, ki, wi)


def _paged_qk_kernel(pt_ref, q_ref, k_ref, o_ref):
    o_ref[...] = _dot_nt(q_ref[...], k_ref[...])


def _paged_qk(q_rows, cache, layer, page_table):
    bd, r, d = q_rows.shape
    page = cache.shape[2]
    n_pages = page_table.shape[1]
    return pl.pallas_call(
        _paged_qk_kernel,
        grid_spec=pltpu.PrefetchScalarGridSpec(
            num_scalar_prefetch=1, grid=(bd, n_pages),
            in_specs=[pl.BlockSpec((None, r, d), lambda b, p, pt: (b, 0, 0)),
                      pl.BlockSpec((None, None, page, d), lambda b, p, pt: (layer, pt[b, p], 0, 0))],
            out_specs=pl.BlockSpec((None, r, page), lambda b, p, pt: (b, 0, p))),
        out_shape=jax.ShapeDtypeStruct((bd, r, n_pages * page), F32),
        compiler_params=_params(("parallel", "arbitrary")),
        name="paged_qk",
    )(page_table, q_rows, cache)


def _paged_pv_kernel(pt_ref, a_ref, v_ref, o_ref):
    @pl.when(pl.program_id(1) == 0)
    def _():
        o_ref[...] = jnp.zeros(o_ref.shape, F32)

    o_ref[...] += _dot(a_ref[...], v_ref[...])


def _paged_pv(a, cache, layer, page_table):
    bd, r, _ = a.shape
    page, d = cache.shape[2], cache.shape[3]
    n_pages = page_table.shape[1]
    return pl.pallas_call(
        _paged_pv_kernel,
        grid_spec=pltpu.PrefetchScalarGridSpec(
            num_scalar_prefetch=1, grid=(bd, n_pages),
            in_specs=[pl.BlockSpec((None, r, page), lambda b, p, pt: (b, 0, p)),
                      pl.BlockSpec((None, None, page, d), lambda b, p, pt: (layer, pt[b, p], 0, 0))],
            out_specs=pl.BlockSpec((None, r, d), lambda b, p, pt: (b, 0, 0))),
        out_shape=jax.ShapeDtypeStruct((bd, r, d), F32),
        compiler_params=_params(("parallel", "arbitrary")),
        name="paged_pv",
    )(page_table, a, cache)


def _block_rows(x, blocks):
    bd, d = x.shape
    r = len(blocks)
    wdt = d // r
    lane_blk = jnp.arange(d) // wdt
    mask = (lane_blk[None, :] == jnp.asarray(blocks)[:, None]).astype(x.dtype)
    return x[:, None, :] * mask[None]


def _diff_softmax_kernel(s_ref, q_ref, kn_ref, lq1_ref, lk1_ref, lq2_ref, lk2_ref, a_ref, an_ref, *, lam_init, scale):
    nh = a_ref.shape[0]
    s = s_ref[...] * scale
    s_new = jnp.sum(_rnd(q_ref[...]) * _rnd(kn_ref[...]), axis=1, keepdims=True) * scale
    m = jnp.maximum(jnp.max(s, axis=1, keepdims=True), s_new)
    e = jnp.exp(s - m)
    e_new = jnp.exp(s_new - m)
    z = jnp.sum(e, axis=1, keepdims=True) + e_new
    lam = _lambda(lq1_ref[...], lk1_ref[...], lq2_ref[...], lk2_ref[...], lam_init)
    p = e / z
    p_new = e_new / z
    a_ref[...] = p[:nh] - lam * p[nh:]
    an_ref[...] = jnp.broadcast_to(p_new[:nh] - lam * p_new[nh:], an_ref.shape)


def _diff_final_kernel(acc_ref, an_ref, vn_ref, g_ref, o_ref, *, lam_init):
    nh = acc_ref.shape[0]
    dv = acc_ref.shape[1] // nh
    full = acc_ref[...] + _rnd(an_ref[:, 0:1]) * _rnd(vn_ref[...])
    for h in range(nh):
        o = full[h:h + 1, h * dv:(h + 1) * dv]
        o_ref[:, h * dv:(h + 1) * dv] = _rms_sub(o, g_ref[...], lam_init).astype(o_ref.dtype)


def _diff_decode(qb, kb_new, vb_new, cache_k, cache_v, layer, page_table, lams, g_sub, lam_init, nh):
    bd, w = qb.shape
    dv = w // nh
    dh = dv // 2
    n_pool, page = cache_k.shape[1], cache_k.shape[2]
    ck = cache_k.reshape(cache_k.shape[0], n_pool, page, w)
    cv = cache_v.reshape(cache_v.shape[0], n_pool, page, w)
    blocks = tuple(h * 2 + c for c in range(2) for h in range(nh))
    q_rows = _block_rows(qb, blocks)
    s = _paged_qk(q_rows, ck, layer, page_table)
    s_len = s.shape[2]
    per_b = lambda shp: pl.BlockSpec((None,) + shp, lambda b: (b, 0, 0))
    vec = pl.BlockSpec((1, dh), lambda b: (0, 0))
    a, a_new = pl.pallas_call(
        functools.partial(_diff_softmax_kernel, lam_init=lam_init, scale=dh ** -0.5),
        grid=(bd,),
        in_specs=[per_b((2 * nh, s_len)), per_b((2 * nh, w)), per_b((1, w)), vec, vec, vec, vec],
        out_specs=[per_b((nh, s_len)), per_b((nh, LANES))],
        out_shape=[jax.ShapeDtypeStruct((bd, nh, s_len), F32), jax.ShapeDtypeStruct((bd, nh, LANES), F32)],
        compiler_params=_params(("parallel",)),
        name="diff_decode_softmax",
    )(s, q_rows, kb_new.reshape(bd, 1, w), *[x.reshape(1, dh) for x in lams])
    acc = _paged_pv(a, cv, layer, page_table)
    return pl.pallas_call(
        functools.partial(_diff_final_kernel, lam_init=lam_init),
        grid=(bd,),
        in_specs=[per_b((nh, w)), per_b((nh, LANES)), per_b((1, w)), pl.BlockSpec((1, dv), lambda b: (0, 0))],
        out_specs=per_b((1, w)),
        out_shape=jax.ShapeDtypeStruct((bd, 1, w), MXU_DTYPE),
        compiler_params=_params(("parallel",)),
        name="diff_decode_final",
    )(acc, a_new, vb_new.reshape(bd, 1, w), g_sub.reshape(1, dv)).reshape(bd, w)


def _idx_score_kernel(pt_ref, qi_ref, wi_ref, ki_ref, o_ref, *, wi_scale):
    sc = jnp.maximum(_dot_nt(qi_ref[...], ki_ref[...]), 0.0)
    o_ref[...] = jnp.sum(_rnd(wi_ref[...] * wi_scale) * _rnd(sc), axis=0, keepdims=True) + 0.0


def _select_kernel(sc_ref, qi_ref, wi_ref, kn_ref, mask_ref, mnew_ref, *, topk, wi_scale):
    npg, page = sc_ref.shape
    sc_new = jnp.maximum(jnp.sum(_rnd(qi_ref[...]) * _rnd(kn_ref[...]), axis=1, keepdims=True), 0.0)
    s_new = jnp.sum(_rnd(wi_ref[...] * wi_scale) * _rnd(sc_new), axis=0, keepdims=True) + 0.0
    keys = _f32_key(sc_ref[...])
    key_new = _f32_key(s_new)

    def total(x):
        return jnp.sum(jnp.sum(x, axis=1, keepdims=True), axis=0, keepdims=True)

    def count(pred):
        return total(jnp.where(pred(keys), 1.0, 0.0)) + jnp.where(pred(key_new), 1.0, 0.0)

    def bit_step(it, prefix):
        bit = jnp.left_shift(jnp.int32(1), 31 - it)
        cand = (prefix | bit) ^ I32_MIN
        return jnp.where(count(lambda kk: kk >= cand) >= float(topk), prefix | bit, prefix)

    thr = lax.fori_loop(0, 32, bit_step, jnp.zeros((1, 1), I32)) ^ I32_MIN
    need = float(topk) - count(lambda kk: kk > thr)
    eq = keys == thr
    eqf = jnp.where(eq, 1.0, 0.0)
    in_row = jnp.dot(_mx(eqf), _strict_upper(page), preferred_element_type=F32)
    row_tot = jnp.broadcast_to(jnp.sum(eqf, axis=1, keepdims=True), (npg, page))
    strict_lower = jnp.where(_iota((npg, npg), 1) < _iota((npg, npg), 0), 1.0, 0.0).astype(MXU_DTYPE)
    before = in_row + jnp.dot(strict_lower, _mx(row_tot), preferred_element_type=F32)
    sel = ((keys > thr) | (eq & (before < need))) & (keys > KEY_NEG_INF)
    mask_ref[...] = jnp.where(sel, 1.0, 0.0)
    sel_new = ((key_new > thr) | ((key_new == thr) & (total(eqf) < need))) & (key_new > KEY_NEG_INF)
    mnew_ref[...] = jnp.broadcast_to(jnp.where(sel_new, 1.0, 0.0), mnew_ref.shape)


def _dsa_softmax_kernel(s_ref, mask_ref, mnew_ref, q_ref, kn_ref, p_ref, pn_ref, *, scale):
    s = jnp.where(mask_ref[...] > 0.0, s_ref[...] * scale, NEG_INF)
    s_new = jnp.sum(_rnd(q_ref[...]) * _rnd(kn_ref[...]), axis=1, keepdims=True) * scale
    s_new = jnp.where(mnew_ref[:, 0:1] > 0.0, s_new, NEG_INF)
    m = jnp.maximum(jnp.max(s, axis=1, keepdims=True), s_new)
    e = jnp.exp(s - m)
    e_new = jnp.exp(s_new - m)
    z = jnp.sum(e, axis=1, keepdims=True) + e_new
    p_ref[...] = e / z
    pn_ref[...] = jnp.broadcast_to(e_new / z, pn_ref.shape)


def _dsa_final_kernel(acc_ref, pn_ref, vn_ref, o_ref):
    nh = acc_ref.shape[0]
    dh = acc_ref.shape[1] // nh
    full = acc_ref[...] + _rnd(pn_ref[:, 0:1]) * _rnd(vn_ref[...])
    for h in range(nh):
        o_ref[:, h * dh:(h + 1) * dh] = full[h:h + 1, h * dh:(h + 1) * dh].astype(o_ref.dtype)


def _dsa_decode(qc, kc_new, vc_new, qi, ki_new, wi, cache_k, cache_v, cache_i, layer, page_table, nh, n_idx_heads):
    bd, w = qc.shape
    dh = w // nh
    d_i = ki_new.shape[1]
    n_pool, page = cache_k.shape[1], cache_k.shape[2]
    n_pages = page_table.shape[1]
    past = n_pages * page
    topk = min(TOPK_MAX, (past + 1) // 4)
    qi3 = qi.reshape(bd, n_idx_heads, d_i)
    wi3 = wi[:, :n_idx_heads].reshape(bd, n_idx_heads, 1)
    wi_scale = (n_idx_heads * d_i) ** -0.5
    scores = pl.pallas_call(
        functools.partial(_idx_score_kernel, wi_scale=wi_scale),
        grid_spec=pltpu.PrefetchScalarGridSpec(
            num_scalar_prefetch=1, grid=(bd, n_pages),
            in_specs=[pl.BlockSpec((None, n_idx_heads, d_i), lambda b, p, pt: (b, 0, 0)),
                      pl.BlockSpec((None, n_idx_heads, 1), lambda b, p, pt: (b, 0, 0)),
                      pl.BlockSpec((None, None, page, d_i), lambda b, p, pt: (layer, pt[b, p], 0, 0))],
            out_specs=pl.BlockSpec((None, None, 1, page), lambda b, p, pt: (b, p, 0, 0))),
        out_shape=jax.ShapeDtypeStruct((bd, n_pages, 1, page), F32),
        compiler_params=_params(("parallel", "arbitrary")),
        name="idx_scores",
    )(page_table, qi3, wi3, cache_i)
    per_b = lambda shp: pl.BlockSpec((None,) + shp, lambda b: (b, 0, 0))
    mask, mask_new = pl.pallas_call(
        functools.partial(_select_kernel, topk=topk, wi_scale=wi_scale),
        grid=(bd,),
        in_specs=[per_b((n_pages, page)), per_b((n_idx_heads, d_i)), per_b((n_idx_heads, 1)), per_b((1, d_i))],
        out_specs=[per_b((n_pages, page)), per_b((1, LANES))],
        out_shape=[jax.ShapeDtypeStruct((bd, n_pages, page), F32), jax.ShapeDtypeStruct((bd, 1, LANES), F32)],
        compiler_params=_params(("parallel",)),
        name="idx_select",
    )(scores.reshape(bd, n_pages, page), qi3, wi3, ki_new.reshape(bd, 1, d_i))
    ck = cache_k.reshape(cache_k.shape[0], n_pool, page, w)
    cv = cache_v.reshape(cache_v.shape[0], n_pool, page, w)
    q_rows = _block_rows(qc, tuple(range(nh)))
    s = _paged_qk(q_rows, ck, layer, page_table)
    p, p_new = pl.pallas_call(
        functools.partial(_dsa_softmax_kernel, scale=dh ** -0.5),
        grid=(bd,),
        in_specs=[per_b((nh, past)), per_b((1, past)), per_b((1, LANES)), per_b((nh, w)), per_b((1, w))],
        out_specs=[per_b((nh, past)), per_b((nh, LANES))],
        out_shape=[jax.ShapeDtypeStruct((bd, nh, past), F32), jax.ShapeDtypeStruct((bd, nh, LANES), F32)],
        compiler_params=_params(("parallel",)),
        name="dsa_decode_softmax",
    )(s, mask.reshape(bd, 1, past), mask_new, q_rows, kc_new.reshape(bd, 1, w))
    acc = _paged_pv(p, cv, layer, page_table)
    return pl.pallas_call(
        _dsa_final_kernel,
        grid=(bd,),
        in_specs=[per_b((nh, w)), per_b((nh, LANES)), per_b((1, w))],
        out_specs=per_b((1, w)),
        out_shape=jax.ShapeDtypeStruct((bd, 1, w), MXU_DTYPE),
        compiler_params=_params(("parallel",)),
        name="dsa_decode_final",
    )(acc, p_new, vc_new.reshape(bd, 1, w)).reshape(bd, w)


def _route_kernel(lg_ref, br_ref, e_ref, g_ref, r_ref, cnt_ref, carry_s, *, n_valid, n_groups):
    i = pl.program_id(0)
    ne, tn = lg_ref.shape
    per = ne // n_groups

    @pl.when(i == 0)
    def _():
        carry_s[...] = jnp.zeros(carry_s.shape, F32)

    aff = _sigmoid(lg_ref[...])
    sel = aff + br_ref[...]
    sub = _iota((per, tn), 0).astype(F32)
    best = None
    for gi in range(n_groups):
        s = sel[gi * per:(gi + 1) * per]
        m1 = jnp.max(s, axis=0, keepdims=True)
        i1 = jnp.min(jnp.where(s == m1, sub, float(per)), axis=0, keepdims=True)
        s2 = jnp.where(sub == i1, NEG_INF, s)
        m2 = jnp.max(s2, axis=0, keepdims=True)
        i2 = jnp.min(jnp.where(s2 == m2, sub, float(per)), axis=0, keepdims=True)
        cand = (m1 + m2, float(gi * per) + i1, float(gi * per) + i2)
        if best is None:
            best = cand
        else:
            better = cand[0] > best[0]
            best = tuple(jnp.where(better, c, b) for c, b in zip(cand, best))
    e0, e1 = best[1].astype(I32), best[2].astype(I32)
    eid = _iota((ne, tn), 0)
    valid = (i * tn + _iota((1, tn), 1)) < n_valid
    oh0 = (eid == e0) & valid
    oh1 = (eid == e1) & valid
    a0 = jnp.sum(jnp.where(eid == e0, aff, 0.0), axis=0, keepdims=True)
    a1 = jnp.sum(jnp.where(eid == e1, aff, 0.0), axis=0, keepdims=True)
    tot = a0 + a1
    e_ref[...] = jnp.concatenate([e0, e1], axis=0)
    g_ref[...] = jnp.concatenate([a0 / tot, a1 / tot], axis=0)
    oh = jnp.where(oh0, 1.0, 0.0) + jnp.where(oh1, 1.0, 0.0)
    before = carry_s[:, 0:1] + jnp.dot(_mx(oh), _strict_upper(tn), preferred_element_type=F32)
    r0 = jnp.sum(jnp.where(oh0, before, 0.0), axis=0, keepdims=True)
    r1 = jnp.sum(jnp.where(oh1, before, 0.0), axis=0, keepdims=True)
    r_ref[...] = jnp.concatenate([r0, r1], axis=0).astype(I32)
    carry_s[...] = carry_s[...] + jnp.sum(oh, axis=1, keepdims=True)
    cnt_ref[...] = carry_s[...]


def _route(logits_t, b_router, n_valid):
    ne, mp = logits_t.shape
    tn = min(mp, 256)
    tok = lambda i: (0, i)
    return pl.pallas_call(
        functools.partial(_route_kernel, n_valid=n_valid, n_groups=N_GROUPS),
        grid=(mp // tn,),
        in_specs=[pl.BlockSpec((ne, tn), tok), pl.BlockSpec((ne, 1), lambda i: (0, 0))],
        out_specs=[pl.BlockSpec((TOP_K, tn), tok), pl.BlockSpec((TOP_K, tn), tok), pl.BlockSpec((TOP_K, tn), tok),
                   pl.BlockSpec((ne, LANES), lambda i: (0, 0))],
        out_shape=[jax.ShapeDtypeStruct((TOP_K, mp), I32), jax.ShapeDtypeStruct((TOP_K, mp), F32),
                   jax.ShapeDtypeStruct((TOP_K, mp), I32), jax.ShapeDtypeStruct((ne, LANES), F32)],
        scratch_shapes=[pltpu.VMEM((ne, LANES), F32)],
        compiler_params=_params(("arbitrary",)),
        name="moe_route",
    )(logits_t, b_router.reshape(ne, 1))


def _row_copy(src, s_row, dst, d_row, sem):
    return pltpu.make_async_copy(src.at[pl.ds(s_row, 1)], dst.at[pl.ds(d_row, 1)], sem)


def _dispatch_kernel(dest_ref, x_hbm, buf_in, buf_hbm, sem, *, n_tok, tb):
    del buf_in
    base = pl.program_id(0) * tb

    def issue(r, carry):
        n = base + r
        for kk in range(TOP_K):
            _row_copy(x_hbm, n, buf_hbm, dest_ref[kk * n_tok + n], sem).start()
        return carry

    def drain(r, carry):
        for kk in range(TOP_K):
            _row_copy(x_hbm, 0, buf_hbm, 0, sem).wait()
        return carry

    lax.fori_loop(0, tb, issue, 0)
    lax.fori_loop(0, tb, drain, 0)


def _dispatch(dest_flat, x, n_rows):
    n_tok, d = x.shape
    tb = min(n_tok, 128)
    return pl.pallas_call(
        functools.partial(_dispatch_kernel, n_tok=n_tok, tb=tb),
        grid_spec=pltpu.PrefetchScalarGridSpec(
            num_scalar_prefetch=1, grid=(n_tok // tb,),
            in_specs=[pl.BlockSpec(memory_space=pl.ANY), pl.BlockSpec(memory_space=pl.ANY)],
            out_specs=pl.BlockSpec(memory_space=pl.ANY),
            scratch_shapes=[pltpu.SemaphoreType.DMA(())]),
        out_shape=jax.ShapeDtypeStruct((n_rows, d), x.dtype),
        input_output_aliases={2: 0},
        compiler_params=_params(("arbitrary",)),
        name="moe_dispatch",
    )(dest_flat, x, jnp.zeros((n_rows, d), x.dtype))


def _expert_kernel(be_ref, nu_ref, x_ref, wg_ref, wu_ref, wd_ref, o_ref):
    b = pl.program_id(0)

    @pl.when(b < nu_ref[0])
    def _():
        x = _mx(x_ref[...])
        gate = jnp.dot(x, wg_ref[...], preferred_element_type=F32)
        up = jnp.dot(x, wu_ref[...], preferred_element_type=F32)
        hdn = gate * _sigmoid(gate) * up
        y = jnp.dot(_mx(hdn), wd_ref[...], preferred_element_type=F32)
        o_ref[...] = _rnd(y)

    @pl.when(b >= nu_ref[0])
    def _():
        o_ref[...] = jnp.zeros(o_ref.shape, F32)


def _experts(blk_e, n_used, buf, w_gate, w_up, w_down, blk):
    n_rows, d = buf.shape
    de = w_gate.shape[2]
    return pl.pallas_call(
        _expert_kernel,
        grid_spec=pltpu.PrefetchScalarGridSpec(
            num_scalar_prefetch=2, grid=(n_rows // blk,),
            in_specs=[pl.BlockSpec((blk, d), lambda b, be, nu: (b, 0)),
                      pl.BlockSpec((None, d, de), lambda b, be, nu: (be[b], 0, 0)),
                      pl.BlockSpec((None, d, de), lambda b, be, nu: (be[b], 0, 0)),
                      pl.BlockSpec((None, de, d), lambda b, be, nu: (be[b], 0, 0))],
            out_specs=pl.BlockSpec((blk, d), lambda b, be, nu: (b, 0))),
        out_shape=jax.ShapeDtypeStruct((n_rows, d), F32),
        compiler_params=_params(("arbitrary",)),
        name="moe_experts",
    )(blk_e, n_used, buf, w_gate, w_up, w_down)


def _combine_kernel(dest_ref, y_hbm, x_ref, gt_ref, g_ref, b_ref, xo_ref, xb_ref, rows_s, sem, *, n_tok, alpha):
    tc = x_ref.shape[0]
    base = pl.program_id(0) * tc

    def issue(r, carry):
        for kk in range(TOP_K):
            _row_copy(y_hbm, dest_ref[kk * n_tok + base + r], rows_s.at[kk], r, sem).start()
        return carry

    def drain(r, carry):
        for kk in range(TOP_K):
            _row_copy(y_hbm, 0, rows_s.at[kk], 0, sem).wait()
        return carry

    lax.fori_loop(0, tc, issue, 0)
    lax.fori_loop(0, tc, drain, 0)
    gt = gt_ref[...]
    y = _rnd(gt[:, 0:1]) * rows_s[0] + _rnd(gt[:, 1:2]) * rows_s[1]
    out = _layer_norm(alpha * x_ref[...] + y, g_ref[...], b_ref[...])
    xo_ref[...] = out
    xb_ref[...] = out.astype(xb_ref.dtype)


def _combine(dest_flat, y, x, gates, g, bt, alpha):
    n_tok, d = x.shape
    tc = min(n_tok, 128)
    row = lambda i, dst: (i, 0)
    fixed = lambda i, dst: (0, 0)
    return pl.pallas_call(
        functools.partial(_combine_kernel, n_tok=n_tok, alpha=alpha),
        grid_spec=pltpu.PrefetchScalarGridSpec(
            num_scalar_prefetch=1, grid=(n_tok // tc,),
            in_specs=[pl.BlockSpec(memory_space=pl.ANY), pl.BlockSpec((tc, d), row), pl.BlockSpec((tc, TOP_K), row),
                      pl.BlockSpec((1, d), fixed), pl.BlockSpec((1, d), fixed)],
            out_specs=[pl.BlockSpec((tc, d), row), pl.BlockSpec((tc, d), row)],
            scratch_shapes=[pltpu.VMEM((TOP_K, tc, d), F32), pltpu.SemaphoreType.DMA(())]),
        out_shape=[jax.ShapeDtypeStruct((n_tok, d), F32), jax.ShapeDtypeStruct((n_tok, d), MXU_DTYPE)],
        compiler_params=_params(("arbitrary",)),
        name="moe_combine",
    )(dest_flat, y, x, gates, g.reshape(1, d), bt.reshape(1, d))


def _moe_ln(x, xb, w_router_t, b_router, w_gate, w_up, w_down, g, bt, alpha, n_valid, blk):
    m, d = x.shape
    ne = w_gate.shape[0]
    mp = max(m, LANES)
    logits_t = _mm_nt(w_router_t, xb)
    if mp != m:
        logits_t = jnp.pad(logits_t, ((0, 0), (0, mp - m)))
    eidx, gates, rank, counts = _route(logits_t, b_router, n_valid)
    counts = counts[:, 0].astype(I32)
    padded = (counts + blk - 1) // blk * blk
    pad_end = jnp.cumsum(padded)
    pad_start = pad_end - padded
    n_blocks = -(-(n_valid * TOP_K + ne * (blk - 1)) // blk)
    dest = (pad_start[eidx] + rank)[:, :m].astype(I32)
    if n_valid < m:
        dest = jnp.where(jnp.arange(m)[None, :] < n_valid, dest, 0)
    dest_flat = dest.reshape(-1)
    blk_e = jnp.minimum(jnp.searchsorted(pad_end, jnp.arange(n_blocks) * blk, side="right"), ne - 1).astype(I32)
    n_used = (pad_end[-1:] // blk).astype(I32)
    buf = _dispatch_rows(dest_flat, x, n_blocks * blk, n_valid)
    y = _experts(blk_e, n_used, buf, w_gate, w_up, w_down, blk)
    return _combine(dest_flat, y, x, gates[:, :m].T, g, bt, alpha)


def _dispatch_rows(dest_flat, x, n_rows, n_valid):
    m = x.shape[0]
    if n_valid == m:
        return _dispatch(dest_flat, x, n_rows)
    dest = dest_flat.reshape(TOP_K, m)[:, :n_valid].reshape(-1)
    return _dispatch(dest, x[:n_valid], n_rows)


def _rope_tables(pos, half):
    inv = ROPE_THETA ** (-jnp.arange(half, dtype=F32) / half)
    ang = pos.astype(F32)[:, None] * inv[None, :]
    c, s = jnp.cos(ang), jnp.sin(ang)
    reps = LANES // (2 * half)
    return (jnp.tile(jnp.concatenate([c, c], -1), (1, reps)), jnp.tile(jnp.concatenate([-s, s], -1), (1, reps)), half)


def _cols(w, start, width, pad_to=None):
    out = w[:, start:start + width].astype(MXU_DTYPE)
    if pad_to is not None and pad_to > width:
        out = jnp.pad(out, ((0, 0), (0, pad_to - width)))
    return out


def kernel(x_prompt, x_sample, state_mlstm_c, state_mlstm_n, state_mlstm_m, cache_diff_k, cache_diff_v, cache_dsa_k, cache_dsa_v, cache_idx_k, state_ret, page_table, w_in_even, w_out_even, b_igate, b_fgate, g_mlstm, lam_q1, lam_k1, lam_q2, lam_k2, g_subln, w_in_odd, w_out_odd, g_ret, ln_mix_g, ln_mix_b, ln_ffn_g, ln_ffn_b, w_router, b_router, w_gate, w_up, w_down):
    bsz, seq, d_model = x_prompt.shape
    dec_b, dec_seq, _ = x_sample.shape
    assert dec_seq == 1
    depth = w_gate.shape[0]
    alpha = (2 * depth) ** 0.25
    h_a, dh_a = g_mlstm.shape[1:]
    h_b, dv_b = cache_diff_k.shape[3:]
    h_c, dh_c = cache_dsa_k.shape[3:]
    d_i = cache_idx_k.shape[3]
    h_d, dk_d, dv_d = state_ret.shape[2:]
    w_a, w_b, w_c, w_d = h_a * dh_a, h_b * dv_b, h_c * dh_c, h_d * dv_d
    h_i = w_in_odd.shape[2] - (3 * w_c + d_i + 2 * h_d * dk_d + 2 * w_d)
    h_i = h_i // (d_i + 1)
    past_len = page_table.shape[1] * cache_diff_k.shape[2]
    m_p = bsz * seq
    m_s = 2 * SUBLANES

    pos_p = jnp.arange(seq)
    pos_s = jnp.full((m_s,), past_len)
    rope_p = {h: _rope_tables(pos_p, h) for h in (dh_c // 2, d_i // 2)}
    rope_s = {h: _rope_tables(pos_s, h) for h in (dh_c // 2, d_i // 2)}
    log_gamma = jnp.log(1.0 - 2.0 ** (-5.0 - jnp.arange(h_d, dtype=F32)))
    w_router_t = w_router.T.astype(MXU_DTYPE)

    xp = x_prompt.reshape(m_p, d_model)
    xs = jnp.pad(x_sample.reshape(dec_b, d_model), ((0, m_s - dec_b), (0, 0)))
    xp_b, xs_b = xp.astype(MXU_DTYPE), xs.astype(MXU_DTYPE)
    names = ("mlstm_c", "mlstm_n", "mlstm_m", "diff_k", "diff_v", "dsa_k", "dsa_v", "idx_k", "ret")
    new_p = {k: [] for k in names}
    new_s = {k: [] for k in names}

    def pad_tokens(a):
        return jnp.pad(a[:dec_b, None, :], ((0, 0), (0, CHUNK - 1), (0, 0)))

    for l in range(depth):
        j = l // 2
        if l % 2 == 0:
            w = w_in_even[j]
            o_g = 4 * w_a
            o_b = o_g + 2 * h_a
            w_main = _cols(w, 0, 4 * w_a)
            w_gates_t = w[:, o_g:o_b].T.astype(MXU_DTYPE)
            w_qb, w_kb, w_vb = (_cols(w, o_b + i * w_b, w_b) for i in range(3))
            lams = (lam_q1[j], lam_k1[j], lam_q2[j], lam_k2[j])
            lam_init = 0.8 - 0.6 * math.exp(-0.3 * l)
            half = dv_b // 4
            z = _mm(xp_b, w_main).reshape(bsz, seq, 4 * w_a)
            gt = _mm_nt(w_gates_t, xp_b).reshape(2 * h_a, bsz, seq).transpose(1, 0, 2).reshape(bsz, 2 * h_a, 1, seq)
            qb = _mm(xp_b, w_qb, rope=rope_p[half]).reshape(bsz, seq, w_b)
            kb = _mm(xp_b, w_kb, rope=rope_p[half]).reshape(bsz, seq, w_b)
            vb = _mm(xp_b, w_vb).reshape(bsz, seq, w_b)
            zero = lambda *s: jnp.zeros(s, F32)
            h_mix, c_p, n_p, mm_p = _mlstm(z, gt, b_igate[j], b_fgate[j], g_mlstm[j], zero(bsz, h_a, dh_a, dh_a),
                                           zero(bsz, h_a, dh_a), zero(bsz, h_a), CHUNK)
            ob = _diff_attn(qb, kb, vb, lams, g_subln[j], lam_init, h_b)
            mix_a_p, mix_b_p = h_mix.reshape(m_p, w_a), ob.reshape(m_p, w_b)
            st_p = (c_p, n_p.reshape(bsz, h_a, dh_a), mm_p.reshape(bsz, h_a),
                    kb.reshape(bsz, seq, h_b, dv_b), vb.reshape(bsz, seq, h_b, dv_b))
            zs = pad_tokens(_mm(xs_b, w_main))
            gts = _mm_nt(w_gates_t, xs_b)[:, :dec_b].T
            gts = jnp.pad(gts[:, :, None, None], ((0, 0), (0, 0), (0, 0), (0, CHUNK - 1)))
            qbs = _mm(xs_b, w_qb, rope=rope_s[half])[:dec_b]
            kbs = _mm(xs_b, w_kb, rope=rope_s[half])[:dec_b]
            vbs = _mm(xs_b, w_vb)[:dec_b]
            hs_mix, c_s, n_s, mm_s = _mlstm(zs, gts, b_igate[j], b_fgate[j], g_mlstm[j], state_mlstm_c[j],
                                            state_mlstm_n[j], state_mlstm_m[j], 1)
            obs = _diff_decode(qbs, kbs, vbs, cache_diff_k, cache_diff_v, j, page_table, lams, g_subln[j], lam_init, h_b)
            pad_rows = lambda a: jnp.pad(a, ((0, m_s - dec_b), (0, 0)))
            mix_a_s, mix_b_s = pad_rows(hs_mix[:, 0, :]), pad_rows(obs)
            st_s = (c_s, n_s.reshape(dec_b, h_a, dh_a), mm_s.reshape(dec_b, h_a),
                    kbs.reshape(dec_b, 1, h_b, dv_b), vbs.reshape(dec_b, 1, h_b, dv_b))
            w_out = w_out_even[j].astype(MXU_DTYPE)
            keys = names[:5]
        else:
            w = w_in_odd[j]
            o_qi = 3 * w_c
            o_ki = o_qi + h_i * d_i
            o_wi = o_ki + d_i
            o_qd = o_wi + h_i
            o_vd = o_qd + 2 * h_d * dk_d
            w_qc, w_kc, w_vc = (_cols(w, i * w_c, w_c) for i in range(3))
            w_qi = _cols(w, o_qi, h_i * d_i)
            w_ki = _cols(w, o_ki, d_i, pad_to=LANES)
            w_wi = _cols(w, o_wi, h_i, pad_to=LANES)
            w_qkd = _cols(w, o_qd, 2 * h_d * dk_d)
            w_vgd = _cols(w, o_vd, 2 * w_d)
            hc, hi, hd = dh_c // 2, d_i // 2, dk_d // 2

            def project(xb_, rope):
                qc = _mm(xb_, w_qc, rope=rope[hc])
                kc = _mm(xb_, w_kc, rope=rope[hc])
                vc = _mm(xb_, w_vc)
                qi = _mm(xb_, w_qi, rope=rope[hi])
                ki = _mm(xb_, w_ki, rope=rope[hi], n_out=d_i)
                wi = _mm(xb_, w_wi)
                qkd = _mm(xb_, w_qkd, rope=rope[hd])
                vgd = _mm(xb_, w_vgd)
                return qc, kc, vc, qi, ki, wi, qkd, vgd

            qc, kc, vc, qi, ki, wi, qkd, vgd = project(xp_b, rope_p)
            r3 = lambda a: a.reshape(bsz, seq, a.shape[1])
            oc = _dsa_attn(r3(qc), r3(kc), r3(vc), r3(qi), r3(ki), r3(wi), h_c, h_i)
            od, s_p = _retention(r3(qkd), r3(vgd), log_gamma, g_ret[j], jnp.zeros((bsz, h_d, dk_d, dv_d), F32), CHUNK)
            mix_a_p, mix_b_p = oc.reshape(m_p, w_c), od.reshape(m_p, w_d)
            st_p = (kc.reshape(bsz, seq, h_c, dh_c), vc.reshape(bsz, seq, h_c, dh_c), ki.reshape(bsz, seq, d_i), s_p)
            qc, kc, vc, qi, ki, wi, qkd, vgd = (a[:dec_b] for a in project(xs_b, rope_s))
            ocs = _dsa_decode(qc, kc, vc, qi, ki, wi, cache_dsa_k, cache_dsa_v, cache_idx_k, j, page_table, h_c, h_i)
            ods, s_s = _retention(pad_tokens(qkd), pad_tokens(vgd), log_gamma, g_ret[j], state_ret[j], 1)
            pad_rows = lambda a: jnp.pad(a, ((0, m_s - dec_b), (0, 0)))
            mix_a_s, mix_b_s = pad_rows(ocs), pad_rows(ods[:, 0, :])
            st_s = (kc.reshape(dec_b, 1, h_c, dh_c), vc.reshape(dec_b, 1, h_c, dh_c), ki.reshape(dec_b, 1, d_i), s_s)
            w_out = w_out_odd[j].astype(MXU_DTYPE)
            keys = names[5:]
        for k, a, b in zip(keys, st_p, st_s):
            new_p[k].append(a)
            new_s[k].append(b)
        xp, xp_b = _proj_ln(mix_a_p, mix_b_p, w_out, xp, ln_mix_g[l], ln_mix_b[l], alpha)
        xs, xs_b = _proj_ln(mix_a_s, mix_b_s, w_out, xs, ln_mix_g[l], ln_mix_b[l], alpha)
        wg, wu, wd = (a[l].astype(MXU_DTYPE) for a in (w_gate, w_up, w_down))
        xp, xp_b = _moe_ln(xp, xp_b, w_router_t, b_router, wg, wu, wd, ln_ffn_g[l], ln_ffn_b[l], alpha, m_p, 256)
        xs, xs_b = _moe_ln(xs, xs_b, w_router_t, b_router, wg, wu, wd, ln_ffn_g[l], ln_ffn_b[l], alpha, dec_b, 16)

    out = [xp.reshape(bsz, seq, d_model), xs[:dec_b].reshape(dec_b, 1, d_model)]
    for k in names:
        out += [jnp.stack(new_p[k]), jnp.stack(new_s[k])]
    return tuple(out)
```

```python
import functools
import math

import jax
import jax.numpy as jnp
from jax import lax
from jax.experimental import pallas as pl
from jax.experimental.pallas import tpu as pltpu

F32 = jnp.float32
I32 = jnp.int32
MXU_DTYPE = jnp.bfloat16

N_GROUPS = 4
TOP_K = 2
TOPK_MAX = 256
CHUNK = 128
ROPE_THETA = 10000.0
EPS = 1e-5

LANES = 128
SUBLANES = 8
VMEM_LIMIT_BYTES = 52 * 1024 * 1024

NEG_INF = float("-inf")


def _mx(x):
    return x.astype(MXU_DTYPE)


def _rnd(x):
    return x.astype(MXU_DTYPE).astype(F32)


def _dot(a, b):
    return jnp.dot(_mx(a), _mx(b), preferred_element_type=F32)


def _dot_nt(a, b):
    return lax.dot_general(_mx(a), _mx(b), (((1,), (1,)), ((), ())), preferred_element_type=F32)


def _params(sem, vmem=VMEM_LIMIT_BYTES):
    return pltpu.CompilerParams(dimension_semantics=sem, vmem_limit_bytes=vmem)


def _iota(shape, axis):
    return lax.broadcasted_iota(I32, shape, axis)


def _sigmoid(x):
    return 1.0 / (1.0 + jnp.exp(-x))


def _row_to_col(row, n):
    eye = _iota((n, n), 0) == _iota((n, n), 1)
    return jnp.sum(jnp.where(eye, jnp.broadcast_to(row, (n, n)), 0.0), axis=1, keepdims=True)


def _head_norm(x, g):
    mu = jnp.mean(x, axis=-1, keepdims=True)
    xc = x - mu
    var = jnp.mean(xc * xc, axis=-1, keepdims=True)
    return xc * lax.rsqrt(var + EPS) * g


def _layer_norm(z, g, b):
    mu = jnp.mean(z, axis=-1, keepdims=True)
    zc = z - mu
    var = jnp.mean(zc * zc, axis=-1, keepdims=True)
    return zc * lax.rsqrt(var + EPS) * g + b


def _rope_apply(a, cos, sin, half):
    if 2 * half == LANES:
        r = pltpu.roll(a, half, 1)
    else:
        lane = _iota(a.shape, 1)
        r = jnp.where((lane % (2 * half)) < half, pltpu.roll(a, LANES - half, 1), pltpu.roll(a, half, 1))
    return a * cos + r * sin


def _mm_kernel(x_ref, w_ref, *rest, rope_half, n_outs):
    o_refs = rest[len(rest) - n_outs:]
    acc = jnp.dot(x_ref[...], w_ref[...], preferred_element_type=F32)
    if rope_half:
        cos = rest[0][...]
        sin = rest[1][...]
        n_out = o_refs[0].shape[1]
        for j in range(acc.shape[1] // LANES):
            res = _rope_apply(acc[:, j * LANES:(j + 1) * LANES], cos, sin, rope_half)
            nw = min(LANES, n_out - j * LANES)
            for o_ref in o_refs:
                o_ref[:, j * LANES:j * LANES + nw] = res[:, :nw].astype(o_ref.dtype)
    else:
        for o_ref in o_refs:
            o_ref[...] = acc.astype(o_ref.dtype)


def _mm(x, w, rope=None, n_out=None, out_dtype=F32, mxu_copy=False):
    m, k = x.shape
    n = w.shape[1]
    n_out = n if n_out is None else n_out
    tm = min(m, 512)
    tn = min(n, 512)
    assert m % tm == 0 and n % tn == 0 and (n_out == n or n == tn)
    in_specs = [pl.BlockSpec((tm, k), lambda i, j: (i, 0)), pl.BlockSpec((k, tn), lambda i, j: (0, j))]
    args = [x, w]
    half = 0
    if rope is not None:
        cos, sin, half = rope
        nt = cos.shape[0] // tm
        assert cos.shape[0] % tm == 0
        in_specs += [pl.BlockSpec((tm, LANES), lambda i, j: (i % nt, 0))] * 2
        args += [cos, sin]
    out_spec = pl.BlockSpec((tm, min(tn, n_out)), lambda i, j: (i, j))
    dtypes = (out_dtype, MXU_DTYPE) if mxu_copy else (out_dtype,)
    outs = pl.pallas_call(
        functools.partial(_mm_kernel, rope_half=half, n_outs=len(dtypes)),
        grid=(m // tm, n // tn),
        in_specs=in_specs,
        out_specs=[out_spec] * len(dtypes),
        out_shape=[jax.ShapeDtypeStruct((m, n_out), dt) for dt in dtypes],
        compiler_params=_params(("parallel", "parallel")),
        name="proj",
    )(*args)
    return outs if mxu_copy else outs[0]


def _mm_nt_kernel(w_ref, x_ref, o_ref):
    o_ref[...] = lax.dot_general(w_ref[...], x_ref[...], (((1,), (1,)), ((), ())), preferred_element_type=F32)


def _mm_nt(w_t, x):
    n, k = w_t.shape
    m = x.shape[0]
    tm = min(m, 512)
    return pl.pallas_call(
        _mm_nt_kernel,
        grid=(m // tm,),
        in_specs=[pl.BlockSpec((n, k), lambda i: (0, 0)), pl.BlockSpec((tm, k), lambda i: (i, 0))],
        out_specs=pl.BlockSpec((n, tm), lambda i: (0, i)),
        out_shape=jax.ShapeDtypeStruct((n, m), F32),
        compiler_params=_params(("parallel",)),
        name="proj_t",
    )(w_t, x)


def _proj_ln_kernel(a_ref, b_ref, wa_ref, wb_ref, x_ref, g_ref, bt_ref, xo_ref, xb_ref, *, alpha):
    y = jnp.dot(a_ref[...], wa_ref[...], preferred_element_type=F32)
    y = y + jnp.dot(b_ref[...], wb_ref[...], preferred_element_type=F32)
    out = _layer_norm(alpha * x_ref[...] + y, g_ref[...], bt_ref[...])
    xo_ref[...] = out
    xb_ref[...] = out.astype(xb_ref.dtype)


def _proj_ln(a, b, w_out, x, g, bt, alpha):
    m, d = x.shape
    ka = a.shape[1]
    tm = min(m, 256)
    wa, wb = w_out[:ka], w_out[ka:]
    row = lambda i: (i, 0)
    fixed = lambda i: (0, 0)
    return pl.pallas_call(
        functools.partial(_proj_ln_kernel, alpha=alpha),
        grid=(m // tm,),
        in_specs=[pl.BlockSpec((tm, ka), row), pl.BlockSpec((tm, b.shape[1]), row),
                  pl.BlockSpec(wa.shape, fixed), pl.BlockSpec(wb.shape, fixed),
                  pl.BlockSpec((tm, d), row), pl.BlockSpec((1, d), fixed), pl.BlockSpec((1, d), fixed)],
        out_specs=[pl.BlockSpec((tm, d), row), pl.BlockSpec((tm, d), row)],
        out_shape=[jax.ShapeDtypeStruct((m, d), F32), jax.ShapeDtypeStruct((m, d), MXU_DTYPE)],
        compiler_params=_params(("parallel",)),
        name="out_proj_ln",
    )(a, b, wa, wb, x, g.reshape(1, d), bt.reshape(1, d))


def _lane_scan(x, op, fill):
    n = x.shape[1]
    lane = _iota(x.shape, 1)
    s = 1
    while s < n:
        x = op(x, jnp.where(lane >= s, pltpu.roll(x, s, 1), fill))
        s *= 2
    return x


def _mlstm_head(q, k, v, o_gate, ig, fpre, g, c_prev, n_prev, m_prev, valid_len):
    L = q.shape[0]
    lf = jnp.minimum(fpre, 0.0) - jnp.log1p(jnp.exp(-jnp.abs(fpre)))
    if valid_len < L:
        live = _iota((1, L), 1) < valid_len
        ig = jnp.where(live, ig, NEG_INF)
        lf = jnp.where(live, lf, 0.0)
    a_row = _lane_scan(lf, jnp.add, 0.0)
    b_row = ig - a_row
    run_row = jnp.maximum(_lane_scan(b_row, jnp.maximum, NEG_INF), m_prev)
    run_col = _row_to_col(run_row, L)
    causal = _iota((L, L), 1) <= _iota((L, L), 0)
    dmat = jnp.where(causal, jnp.exp(b_row - run_col), 0.0)
    w_inter = jnp.exp(m_prev - run_col)
    s = _dot_nt(q, k) * dmat
    num = w_inter * _dot(q, c_prev) + _dot(s, v)
    qn = jnp.sum(_rnd(q) * _rnd(n_prev), axis=1, keepdims=True)
    den = w_inter * qn + jnp.sum(s, axis=1, keepdims=True)
    a_col = _row_to_col(a_row, L)
    m_t = a_col + run_col
    h = num / jnp.maximum(jnp.abs(den), jnp.exp(-m_t))
    h = _head_norm(h, g) * _sigmoid(o_gate)
    a_last = a_row[:, L - 1:L]
    m_new = a_last + run_row[:, L - 1:L]
    w_c = jnp.exp(a_last + m_prev - m_new)
    ws_row = jnp.exp(a_last - a_row + ig - m_new)
    ws_col = _row_to_col(ws_row, L)
    kw = k * ws_col
    c_new = w_c * c_prev + _dot(kw.T, v)
    n_new = w_c * n_prev + jnp.sum(_rnd(ws_col) * _rnd(k), axis=0, keepdims=True)
    return h, c_new, n_new, m_new


def _mlstm_kernel(q_ref, k_ref, v_ref, o_ref, ig_ref, fg_ref, big_ref, bfg_ref, g_ref, c0_ref, n0_ref, m0_ref,
                  h_ref, c_ref, n_ref, m_ref, c_s, n_s, m_s, *, valid_len, scale):
    ci = pl.program_id(1)

    @pl.when(ci == 0)
    def _():
        c_s[...] = c0_ref[...]
        n_s[...] = n0_ref[...]
        m_s[...] = m0_ref[...]

    nh, dh, _ = c_s.shape
    for hh in range(nh):
        cols = slice(hh * dh, (hh + 1) * dh)
        h, c_new, n_new, m_new = _mlstm_head(
            q_ref[:, cols], k_ref[:, cols] * scale, v_ref[:, cols], o_ref[:, cols],
            ig_ref[hh] + big_ref[hh], fg_ref[hh] + bfg_ref[hh], g_ref[hh], c_s[hh], n_s[hh], m_s[hh], valid_len)
        h_ref[:, cols] = h.astype(h_ref.dtype)
        c_s[hh] = c_new
        n_s[hh] = n_new
        m_s[hh] = m_new

    @pl.when(ci == pl.num_programs(1) - 1)
    def _():
        c_ref[...] = c_s[...]
        n_ref[...] = n_s[...]
        m_ref[...] = m_s[...]


def _mlstm(z, gates_t, b_ig, b_fg, g_mh, c0, n0, m0, valid_len):
    bsz, t, _ = z.shape
    nh, dh = g_mh.shape
    L = CHUNK
    nc = t // L
    w = nh * dh
    assert dh == LANES and t % L == 0
    col = lambda part: pl.BlockSpec((None, L, w), lambda b, c: (b, c, part))
    gate = lambda part: pl.BlockSpec((None, nh, 1, L), lambda b, c: (b, part, 0, c))
    per_head = lambda shp: pl.BlockSpec((nh,) + shp, lambda b, c: (0, 0, 0))
    state = lambda shp: pl.BlockSpec((None, nh) + shp, lambda b, c: (b, 0, 0, 0))
    return pl.pallas_call(
        functools.partial(_mlstm_kernel, valid_len=valid_len, scale=dh ** -0.5),
        grid=(bsz, nc),
        in_specs=[col(0), col(1), col(2), col(3), gate(0), gate(1),
                  per_head((1, 1)), per_head((1, 1)), per_head((1, dh)),
                  state((dh, dh)), state((1, dh)), state((1, 1))],
        out_specs=[pl.BlockSpec((None, L, w), lambda b, c: (b, c, 0)),
                   state((dh, dh)), state((1, dh)), state((1, 1))],
        out_shape=[jax.ShapeDtypeStruct((bsz, t, w), MXU_DTYPE),
                   jax.ShapeDtypeStruct((bsz, nh, dh, dh), F32),
                   jax.ShapeDtypeStruct((bsz, nh, 1, dh), F32),
                   jax.ShapeDtypeStruct((bsz, nh, 1, 1), F32)],
        scratch_shapes=[pltpu.VMEM((nh, dh, dh), F32), pltpu.VMEM((nh, 1, dh), F32), pltpu.VMEM((nh, 1, 1), F32)],
        compiler_params=_params(("parallel", "arbitrary")),
        name="mlstm",
    )(z, z, z, z, gates_t, gates_t, b_ig.reshape(nh, 1, 1), b_fg.reshape(nh, 1, 1), g_mh.reshape(nh, 1, dh),
      c0, n0.reshape(bsz, nh, 1, dh), m0.reshape(bsz, nh, 1, 1))


def _ret_kernel(q_ref, k_ref, v_ref, gd_ref, lg_ref, g_ref, s0_ref, o_ref, s_ref, s_s, *, true_len, scale):
    ci = pl.program_id(1)

    @pl.when(ci == 0)
    def _():
        s_s[...] = s0_ref[...]

    L = q_ref.shape[0]
    nh, dk, dv = s_s.shape
    t_col = _iota((L, 1), 0).astype(F32)
    diff = (_iota((L, L), 0) - _iota((L, L), 1)).astype(F32)
    causal = diff >= 0.0
    chunk_len = float(min(L, true_len))
    for hh in range(nh):
        q = q_ref[:, hh * dk:(hh + 1) * dk]
        k = k_ref[:, hh * dk:(hh + 1) * dk] * scale
        v = v_ref[:, hh * dv:(hh + 1) * dv]
        lg = lg_ref[hh]
        decay = jnp.where(causal, jnp.exp(jnp.where(causal, diff, 0.0) * lg), 0.0)
        s_prev = s_s[hh]
        inner = _dot_nt(q, k) * decay
        o = _dot(inner, v) + jnp.exp((t_col + 1.0) * lg) * _dot(q, s_prev)
        ws_col = jnp.where(t_col < chunk_len, jnp.exp((chunk_len - 1.0 - t_col) * lg), 0.0)
        s_s[hh] = jnp.exp(chunk_len * lg) * s_prev + _dot((k * ws_col).T, v)
        gd = gd_ref[:, hh * dv:(hh + 1) * dv]
        o_ref[:, hh * dv:(hh + 1) * dv] = (_head_norm(o, g_ref[hh]) * (gd * _sigmoid(gd))).astype(o_ref.dtype)

    @pl.when(ci == pl.num_programs(1) - 1)
    def _():
        s_ref[...] = s_s[...]


def _retention(qk, vg, log_gamma, g_ret, s0, true_len):
    bsz, t, _ = qk.shape
    nh, dv = g_ret.shape
    dk = s0.shape[2]
    L = CHUNK
    nc = t // L
    assert dk % LANES == 0 and dv % LANES == 0
    half = lambda wdt, part: pl.BlockSpec((None, L, nh * wdt), lambda b, c: (b, c, part))
    return pl.pallas_call(
        functools.partial(_ret_kernel, true_len=true_len, scale=dk ** -0.5),
        grid=(bsz, nc),
        in_specs=[half(dk, 0), half(dk, 1), half(dv, 0), half(dv, 1),
                  pl.BlockSpec((nh, 1, 1), lambda b, c: (0, 0, 0)),
                  pl.BlockSpec((nh, 1, dv), lambda b, c: (0, 0, 0)),
                  pl.BlockSpec((None, nh, dk, dv), lambda b, c: (b, 0, 0, 0))],
        out_specs=[pl.BlockSpec((None, L, nh * dv), lambda b, c: (b, c, 0)),
                   pl.BlockSpec((None, nh, dk, dv), lambda b, c: (b, 0, 0, 0))],
        out_shape=[jax.ShapeDtypeStruct((bsz, t, nh * dv), MXU_DTYPE),
                   jax.ShapeDtypeStruct((bsz, nh, dk, dv), F32)],
        scratch_shapes=[pltpu.VMEM((nh, dk, dv), F32)],
        compiler_params=_params(("parallel", "arbitrary")),
        name="retention",
    )(qk, qk, vg, vg, log_gamma.reshape(nh, 1, 1), g_ret.reshape(nh, 1, dv), s0)


def _lambda(lq1, lk1, lq2, lk2, lam_init):
    s1 = jnp.sum(lq1 * lk1, axis=1, keepdims=True)
    s2 = jnp.sum(lq2 * lk2, axis=1, keepdims=True)
    return jnp.exp(s1) - jnp.exp(s2) + lam_init


def _rms_sub(o, g, lam_init):
    return o * lax.rsqrt(jnp.mean(o * o, axis=-1, keepdims=True) + EPS) * g * (1.0 - lam_init)


def _diff_kernel(qi_ref, kj_ref, q_ref, k_ref, v_ref, lq1_ref, lk1_ref, lq2_ref, lk2_ref, g_ref, o_ref,
                 m_s, l_s, acc_s, *, lam_init, scale):
    qi = qi_ref[pl.program_id(2)]
    kj = kj_ref[pl.program_id(2)]
    tq, dv = q_ref.shape
    tk = k_ref.shape[0]
    dh = dv // 2

    @pl.when(kj == 0)
    def _():
        m_s[...] = jnp.full(m_s.shape, NEG_INF, F32)
        l_s[...] = jnp.zeros(l_s.shape, F32)
        acc_s[...] = jnp.zeros(acc_s.shape, F32)

    qpos = qi * tq + _iota((tq, tk), 0)
    kpos = kj * tk + _iota((tq, tk), 1)
    ok = kpos <= qpos
    vb = _mx(v_ref[...])
    for c in range(2):
        s = _dot_nt(q_ref[:, c * dh:(c + 1) * dh], k_ref[:, c * dh:(c + 1) * dh]) * scale
        s = jnp.where(ok, s, NEG_INF)
        m_old = m_s[c]
        m_new = jnp.maximum(m_old, jnp.max(s, axis=1, keepdims=True))
        alpha = jnp.exp(m_old - m_new)
        p = jnp.exp(s - m_new)
        l_s[c] = alpha * l_s[c] + jnp.sum(p, axis=1, keepdims=True)
        acc_s[c] = alpha * acc_s[c] + jnp.dot(_mx(p), vb, preferred_element_type=F32)
        m_s[c] = m_new

    @pl.when(kj == qi)
    def _():
        lam = _lambda(lq1_ref[...], lk1_ref[...], lq2_ref[...], lk2_ref[...], lam_init)
        o = acc_s[0] / l_s[0] - lam * (acc_s[1] / l_s[1])
        o_ref[...] = _rms_sub(o, g_ref[...], lam_init).astype(o_ref.dtype)


def _diff_attn(qb, kb, vb, lams, g_sub, lam_init, nh):
    bsz, t, w = qb.shape
    dv = w // nh
    dh = dv // 2
    tq = tk = min(t, 512)
    nq = t // tq
    pairs = [(i, j) for i in range(nq) for j in range(i + 1)]
    qi_tab = jnp.asarray([p[0] for p in pairs], I32)
    kj_tab = jnp.asarray([p[1] for p in pairs], I32)
    vec = pl.BlockSpec((1, dh), lambda b, h, p, qt, kt: (0, 0))
    qo = pl.BlockSpec((None, tq, dv), lambda b, h, p, qt, kt: (b, qt[p], h))
    kv = pl.BlockSpec((None, tk, dv), lambda b, h, p, qt, kt: (b, kt[p], h))
    return pl.pallas_call(
        functools.partial(_diff_kernel, lam_init=lam_init, scale=dh ** -0.5),
        grid_spec=pltpu.PrefetchScalarGridSpec(
            num_scalar_prefetch=2, grid=(bsz, nh, len(pairs)),
            in_specs=[qo, kv, kv, vec, vec, vec, vec, pl.BlockSpec((1, dv), lambda b, h, p, qt, kt: (0, 0))],
            out_specs=qo,
            scratch_shapes=[pltpu.VMEM((2, tq, 1), F32), pltpu.VMEM((2, tq, 1), F32), pltpu.VMEM((2, tq, dv), F32)]),
        out_shape=jax.ShapeDtypeStruct((bsz, t, w), MXU_DTYPE),
        compiler_params=_params(("parallel", "parallel", "arbitrary")),
        name="diff_attn",
    )(qi_tab, kj_tab, qb, kb, vb, *[x.reshape(1, dh) for x in lams], g_sub.reshape(1, dv))


def _f32_key(x):
    i = lax.bitcast_convert_type(x, I32)
    return i ^ ((i >> 31) & 0x7FFFFFFF)


KEY_NEG_INF = -2139095041
I32_MIN = -2 ** 31


def _strict_upper(n):
    return jnp.where(_iota((n, n), 0) < _iota((n, n), 1), 1.0, 0.0).astype(MXU_DTYPE)


def _dsa_kernel(qc_ref, kc_ref, vc_ref, qi_ref, ki_ref, wi_ref, o_ref, key_s, bias_s,
                *, topk, n_heads, n_idx_heads, scale, wi_scale):
    qb = pl.program_id(1)
    tq = qc_ref.shape[0]
    dh = qc_ref.shape[1] // n_heads
    ck = key_s.shape[2]
    d_i = ki_ref.shape[1]
    q0 = qb * tq
    n_chunks = (q0 + tq + ck - 1) // ck

    def _select():
        qi = qi_ref[...]
        wi = _rnd(wi_ref[...] * wi_scale)
        qpos = q0 + _iota((tq, ck), 0)

        def score_chunk(c, carry):
            start = pl.multiple_of(c * ck, ck)
            kic = ki_ref[pl.ds(start, ck), :]
            acc = jnp.zeros((tq, ck), F32)
            for hh in range(n_idx_heads):
                sc = jnp.maximum(_dot_nt(qi[:, hh * d_i:(hh + 1) * d_i], kic), 0.0)
                acc = acc + wi[:, hh:hh + 1] * sc
            acc = acc + 0.0
            kpos = c * ck + _iota((tq, ck), 1)
            key_s[c] = jnp.where(kpos <= qpos, _f32_key(acc), KEY_NEG_INF)
            return carry

        lax.fori_loop(0, n_chunks, score_chunk, 0)

        def count(pred):
            def body(c, acc):
                hit = jnp.where(pred(key_s[c]), 1.0, 0.0)
                part = hit[:, 0:LANES]
                for j in range(1, ck // LANES):
                    part = part + hit[:, j * LANES:(j + 1) * LANES]
                return acc + part
            acc = lax.fori_loop(0, n_chunks, body, jnp.zeros((tq, LANES), F32))
            return jnp.sum(acc, axis=1, keepdims=True)

        def bit_step(it, prefix):
            bit = jnp.left_shift(jnp.int32(1), 31 - it)
            cand = (prefix | bit) ^ I32_MIN
            cnt = count(lambda kk: kk >= cand)
            return jnp.where(cnt >= float(topk), prefix | bit, prefix)

        prefix = lax.fori_loop(0, 32, bit_step, jnp.zeros((tq, 1), I32))
        thr = prefix ^ I32_MIN
        need = float(topk) - count(lambda kk: kk > thr)
        upper = _strict_upper(LANES)

        def mask_chunk(c, taken):
            kk = key_s[c]
            cols = []
            for j in range(ck // LANES):
                kj = kk[:, j * LANES:(j + 1) * LANES]
                eq = kj == thr
                eqf = jnp.where(eq, 1.0, 0.0)
                before = taken + jnp.dot(_mx(eqf), upper, preferred_element_type=F32)
                sel = ((kj > thr) | (eq & (before < need))) & (kj > KEY_NEG_INF)
                cols.append(jnp.where(sel, 0.0, NEG_INF))
                taken = taken + jnp.sum(eqf, axis=1, keepdims=True)
            bias_s[c] = jnp.concatenate(cols, axis=1)
            return taken

        lax.fori_loop(0, n_chunks, mask_chunk, jnp.zeros((tq, 1), F32))

    _select()

    for h in range(n_heads):
        cols = slice(h * dh, (h + 1) * dh)
        q = _mx(qc_ref[:, cols])

        def attend(c, carry, q=q, cols=cols):
            m_old, l_old, acc = carry
            start = pl.multiple_of(c * ck, ck)
            s = _dot_nt(q, kc_ref[pl.ds(start, ck), cols]) * scale + bias_s[c]
            m_new = jnp.maximum(m_old, jnp.max(s, axis=1, keepdims=True))
            m_safe = jnp.where(m_new == NEG_INF, 0.0, m_new)
            alpha = jnp.exp(m_old - m_safe)
            p = jnp.exp(s - m_safe)
            l_new = alpha * l_old + jnp.sum(p, axis=1, keepdims=True)
            acc = alpha * acc + _dot(p, vc_ref[pl.ds(start, ck), cols])
            return m_new, l_new, acc

        init = (jnp.full((tq, 1), NEG_INF, F32), jnp.zeros((tq, 1), F32), jnp.zeros((tq, dh), F32))
        _, l_fin, acc = lax.fori_loop(0, n_chunks, attend, init)
        o_ref[:, cols] = (acc / l_fin).astype(o_ref.dtype)


def _dsa_attn(qc, kc, vc, qi, ki, wi, nh, n_idx_heads):
    bsz, t, w = qc.shape
    dh = w // nh
    d_i = ki.shape[2]
    tq = min(t, 128)
    ck = min(t, 512)
    topk = min(TOPK_MAX, t // 4)
    rows = lambda wdt: pl.BlockSpec((None, tq, wdt), lambda b, i: (b, i, 0))
    whole = lambda wdt: pl.BlockSpec((None, t, wdt), lambda b, i: (b, 0, 0))
    return pl.pallas_call(
        functools.partial(_dsa_kernel, topk=topk, n_heads=nh, n_idx_heads=n_idx_heads, scale=dh ** -0.5,
                          wi_scale=(n_idx_heads * d_i) ** -0.5),
        grid=(bsz, t // tq),
        in_specs=[rows(w), whole(w), whole(w), rows(qi.shape[2]), whole(d_i), rows(wi.shape[2])],
        out_specs=rows(w),
        out_shape=jax.ShapeDtypeStruct((bsz, t, w), MXU_DTYPE),
        scratch_shapes=[pltpu.VMEM((t // ck, tq, ck), I32), pltpu.VMEM((t // ck, tq, ck), F32)],
        compiler_params=_params(("parallel", "arbitrary")),
        name="dsa_attn",
    )(qc, kc, vc, qi, ki, wi)


def _page_copy(cache_hbm, layer, phys, buf, slot, i, sem):
    return pltpu.make_async_copy(cache_hbm.at[layer, phys], buf.at[slot, i], sem.at[slot])


def _paged_step(cache_hbm, layer, pt_ref, buf, sem):
    g = buf.shape[1]
    per_b = pl.num_programs(1)
    step = pl.program_id(0) * per_b + pl.program_id(1)

    def fetch(t):
        b = t // per_b
        s = t % per_b
        for i in range(g):
            _page_copy(cache_hbm, layer, pt_ref[b, s * g + i], buf, t % 2, i, sem).start()

    @pl.when(step == 0)
    def _():
        fetch(step)

    @pl.when(step + 1 < pl.num_programs(0) * per_b)
    def _():
        fetch(step + 1)

    slot = step % 2
    for i in range(g):
        _page_copy(cache_hbm, layer, 0, buf, slot, i, sem).wait()
    return slot


def _paged_call(body, name, page_table, cache, g, in_arrays, in_specs, out_spec, out_shape):
    bd, n_pages = page_table.shape
    assert n_pages % g == 0
    return pl.pallas_call(
        body,
        grid_spec=pltpu.PrefetchScalarGridSpec(
            num_scalar_prefetch=1, grid=(bd, n_pages // g),
            in_specs=in_specs + [pl.BlockSpec(memory_space=pl.ANY)],
            out_specs=out_spec,
            scratch_shapes=[pltpu.VMEM((2, g) + cache.shape[2:], cache.dtype), pltpu.SemaphoreType.DMA((2,))]),
        out_shape=out_shape,
        compiler_params=_params(("arbitrary", "arbitrary")),
        name=name,
    )(page_table, *in_arrays, cache)


def _pages_per_step(n_pages, want):
    g = min(want, n_pages)
    while n_pages % g:
        g -= 1
    return g


def _diff_qk_kernel(pt_ref, q_ref, k_hbm, o_ref, buf, sem, *, layer):
    slot = _paged_step(k_hbm, layer, pt_ref, buf, sem)
    _, g, page, nh, _ = buf.shape
    for i in range(g):
        for h in range(nh):
            r = _dot_nt(q_ref[h], buf[slot, i, :, h, :])
            o_ref[h:h + 1, i * page:(i + 1) * page] = r[0:1]
            o_ref[nh + h:nh + h + 1, i * page:(i + 1) * page] = r[1:2]


def _diff_pv_kernel(pt_ref, a_ref, v_hbm, o_ref, buf, sem, *, layer):
    slot = _paged_step(v_hbm, layer, pt_ref, buf, sem)
    _, g, page, nh, dv = buf.shape

    @pl.when(pl.program_id(1) == 0)
    def _():
        o_ref[...] = jnp.zeros(o_ref.shape, F32)

    for h in range(nh):
        acc = jnp.zeros((SUBLANES, dv), F32)
        for i in range(g):
            acc = acc + _dot(a_ref[:, i * page:(i + 1) * page], buf[slot, i, :, h, :])
        o_ref[h:h + 1, :] += acc[h:h + 1]


def _block_rows(x, blocks):
    bd, d = x.shape
    r = len(blocks)
    wdt = d // r
    lane_blk = jnp.arange(d) // wdt
    mask = (lane_blk[None, :] == jnp.asarray(blocks)[:, None]).astype(x.dtype)
    return x[:, None, :] * mask[None]


def _diff_softmax_kernel(s_ref, q_ref, kn_ref, lq1_ref, lk1_ref, lq2_ref, lk2_ref, a_ref, an_ref,
                         *, lam_init, scale, nh):
    s = s_ref[...] * scale
    s_new = jnp.sum(_rnd(q_ref[...]) * _rnd(kn_ref[...]), axis=1, keepdims=True) * scale
    m = jnp.maximum(jnp.max(s, axis=1, keepdims=True), s_new)
    e = jnp.exp(s - m)
    e_new = jnp.exp(s_new - m)
    z = jnp.sum(e, axis=1, keepdims=True) + e_new
    lam = _lambda(lq1_ref[...], lk1_ref[...], lq2_ref[...], lk2_ref[...], lam_init)
    p = e / z
    p_new = e_new / z
    pad = a_ref.shape[0] - nh
    a_ref[...] = jnp.concatenate([p[:nh] - lam * p[nh:], jnp.zeros((pad, p.shape[1]), F32)], axis=0)
    an_ref[...] = jnp.broadcast_to(p_new[:nh] - lam * p_new[nh:], an_ref.shape)


def _diff_final_kernel(acc_ref, an_ref, vn_ref, g_ref, o_ref, *, lam_init):
    nh = vn_ref.shape[0]
    full = acc_ref[0:nh, :] + _rnd(an_ref[:, 0:1]) * _rnd(vn_ref[...])
    o_ref[...] = _rms_sub(full, g_ref[...], lam_init).astype(o_ref.dtype)


def _diff_decode(qb, kb_new, vb_new, cache_k, cache_v, layer, page_table, lams, g_sub, lam_init, nh):
    bd, w = qb.shape
    dv = w // nh
    dh = dv // 2
    page = cache_k.shape[2]
    n_pages = page_table.shape[1]
    s_len = n_pages * page
    g = _pages_per_step(n_pages, 8)
    blocks = tuple(h * 2 + c for c in range(2) for h in range(nh))
    q_rows = _block_rows(qb, blocks)
    half_mask = (jnp.arange(dv)[None, :] // dh == jnp.arange(SUBLANES)[:, None]).astype(F32)
    q_tiles = qb.reshape(bd, nh, 1, dv) * half_mask[None, None]
    s = _paged_call(
        functools.partial(_diff_qk_kernel, layer=layer), "diff_decode_qk", page_table, cache_k, g, [q_tiles],
        [pl.BlockSpec((None, nh, SUBLANES, dv), lambda b, p, pt: (b, 0, 0, 0))],
        pl.BlockSpec((None, 2 * nh, g * page), lambda b, p, pt: (b, 0, p)),
        jax.ShapeDtypeStruct((bd, 2 * nh, s_len), F32))
    per_b = lambda shp: pl.BlockSpec((None,) + shp, lambda b: (b, 0, 0))
    vec = pl.BlockSpec((1, dh), lambda b: (0, 0))
    a, a_new = pl.pallas_call(
        functools.partial(_diff_softmax_kernel, lam_init=lam_init, scale=dh ** -0.5, nh=nh),
        grid=(bd,),
        in_specs=[per_b((2 * nh, s_len)), per_b((2 * nh, w)), per_b((1, w)), vec, vec, vec, vec],
        out_specs=[per_b((SUBLANES, s_len)), per_b((nh, LANES))],
        out_shape=[jax.ShapeDtypeStruct((bd, SUBLANES, s_len), F32), jax.ShapeDtypeStruct((bd, nh, LANES), F32)],
        compiler_params=_params(("parallel",)),
        name="diff_decode_softmax",
    )(s, q_rows, kb_new.reshape(bd, 1, w), *[x.reshape(1, dh) for x in lams])
    acc = _paged_call(
        functools.partial(_diff_pv_kernel, layer=layer), "diff_decode_pv", page_table, cache_v, g, [a],
        [pl.BlockSpec((None, SUBLANES, g * page), lambda b, p, pt: (b, 0, p))],
        pl.BlockSpec((None, SUBLANES, dv), lambda b, p, pt: (b, 0, 0)),
        jax.ShapeDtypeStruct((bd, SUBLANES, dv), F32))
    return pl.pallas_call(
        functools.partial(_diff_final_kernel, lam_init=lam_init),
        grid=(bd,),
        in_specs=[per_b((SUBLANES, dv)), per_b((nh, LANES)), per_b((nh, dv)), pl.BlockSpec((1, dv), lambda b: (0, 0))],
        out_specs=per_b((nh, dv)),
        out_shape=jax.ShapeDtypeStruct((bd, nh, dv), MXU_DTYPE),
        compiler_params=_params(("parallel",)),
        name="diff_decode_final",
    )(acc, a_new, vb_new.reshape(bd, nh, dv), g_sub.reshape(1, dv)).reshape(bd, w)


def _idx_score_kernel(pt_ref, qi_ref, wi_ref, ki_hbm, o_ref, buf, sem, *, layer, wi_scale):
    slot = _paged_step(ki_hbm, layer, pt_ref, buf, sem)
    _, g, page, _ = buf.shape
    w = _rnd(wi_ref[...] * wi_scale)
    for i in range(g):
        sc = jnp.maximum(_dot_nt(qi_ref[...], buf[slot, i]), 0.0)
        o_ref[:, i * page:(i + 1) * page] = jnp.sum(w * _rnd(sc), axis=0, keepdims=True) + 0.0


def _select_kernel(sc_ref, qi_ref, wi_ref, kn_ref, idx_ref, nsel_ref, mnew_ref, pos_s, *, topk, wi_scale):
    npg, page = sc_ref.shape
    sc_new = jnp.maximum(jnp.sum(_rnd(qi_ref[...]) * _rnd(kn_ref[...]), axis=1, keepdims=True), 0.0)
    s_new = jnp.sum(_rnd(wi_ref[...] * wi_scale) * _rnd(sc_new), axis=0, keepdims=True) + 0.0
    keys = _f32_key(sc_ref[...])
    key_new = _f32_key(s_new)

    def total(x):
        return jnp.sum(jnp.sum(x, axis=1, keepdims=True), axis=0, keepdims=True)

    def count(pred):
        return total(jnp.where(pred(keys), 1.0, 0.0)) + jnp.where(pred(key_new), 1.0, 0.0)

    def bit_step(it, prefix):
        bit = jnp.left_shift(jnp.int32(1), 31 - it)
        cand = (prefix | bit) ^ I32_MIN
        return jnp.where(count(lambda kk: kk >= cand) >= float(topk), prefix | bit, prefix)

    thr = lax.fori_loop(0, 32, bit_step, jnp.zeros((1, 1), I32)) ^ I32_MIN
    need = float(topk) - count(lambda kk: kk > thr)
    strict_upper = _strict_upper(page)
    strict_lower = jnp.where(_iota((npg, npg), 1) < _iota((npg, npg), 0), 1.0, 0.0).astype(MXU_DTYPE)

    def count_before(flag):
        in_row = jnp.dot(_mx(flag), strict_upper, preferred_element_type=F32)
        row_tot = jnp.broadcast_to(jnp.sum(flag, axis=1, keepdims=True), (npg, page))
        return in_row + jnp.dot(strict_lower, _mx(row_tot), preferred_element_type=F32)

    eq = keys == thr
    eqf = jnp.where(eq, 1.0, 0.0)
    sel = ((keys > thr) | (eq & (count_before(eqf) < need))) & (keys > KEY_NEG_INF)
    sel_new = ((key_new > thr) | ((key_new == thr) & (total(eqf) < need))) & (key_new > KEY_NEG_INF)
    mnew_ref[...] = jnp.broadcast_to(jnp.where(sel_new, 1.0, 0.0), mnew_ref.shape)
    self = jnp.where(sel, 1.0, 0.0)
    pos_s[...] = jnp.where(sel, count_before(self), -1.0)
    n_slots = idx_ref.shape[0]
    slot_id = _iota((n_slots, page), 0).astype(F32)
    lane = _iota((n_slots, page), 1).astype(F32)

    def gather_page(p, acc):
        hit = pos_s[pl.ds(p, 1), :] == slot_id
        return acc + jnp.where(hit, jnp.asarray(p * page, F32) + lane, 0.0)

    acc = lax.fori_loop(0, npg, gather_page, jnp.zeros((n_slots, page), F32))
    idx_ref[...] = jnp.broadcast_to(jnp.sum(acc, axis=1, keepdims=True).astype(I32), idx_ref.shape)
    nsel_ref[...] = jnp.broadcast_to(total(self), nsel_ref.shape)


def _dsa_gather_kernel(idx_ref, pt_ref, q_ref, kn_ref, vn_ref, mnew_ref, nsel_ref, k_hbm, v_hbm, o_ref,
                       kbuf, vbuf, sem, *, layer, scale):
    b = pl.program_id(0)
    n_slots, nh, dh = kbuf.shape
    page = k_hbm.shape[2]

    def row(cache, i, buf, r, s):
        return pltpu.make_async_copy(cache.at[layer, pt_ref[b, i // page], pl.ds(i % page, 1)],
                                     buf.at[pl.ds(r, 1)], sem.at[s])

    def issue(r, carry):
        i = idx_ref[b, r]
        row(k_hbm, i, kbuf, r, 0).start()
        row(v_hbm, i, vbuf, r, 1).start()
        return carry

    def drain(r, carry):
        row(k_hbm, 0, kbuf, 0, 0).wait()
        row(v_hbm, 0, vbuf, 0, 1).wait()
        return carry

    lax.fori_loop(0, n_slots, issue, 0)
    lax.fori_loop(0, n_slots, drain, 0)
    q = q_ref[...]
    head = _iota((nh, n_slots), 0)
    s = jnp.zeros((nh, n_slots), F32)
    for h in range(nh):
        s = jnp.where(head == h, _dot_nt(q, kbuf[:, h, :]), s)
    live = _iota((nh, n_slots), 1).astype(F32) < nsel_ref[:, 0:1]
    s = jnp.where(live, s * scale, NEG_INF)
    s_new = jnp.sum(_rnd(q) * _rnd(kn_ref[...]), axis=1, keepdims=True) * scale
    s_new = jnp.where(mnew_ref[:, 0:1] > 0.0, s_new, NEG_INF)
    m = jnp.maximum(jnp.max(s, axis=1, keepdims=True), s_new)
    e = jnp.exp(s - m)
    e_new = jnp.exp(s_new - m)
    z = jnp.sum(e, axis=1, keepdims=True) + e_new
    p = e / z
    o = jnp.zeros((nh, dh), F32)
    head = _iota((nh, dh), 0)
    for h in range(nh):
        o = jnp.where(head == h, _dot(p, vbuf[:, h, :]), o)
    o = o + _rnd(e_new / z) * _rnd(vn_ref[...])
    o_ref[...] = o.astype(o_ref.dtype)


def _dsa_decode(qc, kc_new, vc_new, qi, ki_new, wi, cache_k, cache_v, cache_i, layer, page_table, nh, n_idx_heads):
    bd, w = qc.shape
    dh = w // nh
    d_i = ki_new.shape[1]
    page = cache_k.shape[2]
    n_pages = page_table.shape[1]
    past = n_pages * page
    topk = min(TOPK_MAX, (past + 1) // 4)
    n_slots = -(-topk // SUBLANES) * SUBLANES
    qi3 = qi.reshape(bd, n_idx_heads, d_i)
    wi3 = wi[:, :n_idx_heads].reshape(bd, n_idx_heads, 1)
    wi_scale = (n_idx_heads * d_i) ** -0.5
    g = _pages_per_step(n_pages, 16)
    scores = _paged_call(
        functools.partial(_idx_score_kernel, layer=layer, wi_scale=wi_scale), "idx_scores", page_table, cache_i, g,
        [qi3, wi3],
        [pl.BlockSpec((None, n_idx_heads, d_i), lambda b, p, pt: (b, 0, 0)),
         pl.BlockSpec((None, n_idx_heads, 1), lambda b, p, pt: (b, 0, 0))],
        pl.BlockSpec((None, 1, g * page), lambda b, p, pt: (b, 0, p)),
        jax.ShapeDtypeStruct((bd, 1, past), F32))
    per_b = lambda shp: pl.BlockSpec((None,) + shp, lambda b: (b, 0, 0))
    sel_idx, n_sel, mask_new = pl.pallas_call(
        functools.partial(_select_kernel, topk=topk, wi_scale=wi_scale),
        grid=(bd,),
        in_specs=[per_b((n_pages, page)), per_b((n_idx_heads, d_i)), per_b((n_idx_heads, 1)), per_b((1, d_i))],
        out_specs=[per_b((n_slots, LANES)), per_b((1, LANES)), per_b((1, LANES))],
        out_shape=[jax.ShapeDtypeStruct((bd, n_slots, LANES), I32), jax.ShapeDtypeStruct((bd, 1, LANES), F32),
                   jax.ShapeDtypeStruct((bd, 1, LANES), F32)],
        scratch_shapes=[pltpu.VMEM((n_pages, page), F32)],
        compiler_params=_params(("parallel",)),
        name="idx_select",
    )(scores.reshape(bd, n_pages, page), qi3, wi3, ki_new.reshape(bd, 1, d_i))
    per_b2 = lambda shp: pl.BlockSpec((None,) + shp, lambda b, idx, pt: (b, 0, 0))
    hbm = pl.BlockSpec(memory_space=pl.ANY)
    return pl.pallas_call(
        functools.partial(_dsa_gather_kernel, layer=layer, scale=dh ** -0.5),
        grid_spec=pltpu.PrefetchScalarGridSpec(
            num_scalar_prefetch=2, grid=(bd,),
            in_specs=[per_b2((nh, dh)), per_b2((nh, dh)), per_b2((nh, dh)), per_b2((1, LANES)), per_b2((1, LANES)),
                      hbm, hbm],
            out_specs=per_b2((nh, dh)),
            scratch_shapes=[pltpu.VMEM((n_slots, nh, dh), F32), pltpu.VMEM((n_slots, nh, dh), F32),
                            pltpu.SemaphoreType.DMA((2,))]),
        out_shape=jax.ShapeDtypeStruct((bd, nh, dh), MXU_DTYPE),
        compiler_params=_params(("arbitrary",)),
        name="dsa_decode_attn",
    )(sel_idx[:, :, 0], page_table, qc.reshape(bd, nh, dh), kc_new.reshape(bd, nh, dh), vc_new.reshape(bd, nh, dh),
      mask_new, n_sel, cache_k, cache_v).reshape(bd, w)


def _route_kernel(lg_ref, br_ref, e_ref, g_ref, r_ref, cnt_ref, carry_s, *, n_valid, n_groups):
    i = pl.program_id(0)
    ne, tn = lg_ref.shape
    per = ne // n_groups

    @pl.when(i == 0)
    def _():
        carry_s[...] = jnp.zeros(carry_s.shape, F32)

    aff = _sigmoid(lg_ref[...])
    sel = aff + br_ref[...]
    sub = _iota((per, tn), 0).astype(F32)
    best = None
    for gi in range(n_groups):
        s = sel[gi * per:(gi + 1) * per]
        m1 = jnp.max(s, axis=0, keepdims=True)
        i1 = jnp.min(jnp.where(s == m1, sub, float(per)), axis=0, keepdims=True)
        s2 = jnp.where(sub == i1, NEG_INF, s)
        m2 = jnp.max(s2, axis=0, keepdims=True)
        i2 = jnp.min(jnp.where(s2 == m2, sub, float(per)), axis=0, keepdims=True)
        cand = (m1 + m2, float(gi * per) + i1, float(gi * per) + i2)
        if best is None:
            best = cand
        else:
            better = cand[0] > best[0]
            best = tuple(jnp.where(better, c, b) for c, b in zip(cand, best))
    e0, e1 = best[1].astype(I32), best[2].astype(I32)
    eid = _iota((ne, tn), 0)
    valid = (i * tn + _iota((1, tn), 1)) < n_valid
    oh0 = (eid == e0) & valid
    oh1 = (eid == e1) & valid
    a0 = jnp.sum(jnp.where(eid == e0, aff, 0.0), axis=0, keepdims=True)
    a1 = jnp.sum(jnp.where(eid == e1, aff, 0.0), axis=0, keepdims=True)
    tot = a0 + a1
    e_ref[...] = jnp.concatenate([e0, e1], axis=0)
    g_ref[...] = jnp.concatenate([a0 / tot, a1 / tot], axis=0)
    oh = jnp.where(oh0, 1.0, 0.0) + jnp.where(oh1, 1.0, 0.0)
    before = carry_s[:, 0:1] + jnp.dot(_mx(oh), _strict_upper(tn), preferred_element_type=F32)
    r0 = jnp.sum(jnp.where(oh0, before, 0.0), axis=0, keepdims=True)
    r1 = jnp.sum(jnp.where(oh1, before, 0.0), axis=0, keepdims=True)
    r_ref[...] = jnp.concatenate([r0, r1], axis=0).astype(I32)
    carry_s[...] = carry_s[...] + jnp.sum(oh, axis=1, keepdims=True)
    cnt_ref[...] = carry_s[...]


def _route(logits_t, b_router, n_valid):
    ne, mp = logits_t.shape
    tn = min(mp, 256)
    tok = lambda i: (0, i)
    return pl.pallas_call(
        functools.partial(_route_kernel, n_valid=n_valid, n_groups=N_GROUPS),
        grid=(mp // tn,),
        in_specs=[pl.BlockSpec((ne, tn), tok), pl.BlockSpec((ne, 1), lambda i: (0, 0))],
        out_specs=[pl.BlockSpec((TOP_K, tn), tok), pl.BlockSpec((TOP_K, tn), tok), pl.BlockSpec((TOP_K, tn), tok),
                   pl.BlockSpec((ne, LANES), lambda i: (0, 0))],
        out_shape=[jax.ShapeDtypeStruct((TOP_K, mp), I32), jax.ShapeDtypeStruct((TOP_K, mp), F32),
                   jax.ShapeDtypeStruct((TOP_K, mp), I32), jax.ShapeDtypeStruct((ne, LANES), F32)],
        scratch_shapes=[pltpu.VMEM((ne, LANES), F32)],
        compiler_params=_params(("arbitrary",)),
        name="moe_route",
    )(logits_t, b_router.reshape(ne, 1))


def _row_copy(src, s_row, dst, d_row, sem):
    return pltpu.make_async_copy(src.at[pl.ds(s_row, 1)], dst.at[pl.ds(d_row, 1)], sem)


def _expert_kernel(be_ref, nu_ref, src_ref, x_hbm, wg_ref, wu_ref, wd_ref, o_ref, xbuf, sem):
    b = pl.program_id(0)
    n_used = nu_ref[0]
    blk = xbuf.shape[1]

    def fetch(block):
        def issue(r, carry):
            _row_copy(x_hbm, src_ref[block * blk + r], xbuf.at[block % 2], r, sem.at[block % 2]).start()
            return carry
        lax.fori_loop(0, blk, issue, 0)

    @pl.when(b == 0)
    def _():
        fetch(b)

    @pl.when(b + 1 < n_used)
    def _():
        fetch(b + 1)

    @pl.when(b < n_used)
    def _():
        def drain(r, carry):
            _row_copy(x_hbm, 0, xbuf.at[b % 2], 0, sem.at[b % 2]).wait()
            return carry
        lax.fori_loop(0, blk, drain, 0)
        x = _mx(xbuf[b % 2])
        gate = jnp.dot(x, _mx(wg_ref[...]), preferred_element_type=F32)
        up = jnp.dot(x, _mx(wu_ref[...]), preferred_element_type=F32)
        hdn = gate * _sigmoid(gate) * up
        y = jnp.dot(_mx(hdn), _mx(wd_ref[...]), preferred_element_type=F32)
        o_ref[...] = _rnd(y)

    @pl.when(b >= n_used)
    def _():
        o_ref[...] = jnp.zeros(o_ref.shape, F32)


def _experts(blk_e, n_used, src, x, w_gate, w_up, w_down, layer, blk):
    n_rows = src.shape[0]
    d = x.shape[1]
    de = w_gate.shape[3]
    wspec = lambda shp: pl.BlockSpec((None, None) + shp, lambda b, be, nu, sr: (layer, be[b], 0, 0))
    return pl.pallas_call(
        _expert_kernel,
        grid_spec=pltpu.PrefetchScalarGridSpec(
            num_scalar_prefetch=3, grid=(n_rows // blk,),
            in_specs=[pl.BlockSpec(memory_space=pl.ANY), wspec((d, de)), wspec((d, de)), wspec((de, d))],
            out_specs=pl.BlockSpec((blk, d), lambda b, be, nu, sr: (b, 0)),
            scratch_shapes=[pltpu.VMEM((2, blk, d), F32), pltpu.SemaphoreType.DMA((2,))]),
        out_shape=jax.ShapeDtypeStruct((n_rows, d), F32),
        compiler_params=_params(("arbitrary",)),
        name="moe_experts",
    )(blk_e, n_used, src, x, w_gate, w_up, w_down)


def _combine_kernel(dest_ref, y_hbm, x_ref, gt_ref, g_ref, b_ref, xo_ref, xb_ref, rows_s, sem, *, n_tok, alpha):
    tc = x_ref.shape[0]
    base = pl.program_id(0) * tc

    def issue(r, carry):
        for kk in range(TOP_K):
            _row_copy(y_hbm, dest_ref[kk * n_tok + base + r], rows_s.at[kk], r, sem).start()
        return carry

    def drain(r, carry):
        for kk in range(TOP_K):
            _row_copy(y_hbm, 0, rows_s.at[kk], 0, sem).wait()
        return carry

    lax.fori_loop(0, tc, issue, 0)
    lax.fori_loop(0, tc, drain, 0)
    gt = gt_ref[...]
    y = _rnd(gt[:, 0:1]) * rows_s[0] + _rnd(gt[:, 1:2]) * rows_s[1]
    out = _layer_norm(alpha * x_ref[...] + y, g_ref[...], b_ref[...])
    xo_ref[...] = out
    xb_ref[...] = out.astype(xb_ref.dtype)


def _combine(dest_flat, y, x, gates, g, bt, alpha):
    n_tok, d = x.shape
    tc = min(n_tok, 128)
    row = lambda i, dst: (i, 0)
    fixed = lambda i, dst: (0, 0)
    return pl.pallas_call(
        functools.partial(_combine_kernel, n_tok=n_tok, alpha=alpha),
        grid_spec=pltpu.PrefetchScalarGridSpec(
            num_scalar_prefetch=1, grid=(n_tok // tc,),
            in_specs=[pl.BlockSpec(memory_space=pl.ANY), pl.BlockSpec((tc, d), row), pl.BlockSpec((tc, TOP_K), row),
                      pl.BlockSpec((1, d), fixed), pl.BlockSpec((1, d), fixed)],
            out_specs=[pl.BlockSpec((tc, d), row), pl.BlockSpec((tc, d), row)],
            scratch_shapes=[pltpu.VMEM((TOP_K, tc, d), F32), pltpu.SemaphoreType.DMA(())]),
        out_shape=[jax.ShapeDtypeStruct((n_tok, d), F32), jax.ShapeDtypeStruct((n_tok, d), MXU_DTYPE)],
        compiler_params=_params(("arbitrary",)),
        name="moe_combine",
    )(dest_flat, y, x, gates, g.reshape(1, d), bt.reshape(1, d))


def _moe_ln(x, xb, w_router_t, b_router, w_gate, w_up, w_down, layer, g, bt, alpha, n_valid, blk):
    m, d = x.shape
    ne = w_gate.shape[1]
    mp = max(m, LANES)
    logits_t = _mm_nt(w_router_t, xb)
    if mp != m:
        logits_t = jnp.pad(logits_t, ((0, 0), (0, mp - m)))
    eidx, gates, rank, counts = _route(logits_t, b_router, n_valid)
    counts = counts[:, 0].astype(I32)
    padded = (counts + blk - 1) // blk * blk
    pad_end = jnp.cumsum(padded)
    pad_start = pad_end - padded
    n_blocks = -(-(n_valid * TOP_K + ne * (blk - 1)) // blk)
    dest = (pad_start[eidx] + rank)[:, :m].astype(I32)
    n_rows = n_blocks * blk
    live = jnp.arange(m)[None, :] < n_valid
    tok = jnp.broadcast_to(jnp.arange(m, dtype=I32)[None, :], (TOP_K, m))
    src = jnp.zeros((n_rows,), I32).at[jnp.where(live, dest, n_rows).reshape(-1)].set(tok.reshape(-1), mode="drop")
    dest_flat = jnp.where(live, dest, 0).reshape(-1)
    blk_e = jnp.minimum(jnp.searchsorted(pad_end, jnp.arange(n_blocks) * blk, side="right"), ne - 1).astype(I32)
    n_used = (pad_end[-1:] // blk).astype(I32)
    y = _experts(blk_e, n_used, src, x, w_gate, w_up, w_down, layer, blk)
    return _combine(dest_flat, y, x, gates[:, :m].T, g, bt, alpha)


def _rope_tables(pos, half):
    inv = ROPE_THETA ** (-jnp.arange(half, dtype=F32) / half)
    ang = pos.astype(F32)[:, None] * inv[None, :]
    c, s = jnp.cos(ang), jnp.sin(ang)
    reps = LANES // (2 * half)
    return (jnp.tile(jnp.concatenate([c, c], -1), (1, reps)), jnp.tile(jnp.concatenate([-s, s], -1), (1, reps)), half)


def _cols(w, start, width, pad_to=None):
    out = w[:, start:start + width].astype(MXU_DTYPE)
    if pad_to is not None and pad_to > width:
        out = jnp.pad(out, ((0, 0), (0, pad_to - width)))
    return out


def kernel(x_prompt, x_sample, state_mlstm_c, state_mlstm_n, state_mlstm_m, cache_diff_k, cache_diff_v, cache_dsa_k, cache_dsa_v, cache_idx_k, state_ret, page_table, w_in_even, w_out_even, b_igate, b_fgate, g_mlstm, lam_q1, lam_k1, lam_q2, lam_k2, g_subln, w_in_odd, w_out_odd, g_ret, ln_mix_g, ln_mix_b, ln_ffn_g, ln_ffn_b, w_router, b_router, w_gate, w_up, w_down):
    bsz, seq, d_model = x_prompt.shape
    dec_b, dec_seq, _ = x_sample.shape
    assert dec_seq == 1
    depth = w_gate.shape[0]
    alpha = (2 * depth) ** 0.25
    h_a, dh_a = g_mlstm.shape[1:]
    h_b, dv_b = cache_diff_k.shape[3:]
    h_c, dh_c = cache_dsa_k.shape[3:]
    d_i = cache_idx_k.shape[3]
    h_d, dk_d, dv_d = state_ret.shape[2:]
    w_a, w_b, w_c, w_d = h_a * dh_a, h_b * dv_b, h_c * dh_c, h_d * dv_d
    h_i = w_in_odd.shape[2] - (3 * w_c + d_i + 2 * h_d * dk_d + 2 * w_d)
    h_i = h_i // (d_i + 1)
    past_len = page_table.shape[1] * cache_diff_k.shape[2]
    m_p = bsz * seq
    m_s = 2 * SUBLANES

    pos_p = jnp.arange(seq)
    pos_s = jnp.full((m_s,), past_len)
    rope_p = {h: _rope_tables(pos_p, h) for h in (dh_c // 2, d_i // 2)}
    rope_s = {h: _rope_tables(pos_s, h) for h in (dh_c // 2, d_i // 2)}
    log_gamma = jnp.log(1.0 - 2.0 ** (-5.0 - jnp.arange(h_d, dtype=F32)))
    w_router_t = w_router.T.astype(MXU_DTYPE)

    xp = x_prompt.reshape(m_p, d_model)
    xs = jnp.pad(x_sample.reshape(dec_b, d_model), ((0, m_s - dec_b), (0, 0)))
    xp_b, xs_b = xp.astype(MXU_DTYPE), xs.astype(MXU_DTYPE)
    names = ("mlstm_c", "mlstm_n", "mlstm_m", "diff_k", "diff_v", "dsa_k", "dsa_v", "idx_k", "ret")
    new_p = {k: [] for k in names}
    new_s = {k: [] for k in names}

    def pad_tokens(a):
        return jnp.pad(a[:dec_b, None, :], ((0, 0), (0, CHUNK - 1), (0, 0)))

    for l in range(depth):
        j = l // 2
        if l % 2 == 0:
            w = w_in_even[j]
            o_g = 4 * w_a
            o_b = o_g + 2 * h_a
            w_main = _cols(w, 0, 4 * w_a)
            w_gates_t = w[:, o_g:o_b].T.astype(MXU_DTYPE)
            w_qb, w_kb, w_vb = (_cols(w, o_b + i * w_b, w_b) for i in range(3))
            lams = (lam_q1[j], lam_k1[j], lam_q2[j], lam_k2[j])
            lam_init = 0.8 - 0.6 * math.exp(-0.3 * l)
            half = dv_b // 4
            z = _mm(xp_b, w_main).reshape(bsz, seq, 4 * w_a)
            gt = _mm_nt(w_gates_t, xp_b).reshape(2 * h_a, bsz, seq).transpose(1, 0, 2).reshape(bsz, 2 * h_a, 1, seq)
            r3 = lambda a: a.reshape(bsz, seq, a.shape[1])
            qb_m = _mm(xp_b, w_qb, rope=rope_p[half], out_dtype=MXU_DTYPE)
            kb, kb_m = _mm(xp_b, w_kb, rope=rope_p[half], mxu_copy=True)
            vb, vb_m = _mm(xp_b, w_vb, mxu_copy=True)
            zero = lambda *s: jnp.zeros(s, F32)
            h_mix, c_p, n_p, mm_p = _mlstm(z, gt, b_igate[j], b_fgate[j], g_mlstm[j], zero(bsz, h_a, dh_a, dh_a),
                                           zero(bsz, h_a, dh_a), zero(bsz, h_a), CHUNK)
            ob = _diff_attn(r3(qb_m), r3(kb_m), r3(vb_m), lams, g_subln[j], lam_init, h_b)
            mix_a_p, mix_b_p = h_mix.reshape(m_p, w_a), ob.reshape(m_p, w_b)
            st_p = (c_p, n_p.reshape(bsz, h_a, dh_a), mm_p.reshape(bsz, h_a),
                    kb.reshape(bsz, seq, h_b, dv_b), vb.reshape(bsz, seq, h_b, dv_b))
            zs = pad_tokens(_mm(xs_b, w_main))
            gts = _mm_nt(w_gates_t, xs_b)[:, :dec_b].T
            gts = jnp.pad(gts[:, :, None, None], ((0, 0), (0, 0), (0, 0), (0, CHUNK - 1)))
            qbs = _mm(xs_b, w_qb, rope=rope_s[half])[:dec_b]
            kbs = _mm(xs_b, w_kb, rope=rope_s[half])[:dec_b]
            vbs = _mm(xs_b, w_vb)[:dec_b]
            hs_mix, c_s, n_s, mm_s = _mlstm(zs, gts, b_igate[j], b_fgate[j], g_mlstm[j], state_mlstm_c[j],
                                            state_mlstm_n[j], state_mlstm_m[j], 1)
            obs = _diff_decode(qbs, kbs, vbs, cache_diff_k, cache_diff_v, j, page_table, lams, g_subln[j], lam_init, h_b)
            pad_rows = lambda a: jnp.pad(a, ((0, m_s - dec_b), (0, 0)))
            mix_a_s, mix_b_s = pad_rows(hs_mix[:, 0, :]), pad_rows(obs)
            st_s = (c_s, n_s.reshape(dec_b, h_a, dh_a), mm_s.reshape(dec_b, h_a),
                    kbs.reshape(dec_b, 1, h_b, dv_b), vbs.reshape(dec_b, 1, h_b, dv_b))
            w_out = w_out_even[j].astype(MXU_DTYPE)
            keys = names[:5]
        else:
            w = w_in_odd[j]
            o_qi = 3 * w_c
            o_ki = o_qi + h_i * d_i
            o_wi = o_ki + d_i
            o_qd = o_wi + h_i
            o_vd = o_qd + 2 * h_d * dk_d
            w_qc, w_kc, w_vc = (_cols(w, i * w_c, w_c) for i in range(3))
            w_qi = _cols(w, o_qi, h_i * d_i)
            w_ki = _cols(w, o_ki, d_i, pad_to=LANES)
            w_wi = _cols(w, o_wi, h_i, pad_to=LANES)
            w_qkd = _cols(w, o_qd, 2 * h_d * dk_d)
            w_vgd = _cols(w, o_vd, 2 * w_d)
            hc, hi, hd = dh_c // 2, d_i // 2, dk_d // 2

            def project(xb_, rope, q_dtype):
                qc = _mm(xb_, w_qc, rope=rope[hc], out_dtype=q_dtype)
                kc, kc_m = _mm(xb_, w_kc, rope=rope[hc], mxu_copy=True)
                vc, vc_m = _mm(xb_, w_vc, mxu_copy=True)
                qi = _mm(xb_, w_qi, rope=rope[hi], out_dtype=q_dtype)
                ki = _mm(xb_, w_ki, rope=rope[hi], n_out=d_i)
                wi = _mm(xb_, w_wi)
                qkd = _mm(xb_, w_qkd, rope=rope[hd])
                vgd = _mm(xb_, w_vgd)
                return (qc, kc, vc, qi, ki, wi, qkd, vgd), (kc_m, vc_m)

            (qc, kc, vc, qi, ki, wi, qkd, vgd), (kc_m, vc_m) = project(xp_b, rope_p, MXU_DTYPE)
            r3 = lambda a: a.reshape(bsz, seq, a.shape[1])
            oc = _dsa_attn(r3(qc), r3(kc_m), r3(vc_m), r3(qi), r3(ki), r3(wi), h_c, h_i)
            od, s_p = _retention(r3(qkd), r3(vgd), log_gamma, g_ret[j], jnp.zeros((bsz, h_d, dk_d, dv_d), F32), CHUNK)
            mix_a_p, mix_b_p = oc.reshape(m_p, w_c), od.reshape(m_p, w_d)
            st_p = (kc.reshape(bsz, seq, h_c, dh_c), vc.reshape(bsz, seq, h_c, dh_c), ki.reshape(bsz, seq, d_i), s_p)
            qc, kc, vc, qi, ki, wi, qkd, vgd = (a[:dec_b] for a in project(xs_b, rope_s, F32)[0])
            ocs = _dsa_decode(qc, kc, vc, qi, ki, wi, cache_dsa_k, cache_dsa_v, cache_idx_k, j, page_table, h_c, h_i)
            ods, s_s = _retention(pad_tokens(qkd), pad_tokens(vgd), log_gamma, g_ret[j], state_ret[j], 1)
            pad_rows = lambda a: jnp.pad(a, ((0, m_s - dec_b), (0, 0)))
            mix_a_s, mix_b_s = pad_rows(ocs), pad_rows(ods[:, 0, :])
            st_s = (kc.reshape(dec_b, 1, h_c, dh_c), vc.reshape(dec_b, 1, h_c, dh_c), ki.reshape(dec_b, 1, d_i), s_s)
            w_out = w_out_odd[j].astype(MXU_DTYPE)
            keys = names[5:]
        for k, a, b in zip(keys, st_p, st_s):
            new_p[k].append(a)
            new_s[k].append(b)
        xp, xp_b = _proj_ln(mix_a_p, mix_b_p, w_out, xp, ln_mix_g[l], ln_mix_b[l], alpha)
        xs, xs_b = _proj_ln(mix_a_s, mix_b_s, w_out, xs, ln_mix_g[l], ln_mix_b[l], alpha)
        moe_w = (w_router_t, b_router, w_gate, w_up, w_down, l, ln_ffn_g[l], ln_ffn_b[l], alpha)
        xp, xp_b = _moe_ln(xp, xp_b, *moe_w, m_p, 256)
        xs, xs_b = _moe_ln(xs, xs_b, *moe_w, dec_b, 16)

    out = [xp.reshape(bsz, seq, d_model), xs[:dec_b].reshape(dec_b, 1, d_model)]
    for k in names:
        out += [jnp.stack(new_p[k]), jnp.stack(new_s[k])]
    return tuple(out)
```

```python
import functools
import math

import jax
import jax.numpy as jnp
from jax import lax
from jax.experimental import pallas as pl
from jax.experimental.pallas import tpu as pltpu

F32 = jnp.float32
I32 = jnp.int32
MXU_DTYPE = jnp.bfloat16

N_GROUPS = 4
TOP_K = 2
TOPK_MAX = 256
CHUNK = 128
ROPE_THETA = 10000.0
EPS = 1e-5

LANES = 128
SUBLANES = 8
VMEM_LIMIT_BYTES = 52 * 1024 * 1024

NEG_INF = float("-inf")


def _mx(x):
    return x.astype(MXU_DTYPE)


def _rnd(x):
    return x.astype(MXU_DTYPE).astype(F32)


def _dot(a, b):
    return jnp.dot(_mx(a), _mx(b), preferred_element_type=F32)


def _dot_nt(a, b):
    return lax.dot_general(_mx(a), _mx(b), (((1,), (1,)), ((), ())), preferred_element_type=F32)


def _params(sem, vmem=VMEM_LIMIT_BYTES):
    return pltpu.CompilerParams(dimension_semantics=sem, vmem_limit_bytes=vmem)


def _iota(shape, axis):
    return lax.broadcasted_iota(I32, shape, axis)


def _sigmoid(x):
    return 1.0 / (1.0 + jnp.exp(-x))


def _row_to_col(row, n):
    eye = _iota((n, n), 0) == _iota((n, n), 1)
    return jnp.sum(jnp.where(eye, jnp.broadcast_to(row, (n, n)), 0.0), axis=1, keepdims=True)


def _head_norm(x, g):
    mu = jnp.mean(x, axis=-1, keepdims=True)
    xc = x - mu
    var = jnp.mean(xc * xc, axis=-1, keepdims=True)
    return xc * lax.rsqrt(var + EPS) * g


def _layer_norm(z, g, b):
    mu = jnp.mean(z, axis=-1, keepdims=True)
    zc = z - mu
    var = jnp.mean(zc * zc, axis=-1, keepdims=True)
    return zc * lax.rsqrt(var + EPS) * g + b


def _rope_apply(a, cos, sin, half):
    if 2 * half == LANES:
        r = pltpu.roll(a, half, 1)
    else:
        lane = _iota(a.shape, 1)
        r = jnp.where((lane % (2 * half)) < half, pltpu.roll(a, LANES - half, 1), pltpu.roll(a, half, 1))
    return a * cos + r * sin


def _mm_kernel(x_ref, w_ref, *rest, rope_half, n_outs):
    o_refs = rest[len(rest) - n_outs:]
    acc = jnp.dot(x_ref[...], w_ref[...], preferred_element_type=F32)
    if rope_half:
        cos = rest[0][...]
        sin = rest[1][...]
        n_out = o_refs[0].shape[1]
        for j in range(acc.shape[1] // LANES):
            res = _rope_apply(acc[:, j * LANES:(j + 1) * LANES], cos, sin, rope_half)
            nw = min(LANES, n_out - j * LANES)
            for o_ref in o_refs:
                o_ref[:, j * LANES:j * LANES + nw] = res[:, :nw].astype(o_ref.dtype)
    else:
        for o_ref in o_refs:
            o_ref[...] = acc.astype(o_ref.dtype)


def _mm(x, w, rope=None, n_out=None, out_dtype=F32, mxu_copy=False):
    m, k = x.shape
    n = w.shape[1]
    n_out = n if n_out is None else n_out
    tm = min(m, 512)
    tn = min(n, 512)
    assert m % tm == 0 and n % tn == 0 and (n_out == n or n == tn)
    in_specs = [pl.BlockSpec((tm, k), lambda i, j: (i, 0)), pl.BlockSpec((k, tn), lambda i, j: (0, j))]
    args = [x, w]
    half = 0
    if rope is not None:
        cos, sin, half = rope
        nt = cos.shape[0] // tm
        assert cos.shape[0] % tm == 0
        in_specs += [pl.BlockSpec((tm, LANES), lambda i, j: (i % nt, 0))] * 2
        args += [cos, sin]
    out_spec = pl.BlockSpec((tm, min(tn, n_out)), lambda i, j: (i, j))
    dtypes = (out_dtype, MXU_DTYPE) if mxu_copy else (out_dtype,)
    outs = pl.pallas_call(
        functools.partial(_mm_kernel, rope_half=half, n_outs=len(dtypes)),
        grid=(m // tm, n // tn),
        in_specs=in_specs,
        out_specs=[out_spec] * len(dtypes),
        out_shape=[jax.ShapeDtypeStruct((m, n_out), dt) for dt in dtypes],
        compiler_params=_params(("parallel", "parallel")),
        name="proj",
    )(*args)
    return outs if mxu_copy else outs[0]


def _mm_nt_kernel(w_ref, x_ref, o_ref):
    o_ref[...] = lax.dot_general(w_ref[...], x_ref[...], (((1,), (1,)), ((), ())), preferred_element_type=F32)


def _mm_nt(w_t, x):
    n, k = w_t.shape
    m = x.shape[0]
    tm = min(m, 512)
    return pl.pallas_call(
        _mm_nt_kernel,
        grid=(m // tm,),
        in_specs=[pl.BlockSpec((n, k), lambda i: (0, 0)), pl.BlockSpec((tm, k), lambda i: (i, 0))],
        out_specs=pl.BlockSpec((n, tm), lambda i: (0, i)),
        out_shape=jax.ShapeDtypeStruct((n, m), F32),
        compiler_params=_params(("parallel",)),
        name="proj_t",
    )(w_t, x)


def _proj_ln_kernel(a_ref, b_ref, wa_ref, wb_ref, x_ref, g_ref, bt_ref, xo_ref, xb_ref, *, alpha):
    y = jnp.dot(a_ref[...], wa_ref[...], preferred_element_type=F32)
    y = y + jnp.dot(b_ref[...], wb_ref[...], preferred_element_type=F32)
    out = _layer_norm(alpha * x_ref[...] + y, g_ref[...], bt_ref[...])
    xo_ref[...] = out
    xb_ref[...] = out.astype(xb_ref.dtype)


def _proj_ln(a, b, w_out, x, g, bt, alpha):
    m, d = x.shape
    ka = a.shape[1]
    tm = min(m, 256)
    wa, wb = w_out[:ka], w_out[ka:]
    row = lambda i: (i, 0)
    fixed = lambda i: (0, 0)
    return pl.pallas_call(
        functools.partial(_proj_ln_kernel, alpha=alpha),
        grid=(m // tm,),
        in_specs=[pl.BlockSpec((tm, ka), row), pl.BlockSpec((tm, b.shape[1]), row),
                  pl.BlockSpec(wa.shape, fixed), pl.BlockSpec(wb.shape, fixed),
                  pl.BlockSpec((tm, d), row), pl.BlockSpec((1, d), fixed), pl.BlockSpec((1, d), fixed)],
        out_specs=[pl.BlockSpec((tm, d), row), pl.BlockSpec((tm, d), row)],
        out_shape=[jax.ShapeDtypeStruct((m, d), F32), jax.ShapeDtypeStruct((m, d), MXU_DTYPE)],
        compiler_params=_params(("parallel",)),
        name="out_proj_ln",
    )(a, b, wa, wb, x, g.reshape(1, d), bt.reshape(1, d))


def _lane_scan(x, op, fill):
    n = x.shape[1]
    lane = _iota(x.shape, 1)
    s = 1
    while s < n:
        x = op(x, jnp.where(lane >= s, pltpu.roll(x, s, 1), fill))
        s *= 2
    return x


def _mlstm_head(q, k, v, o_gate, ig, fpre, g, c_prev, n_prev, m_prev, valid_len):
    L = q.shape[0]
    lf = jnp.minimum(fpre, 0.0) - jnp.log1p(jnp.exp(-jnp.abs(fpre)))
    if valid_len < L:
        live = _iota((1, L), 1) < valid_len
        ig = jnp.where(live, ig, NEG_INF)
        lf = jnp.where(live, lf, 0.0)
    a_row = _lane_scan(lf, jnp.add, 0.0)
    b_row = ig - a_row
    run_row = jnp.maximum(_lane_scan(b_row, jnp.maximum, NEG_INF), m_prev)
    run_col = _row_to_col(run_row, L)
    causal = _iota((L, L), 1) <= _iota((L, L), 0)
    dmat = jnp.where(causal, jnp.exp(b_row - run_col), 0.0)
    w_inter = jnp.exp(m_prev - run_col)
    s = _dot_nt(q, k) * dmat
    num = w_inter * _dot(q, c_prev) + _dot(s, v)
    qn = jnp.sum(_rnd(q) * _rnd(n_prev), axis=1, keepdims=True)
    den = w_inter * qn + jnp.sum(s, axis=1, keepdims=True)
    a_col = _row_to_col(a_row, L)
    m_t = a_col + run_col
    h = num / jnp.maximum(jnp.abs(den), jnp.exp(-m_t))
    h = _head_norm(h, g) * _sigmoid(o_gate)
    a_last = a_row[:, L - 1:L]
    m_new = a_last + run_row[:, L - 1:L]
    w_c = jnp.exp(a_last + m_prev - m_new)
    ws_row = jnp.exp(a_last - a_row + ig - m_new)
    ws_col = _row_to_col(ws_row, L)
    kw = k * ws_col
    c_new = w_c * c_prev + _dot(kw.T, v)
    n_new = w_c * n_prev + jnp.sum(_rnd(ws_col) * _rnd(k), axis=0, keepdims=True)
    return h, c_new, n_new, m_new


def _mlstm_kernel(q_ref, k_ref, v_ref, o_ref, ig_ref, fg_ref, big_ref, bfg_ref, g_ref, c0_ref, n0_ref, m0_ref,
                  h_ref, c_ref, n_ref, m_ref, c_s, n_s, m_s, *, valid_len, scale):
    ci = pl.program_id(1)

    @pl.when(ci == 0)
    def _():
        c_s[...] = c0_ref[...]
        n_s[...] = n0_ref[...]
        m_s[...] = m0_ref[...]

    nh, dh, _ = c_s.shape
    for hh in range(nh):
        cols = slice(hh * dh, (hh + 1) * dh)
        h, c_new, n_new, m_new = _mlstm_head(
            q_ref[:, cols], k_ref[:, cols] * scale, v_ref[:, cols], o_ref[:, cols],
            ig_ref[hh] + big_ref[hh], fg_ref[hh] + bfg_ref[hh], g_ref[hh], c_s[hh], n_s[hh], m_s[hh], valid_len)
        h_ref[:, cols] = h.astype(h_ref.dtype)
        c_s[hh] = c_new
        n_s[hh] = n_new
        m_s[hh] = m_new

    @pl.when(ci == pl.num_programs(1) - 1)
    def _():
        c_ref[...] = c_s[...]
        n_ref[...] = n_s[...]
        m_ref[...] = m_s[...]


def _mlstm(z, gates_t, b_ig, b_fg, g_mh, c0, n0, m0, valid_len):
    bsz, t, _ = z.shape
    nh, dh = g_mh.shape
    L = CHUNK
    nc = t // L
    w = nh * dh
    assert dh == LANES and t % L == 0
    col = lambda part: pl.BlockSpec((None, L, w), lambda b, c: (b, c, part))
    gate = lambda part: pl.BlockSpec((None, nh, 1, L), lambda b, c: (b, part, 0, c))
    per_head = lambda shp: pl.BlockSpec((nh,) + shp, lambda b, c: (0, 0, 0))
    state = lambda shp: pl.BlockSpec((None, nh) + shp, lambda b, c: (b, 0, 0, 0))
    return pl.pallas_call(
        functools.partial(_mlstm_kernel, valid_len=valid_len, scale=dh ** -0.5),
        grid=(bsz, nc),
        in_specs=[col(0), col(1), col(2), col(3), gate(0), gate(1),
                  per_head((1, 1)), per_head((1, 1)), per_head((1, dh)),
                  state((dh, dh)), state((1, dh)), state((1, 1))],
        out_specs=[pl.BlockSpec((None, L, w), lambda b, c: (b, c, 0)),
                   state((dh, dh)), state((1, dh)), state((1, 1))],
        out_shape=[jax.ShapeDtypeStruct((bsz, t, w), MXU_DTYPE),
                   jax.ShapeDtypeStruct((bsz, nh, dh, dh), F32),
                   jax.ShapeDtypeStruct((bsz, nh, 1, dh), F32),
                   jax.ShapeDtypeStruct((bsz, nh, 1, 1), F32)],
        scratch_shapes=[pltpu.VMEM((nh, dh, dh), F32), pltpu.VMEM((nh, 1, dh), F32), pltpu.VMEM((nh, 1, 1), F32)],
        compiler_params=_params(("parallel", "arbitrary")),
        name="mlstm",
    )(z, z, z, z, gates_t, gates_t, b_ig.reshape(nh, 1, 1), b_fg.reshape(nh, 1, 1), g_mh.reshape(nh, 1, dh),
      c0, n0.reshape(bsz, nh, 1, dh), m0.reshape(bsz, nh, 1, 1))


def _ret_kernel(q_ref, k_ref, v_ref, gd_ref, lg_ref, g_ref, s0_ref, o_ref, s_ref, s_s, *, true_len, scale):
    ci = pl.program_id(1)

    @pl.when(ci == 0)
    def _():
        s_s[...] = s0_ref[...]

    L = q_ref.shape[0]
    nh, dk, dv = s_s.shape
    t_col = _iota((L, 1), 0).astype(F32)
    diff = (_iota((L, L), 0) - _iota((L, L), 1)).astype(F32)
    causal = diff >= 0.0
    chunk_len = float(min(L, true_len))
    for hh in range(nh):
        q = q_ref[:, hh * dk:(hh + 1) * dk]
        k = k_ref[:, hh * dk:(hh + 1) * dk] * scale
        v = v_ref[:, hh * dv:(hh + 1) * dv]
        lg = lg_ref[hh]
        decay = jnp.where(causal, jnp.exp(jnp.where(causal, diff, 0.0) * lg), 0.0)
        s_prev = s_s[hh]
        inner = _dot_nt(q, k) * decay
        o = _dot(inner, v) + jnp.exp((t_col + 1.0) * lg) * _dot(q, s_prev)
        ws_col = jnp.where(t_col < chunk_len, jnp.exp((chunk_len - 1.0 - t_col) * lg), 0.0)
        s_s[hh] = jnp.exp(chunk_len * lg) * s_prev + _dot((k * ws_col).T, v)
        gd = gd_ref[:, hh * dv:(hh + 1) * dv]
        o_ref[:, hh * dv:(hh + 1) * dv] = (_head_norm(o, g_ref[hh]) * (gd * _sigmoid(gd))).astype(o_ref.dtype)

    @pl.when(ci == pl.num_programs(1) - 1)
    def _():
        s_ref[...] = s_s[...]


def _retention(qk, vg, log_gamma, g_ret, s0, true_len):
    bsz, t, _ = qk.shape
    nh, dv = g_ret.shape
    dk = s0.shape[2]
    L = CHUNK
    nc = t // L
    assert dk % LANES == 0 and dv % LANES == 0
    half = lambda wdt, part: pl.BlockSpec((None, L, nh * wdt), lambda b, c: (b, c, part))
    return pl.pallas_call(
        functools.partial(_ret_kernel, true_len=true_len, scale=dk ** -0.5),
        grid=(bsz, nc),
        in_specs=[half(dk, 0), half(dk, 1), half(dv, 0), half(dv, 1),
                  pl.BlockSpec((nh, 1, 1), lambda b, c: (0, 0, 0)),
                  pl.BlockSpec((nh, 1, dv), lambda b, c: (0, 0, 0)),
                  pl.BlockSpec((None, nh, dk, dv), lambda b, c: (b, 0, 0, 0))],
        out_specs=[pl.BlockSpec((None, L, nh * dv), lambda b, c: (b, c, 0)),
                   pl.BlockSpec((None, nh, dk, dv), lambda b, c: (b, 0, 0, 0))],
        out_shape=[jax.ShapeDtypeStruct((bsz, t, nh * dv), MXU_DTYPE),
                   jax.ShapeDtypeStruct((bsz, nh, dk, dv), F32)],
        scratch_shapes=[pltpu.VMEM((nh, dk, dv), F32)],
        compiler_params=_params(("parallel", "arbitrary")),
        name="retention",
    )(qk, qk, vg, vg, log_gamma.reshape(nh, 1, 1), g_ret.reshape(nh, 1, dv), s0)


def _lambda(lq1, lk1, lq2, lk2, lam_init):
    s1 = jnp.sum(lq1 * lk1, axis=1, keepdims=True)
    s2 = jnp.sum(lq2 * lk2, axis=1, keepdims=True)
    return jnp.exp(s1) - jnp.exp(s2) + lam_init


def _rms_sub(o, g, lam_init):
    return o * lax.rsqrt(jnp.mean(o * o, axis=-1, keepdims=True) + EPS) * g * (1.0 - lam_init)


def _diff_kernel(qi_ref, kj_ref, q_ref, k_ref, v_ref, lq1_ref, lk1_ref, lq2_ref, lk2_ref, g_ref, o_ref,
                 m_s, l_s, acc_s, *, lam_init, scale):
    qi = qi_ref[pl.program_id(2)]
    kj = kj_ref[pl.program_id(2)]
    tq, dv = q_ref.shape
    tk = k_ref.shape[0]
    dh = dv // 2

    @pl.when(kj == 0)
    def _():
        m_s[...] = jnp.full(m_s.shape, NEG_INF, F32)
        l_s[...] = jnp.zeros(l_s.shape, F32)
        acc_s[...] = jnp.zeros(acc_s.shape, F32)

    qpos = qi * tq + _iota((tq, tk), 0)
    kpos = kj * tk + _iota((tq, tk), 1)
    ok = kpos <= qpos
    vb = _mx(v_ref[...])
    for c in range(2):
        s = _dot_nt(q_ref[:, c * dh:(c + 1) * dh], k_ref[:, c * dh:(c + 1) * dh]) * scale
        s = jnp.where(ok, s, NEG_INF)
        m_old = m_s[c]
        m_new = jnp.maximum(m_old, jnp.max(s, axis=1, keepdims=True))
        alpha = jnp.exp(m_old - m_new)
        p = jnp.exp(s - m_new)
        l_s[c] = alpha * l_s[c] + jnp.sum(p, axis=1, keepdims=True)
        acc_s[c] = alpha * acc_s[c] + jnp.dot(_mx(p), vb, preferred_element_type=F32)
        m_s[c] = m_new

    @pl.when(kj == qi)
    def _():
        lam = _lambda(lq1_ref[...], lk1_ref[...], lq2_ref[...], lk2_ref[...], lam_init)
        o = acc_s[0] / l_s[0] - lam * (acc_s[1] / l_s[1])
        o_ref[...] = _rms_sub(o, g_ref[...], lam_init).astype(o_ref.dtype)


def _diff_attn(qb, kb, vb, lams, g_sub, lam_init, nh):
    bsz, t, w = qb.shape
    dv = w // nh
    dh = dv // 2
    tq = tk = min(t, 512)
    nq = t // tq
    pairs = [(i, j) for i in range(nq) for j in range(i + 1)]
    qi_tab = jnp.asarray([p[0] for p in pairs], I32)
    kj_tab = jnp.asarray([p[1] for p in pairs], I32)
    vec = pl.BlockSpec((1, dh), lambda b, h, p, qt, kt: (0, 0))
    qo = pl.BlockSpec((None, tq, dv), lambda b, h, p, qt, kt: (b, qt[p], h))
    kv = pl.BlockSpec((None, tk, dv), lambda b, h, p, qt, kt: (b, kt[p], h))
    return pl.pallas_call(
        functools.partial(_diff_kernel, lam_init=lam_init, scale=dh ** -0.5),
        grid_spec=pltpu.PrefetchScalarGridSpec(
            num_scalar_prefetch=2, grid=(bsz, nh, len(pairs)),
            in_specs=[qo, kv, kv, vec, vec, vec, vec, pl.BlockSpec((1, dv), lambda b, h, p, qt, kt: (0, 0))],
            out_specs=qo,
            scratch_shapes=[pltpu.VMEM((2, tq, 1), F32), pltpu.VMEM((2, tq, 1), F32), pltpu.VMEM((2, tq, dv), F32)]),
        out_shape=jax.ShapeDtypeStruct((bsz, t, w), MXU_DTYPE),
        compiler_params=_params(("parallel", "parallel", "arbitrary")),
        name="diff_attn",
    )(qi_tab, kj_tab, qb, kb, vb, *[x.reshape(1, dh) for x in lams], g_sub.reshape(1, dv))


def _f32_key(x):
    i = lax.bitcast_convert_type(x, I32)
    return i ^ ((i >> 31) & 0x7FFFFFFF)


KEY_NEG_INF = -2139095041
I32_MIN = -2 ** 31


def _strict_upper(n):
    return jnp.where(_iota((n, n), 0) < _iota((n, n), 1), 1.0, 0.0).astype(MXU_DTYPE)


def _dsa_kernel(qc_ref, kc_ref, vc_ref, qi_ref, ki_ref, wi_ref, o_ref, key_s, bias_s, m_s, l_s, acc_s,
                *, topk, n_heads, n_idx_heads, scale, wi_scale):
    qb = pl.program_id(1)
    tq = qc_ref.shape[0]
    dh = qc_ref.shape[1] // n_heads
    ck = key_s.shape[2]
    d_i = ki_ref.shape[1]
    q0 = qb * tq
    n_chunks = (q0 + tq + ck - 1) // ck

    def _select():
        qi = qi_ref[...]
        wi = _rnd(wi_ref[...] * wi_scale)
        qpos = q0 + _iota((tq, ck), 0)

        def score_chunk(c, carry):
            start = pl.multiple_of(c * ck, ck)
            kic = ki_ref[pl.ds(start, ck), :]
            acc = jnp.zeros((tq, ck), F32)
            for hh in range(n_idx_heads):
                sc = jnp.maximum(_dot_nt(qi[:, hh * d_i:(hh + 1) * d_i], kic), 0.0)
                acc = acc + wi[:, hh:hh + 1] * sc
            acc = acc + 0.0
            kpos = c * ck + _iota((tq, ck), 1)
            key_s[c] = jnp.where(kpos <= qpos, _f32_key(acc), KEY_NEG_INF)
            return carry

        lax.fori_loop(0, n_chunks, score_chunk, 0)

        def count(pred):
            def body(c, acc):
                hit = jnp.where(pred(key_s[c]), 1.0, 0.0)
                part = hit[:, 0:LANES]
                for j in range(1, ck // LANES):
                    part = part + hit[:, j * LANES:(j + 1) * LANES]
                return acc + part
            acc = lax.fori_loop(0, n_chunks, body, jnp.zeros((tq, LANES), F32))
            return jnp.sum(acc, axis=1, keepdims=True)

        def bit_step(it, prefix):
            bit = jnp.left_shift(jnp.int32(1), 31 - it)
            cand = (prefix | bit) ^ I32_MIN
            cnt = count(lambda kk: kk >= cand)
            return jnp.where(cnt >= float(topk), prefix | bit, prefix)

        prefix = lax.fori_loop(0, 32, bit_step, jnp.zeros((tq, 1), I32))
        thr = prefix ^ I32_MIN
        need = float(topk) - count(lambda kk: kk > thr)
        upper = _strict_upper(LANES)

        def mask_chunk(c, taken):
            kk = key_s[c]
            cols = []
            for j in range(ck // LANES):
                kj = kk[:, j * LANES:(j + 1) * LANES]
                eq = kj == thr
                eqf = jnp.where(eq, 1.0, 0.0)
                before = taken + jnp.dot(_mx(eqf), upper, preferred_element_type=F32)
                sel = ((kj > thr) | (eq & (before < need))) & (kj > KEY_NEG_INF)
                cols.append(jnp.where(sel, 0.0, NEG_INF))
                taken = taken + jnp.sum(eqf, axis=1, keepdims=True)
            bias_s[c] = jnp.concatenate(cols, axis=1)
            return taken

        lax.fori_loop(0, n_chunks, mask_chunk, jnp.zeros((tq, 1), F32))

    _select()

    m_s[...] = jnp.full(m_s.shape, NEG_INF, F32)
    l_s[...] = jnp.zeros(l_s.shape, F32)
    acc_s[...] = jnp.zeros(acc_s.shape, F32)

    def attend(c, carry):
        start = pl.multiple_of(c * ck, ck)
        bias = bias_s[c]
        for h in range(n_heads):
            cols = slice(h * dh, (h + 1) * dh)
            s = _dot_nt(qc_ref[:, cols], kc_ref[pl.ds(start, ck), cols]) * scale + bias
            m_old = m_s[h]
            m_new = jnp.maximum(m_old, jnp.max(s, axis=1, keepdims=True))
            m_safe = jnp.where(m_new == NEG_INF, 0.0, m_new)
            alpha = jnp.exp(m_old - m_safe)
            p = jnp.exp(s - m_safe)
            l_s[h] = alpha * l_s[h] + jnp.sum(p, axis=1, keepdims=True)
            acc_s[h] = alpha * acc_s[h] + _dot(p, vc_ref[pl.ds(start, ck), cols])
            m_s[h] = m_new
        return carry

    lax.fori_loop(0, n_chunks, attend, 0)
    for h in range(n_heads):
        o_ref[:, h * dh:(h + 1) * dh] = (acc_s[h] / l_s[h]).astype(o_ref.dtype)


def _dsa_attn(qc, kc, vc, qi, ki, wi, nh, n_idx_heads):
    bsz, t, w = qc.shape
    dh = w // nh
    d_i = ki.shape[2]
    tq = min(t, 128)
    ck = min(t, 512)
    topk = min(TOPK_MAX, t // 4)
    rows = lambda wdt: pl.BlockSpec((None, tq, wdt), lambda b, i: (b, i, 0))
    whole = lambda wdt: pl.BlockSpec((None, t, wdt), lambda b, i: (b, 0, 0))
    return pl.pallas_call(
        functools.partial(_dsa_kernel, topk=topk, n_heads=nh, n_idx_heads=n_idx_heads, scale=dh ** -0.5,
                          wi_scale=(n_idx_heads * d_i) ** -0.5),
        grid=(bsz, t // tq),
        in_specs=[rows(w), whole(w), whole(w), rows(qi.shape[2]), whole(d_i), rows(wi.shape[2])],
        out_specs=rows(w),
        out_shape=jax.ShapeDtypeStruct((bsz, t, w), MXU_DTYPE),
        scratch_shapes=[pltpu.VMEM((t // ck, tq, ck), I32), pltpu.VMEM((t // ck, tq, ck), F32),
                        pltpu.VMEM((nh, tq, 1), F32), pltpu.VMEM((nh, tq, 1), F32), pltpu.VMEM((nh, tq, dh), F32)],
        compiler_params=_params(("parallel", "arbitrary")),
        name="dsa_attn",
    )(qc, kc, vc, qi, ki, wi)


def _page_copy(cache_hbm, layer, phys, buf, slot, i, sem):
    return pltpu.make_async_copy(cache_hbm.at[layer, phys], buf.at[slot, i], sem.at[slot])


def _paged_step(cache_hbm, layer, pt_ref, buf, sem):
    g = buf.shape[1]
    per_b = pl.num_programs(1)
    step = pl.program_id(0) * per_b + pl.program_id(1)

    def fetch(t):
        b = t // per_b
        s = t % per_b
        for i in range(g):
            _page_copy(cache_hbm, layer, pt_ref[b, s * g + i], buf, t % 2, i, sem).start()

    @pl.when(step == 0)
    def _():
        fetch(step)

    @pl.when(step + 1 < pl.num_programs(0) * per_b)
    def _():
        fetch(step + 1)

    slot = step % 2
    for i in range(g):
        _page_copy(cache_hbm, layer, 0, buf, slot, i, sem).wait()
    return slot


def _paged_call(body, name, page_table, cache, g, in_arrays, in_specs, out_spec, out_shape):
    bd, n_pages = page_table.shape
    assert n_pages % g == 0
    return pl.pallas_call(
        body,
        grid_spec=pltpu.PrefetchScalarGridSpec(
            num_scalar_prefetch=1, grid=(bd, n_pages // g),
            in_specs=in_specs + [pl.BlockSpec(memory_space=pl.ANY)],
            out_specs=out_spec,
            scratch_shapes=[pltpu.VMEM((2, g) + cache.shape[2:], cache.dtype), pltpu.SemaphoreType.DMA((2,))]),
        out_shape=out_shape,
        compiler_params=_params(("arbitrary", "arbitrary")),
        name=name,
    )(page_table, *in_arrays, cache)


def _pages_per_step(n_pages, want):
    g = min(want, n_pages)
    while n_pages % g:
        g -= 1
    return g


def _key_rows(x, nh):
    bd = x.shape[0]
    return x.reshape(bd, nh, 2, LANES).transpose(0, 2, 1, 3).reshape(bd, 2 * nh, LANES)


def _diff_qk_kernel(pt_ref, q_ref, k_hbm, o_ref, buf, sem, *, layer):
    slot = _paged_step(k_hbm, layer, pt_ref, buf, sem)
    _, g, rows, _ = buf.shape
    for i in range(g):
        o_ref[:, i * rows:(i + 1) * rows] = _dot_nt(q_ref[...], buf[slot, i])


def _diff_pv_kernel(pt_ref, a_ref, v_hbm, o_ref, buf, sem, *, layer):
    slot = _paged_step(v_hbm, layer, pt_ref, buf, sem)
    _, g, rows, _ = buf.shape

    @pl.when(pl.program_id(1) == 0)
    def _():
        o_ref[...] = jnp.zeros(o_ref.shape, F32)

    acc = jnp.zeros(o_ref.shape, F32)
    for i in range(g):
        acc = acc + _dot(a_ref[:, i * rows:(i + 1) * rows], buf[slot, i])
    o_ref[...] += acc


def _diff_softmax_kernel(s_ref, q_ref, kn_ref, lq1_ref, lk1_ref, lq2_ref, lk2_ref, a_ref, an_ref,
                         *, lam_init, scale, nh):
    n_rows, width = s_ref.shape
    ch = min(width, 64 * LANES)
    real = (_iota((n_rows, ch), 1) % n_rows) == _iota((n_rows, ch), 0)
    chunks = [slice(i * ch, (i + 1) * ch) for i in range(width // ch)]

    def scores(sl):
        return jnp.where(real, s_ref[:, sl] * scale, NEG_INF)

    s_new = jnp.sum(_rnd(q_ref[...]) * _rnd(kn_ref[...]), axis=1, keepdims=True) * scale
    m = s_new
    for sl in chunks:
        m = jnp.maximum(m, jnp.max(scores(sl), axis=1, keepdims=True))
    e_new = jnp.exp(s_new - m)
    z = e_new
    for sl in chunks:
        z = z + jnp.sum(jnp.exp(scores(sl) - m), axis=1, keepdims=True)
    lam = _lambda(lq1_ref[...], lk1_ref[...], lq2_ref[...], lk2_ref[...], lam_init)
    p_new = e_new / z
    for sl in chunks:
        p = jnp.exp(scores(sl) - m) / z
        p0, p1 = p[:nh], p[nh:]
        a_ref[:, sl] = jnp.concatenate([p0 - lam * pltpu.roll(p1, ch - nh, 1),
                                        pltpu.roll(p0, nh, 1) - lam * p1], axis=0)
    a_new = p_new[:nh] - lam * p_new[nh:]
    an_ref[...] = jnp.broadcast_to(jnp.concatenate([a_new, a_new], axis=0), an_ref.shape)


def _diff_final_kernel(acc_ref, an_ref, vn_ref, g_ref, o_ref, *, lam_init):
    n_rows, wdt = acc_ref.shape
    nh = n_rows // 2
    full = acc_ref[...] + _rnd(an_ref[:, 0:1]) * _rnd(vn_ref[...])
    sq = jnp.sum(full * full, axis=1, keepdims=True)
    ms = (sq + jnp.concatenate([sq[nh:], sq[:nh]], axis=0)) / (2.0 * wdt)
    o_ref[...] = (full * lax.rsqrt(ms + EPS) * g_ref[...] * (1.0 - lam_init)).astype(o_ref.dtype)


def _diff_decode(qb, kb_new, vb_new, cache_k, cache_v, layer, page_table, lams, g_sub, lam_init, nh):
    bd, w = qb.shape
    dv = w // nh
    dh = dv // 2
    assert dh == LANES and 2 * nh == SUBLANES
    n_layers, n_pool, page = cache_k.shape[:3]
    n_pages = page_table.shape[1]
    rows = page * 2 * nh
    width = n_pages * rows
    g = _pages_per_step(n_pages, 8)

    def key_row_view(cache):
        c6 = cache.reshape(n_layers, n_pool, page, nh, 2, LANES)
        return c6.transpose(0, 1, 2, 4, 3, 5).reshape(n_layers, n_pool, rows, LANES)

    q8, k8, v8 = _key_rows(qb, nh), _key_rows(kb_new, nh), _key_rows(vb_new, nh)
    whole = lambda wdt: pl.BlockSpec((None, SUBLANES, wdt), lambda b, p, pt: (b, 0, 0))
    chunk = pl.BlockSpec((None, SUBLANES, g * rows), lambda b, p, pt: (b, 0, p))
    s = _paged_call(
        functools.partial(_diff_qk_kernel, layer=layer), "diff_decode_qk", page_table, key_row_view(cache_k), g,
        [q8], [whole(LANES)], chunk, jax.ShapeDtypeStruct((bd, SUBLANES, width), F32))
    per_b = lambda shp: pl.BlockSpec((None,) + shp, lambda b: (b, 0, 0))
    vec = pl.BlockSpec((1, dh), lambda b: (0, 0))
    a, a_new = pl.pallas_call(
        functools.partial(_diff_softmax_kernel, lam_init=lam_init, scale=dh ** -0.5, nh=nh),
        grid=(bd,),
        in_specs=[per_b((SUBLANES, width)), per_b((SUBLANES, LANES)), per_b((SUBLANES, LANES)), vec, vec, vec, vec],
        out_specs=[per_b((SUBLANES, width)), per_b((SUBLANES, LANES))],
        out_shape=[jax.ShapeDtypeStruct((bd, SUBLANES, width), F32), jax.ShapeDtypeStruct((bd, SUBLANES, LANES), F32)],
        compiler_params=_params(("parallel",)),
        name="diff_decode_softmax",
    )(s, q8, k8, *[x.reshape(1, dh) for x in lams])
    acc = _paged_call(
        functools.partial(_diff_pv_kernel, layer=layer), "diff_decode_pv", page_table, key_row_view(cache_v), g,
        [a], [chunk], whole(LANES), jax.ShapeDtypeStruct((bd, SUBLANES, LANES), F32))
    g8 = jnp.repeat(g_sub.reshape(2, 1, LANES), nh, axis=1).reshape(SUBLANES, LANES)
    out8 = pl.pallas_call(
        functools.partial(_diff_final_kernel, lam_init=lam_init),
        grid=(bd,),
        in_specs=[per_b((SUBLANES, LANES))] * 3 + [pl.BlockSpec((SUBLANES, LANES), lambda b: (0, 0))],
        out_specs=per_b((SUBLANES, LANES)),
        out_shape=jax.ShapeDtypeStruct((bd, SUBLANES, LANES), MXU_DTYPE),
        compiler_params=_params(("parallel",)),
        name="diff_decode_final",
    )(acc, a_new, v8, g8)
    return out8.reshape(bd, 2, nh, LANES).transpose(0, 2, 1, 3).reshape(bd, w)


def _idx_score_kernel(pt_ref, qi_ref, wi_ref, ki_hbm, o_ref, buf, sem, *, layer, wi_scale):
    slot = _paged_step(ki_hbm, layer, pt_ref, buf, sem)
    _, g, page, _ = buf.shape
    w = _rnd(wi_ref[...] * wi_scale)
    for i in range(g):
        sc = jnp.maximum(_dot_nt(qi_ref[...], buf[slot, i]), 0.0)
        o_ref[:, i * page:(i + 1) * page] = jnp.sum(w * _rnd(sc), axis=0, keepdims=True) + 0.0


def _select_kernel(sc_ref, qi_ref, wi_ref, kn_ref, idx_ref, nsel_ref, mnew_ref, pos_s, *, topk, wi_scale):
    npg, page = sc_ref.shape
    sc_new = jnp.maximum(jnp.sum(_rnd(qi_ref[...]) * _rnd(kn_ref[...]), axis=1, keepdims=True), 0.0)
    s_new = jnp.sum(_rnd(wi_ref[...] * wi_scale) * _rnd(sc_new), axis=0, keepdims=True) + 0.0
    keys = _f32_key(sc_ref[...])
    key_new = _f32_key(s_new)

    def total(x):
        return jnp.sum(jnp.sum(x, axis=1, keepdims=True), axis=0, keepdims=True)

    def count(pred):
        return total(jnp.where(pred(keys), 1.0, 0.0)) + jnp.where(pred(key_new), 1.0, 0.0)

    def bit_step(it, prefix):
        bit = jnp.left_shift(jnp.int32(1), 31 - it)
        cand = (prefix | bit) ^ I32_MIN
        return jnp.where(count(lambda kk: kk >= cand) >= float(topk), prefix | bit, prefix)

    thr = lax.fori_loop(0, 32, bit_step, jnp.zeros((1, 1), I32)) ^ I32_MIN
    need = float(topk) - count(lambda kk: kk > thr)
    strict_upper = _strict_upper(page)
    strict_lower = jnp.where(_iota((npg, npg), 1) < _iota((npg, npg), 0), 1.0, 0.0).astype(MXU_DTYPE)

    def count_before(flag):
        in_row = jnp.dot(_mx(flag), strict_upper, preferred_element_type=F32)
        row_tot = jnp.broadcast_to(jnp.sum(flag, axis=1, keepdims=True), (npg, page))
        return in_row + jnp.dot(strict_lower, _mx(row_tot), preferred_element_type=F32)

    eq = keys == thr
    eqf = jnp.where(eq, 1.0, 0.0)
    sel = ((keys > thr) | (eq & (count_before(eqf) < need))) & (keys > KEY_NEG_INF)
    sel_new = ((key_new > thr) | ((key_new == thr) & (total(eqf) < need))) & (key_new > KEY_NEG_INF)
    mnew_ref[...] = jnp.broadcast_to(jnp.where(sel_new, 1.0, 0.0), mnew_ref.shape)
    self = jnp.where(sel, 1.0, 0.0)
    pos_s[...] = jnp.where(sel, count_before(self), -1.0)
    n_slots = idx_ref.shape[0]
    slot_id = _iota((n_slots, page), 0).astype(F32)
    lane = _iota((n_slots, page), 1).astype(F32)

    def gather_page(p, acc):
        hit = pos_s[pl.ds(p, 1), :] == slot_id
        return acc + jnp.where(hit, jnp.asarray(p * page, F32) + lane, 0.0)

    acc = lax.fori_loop(0, npg, gather_page, jnp.zeros((n_slots, page), F32))
    idx_ref[...] = jnp.broadcast_to(jnp.sum(acc, axis=1, keepdims=True).astype(I32), idx_ref.shape)
    nsel_ref[...] = jnp.broadcast_to(total(self), nsel_ref.shape)


def _dsa_gather_kernel(idx_ref, pt_ref, q_ref, kn_ref, vn_ref, mnew_ref, nsel_ref, k_hbm, v_hbm, o_ref,
                       kbuf, vbuf, sem, *, layer, scale):
    b = pl.program_id(0)
    n_slots, nh, dh = kbuf.shape
    page = k_hbm.shape[2]

    def row(cache, i, buf, r, s):
        return pltpu.make_async_copy(cache.at[layer, pt_ref[b, i // page], pl.ds(i % page, 1)],
                                     buf.at[pl.ds(r, 1)], sem.at[s])

    def issue(r, carry):
        i = idx_ref[b, r]
        row(k_hbm, i, kbuf, r, 0).start()
        row(v_hbm, i, vbuf, r, 1).start()
        return carry

    def drain(r, carry):
        row(k_hbm, 0, kbuf, 0, 0).wait()
        row(v_hbm, 0, vbuf, 0, 1).wait()
        return carry

    lax.fori_loop(0, n_slots, issue, 0)
    lax.fori_loop(0, n_slots, drain, 0)
    q = q_ref[...]
    head = _iota((nh, n_slots), 0)
    s = jnp.zeros((nh, n_slots), F32)
    for h in range(nh):
        s = jnp.where(head == h, _dot_nt(q, kbuf[:, h, :]), s)
    live = _iota((nh, n_slots), 1).astype(F32) < nsel_ref[:, 0:1]
    s = jnp.where(live, s * scale, NEG_INF)
    s_new = jnp.sum(_rnd(q) * _rnd(kn_ref[...]), axis=1, keepdims=True) * scale
    s_new = jnp.where(mnew_ref[:, 0:1] > 0.0, s_new, NEG_INF)
    m = jnp.maximum(jnp.max(s, axis=1, keepdims=True), s_new)
    e = jnp.exp(s - m)
    e_new = jnp.exp(s_new - m)
    z = jnp.sum(e, axis=1, keepdims=True) + e_new
    p = e / z
    o = jnp.zeros((nh, dh), F32)
    head = _iota((nh, dh), 0)
    for h in range(nh):
        o = jnp.where(head == h, _dot(p, vbuf[:, h, :]), o)
    o = o + _rnd(e_new / z) * _rnd(vn_ref[...])
    o_ref[...] = o.astype(o_ref.dtype)


def _dsa_decode(qc, kc_new, vc_new, qi, ki_new, wi, cache_k, cache_v, cache_i, layer, page_table, nh, n_idx_heads):
    bd, w = qc.shape
    dh = w // nh
    d_i = ki_new.shape[1]
    page = cache_k.shape[2]
    n_pages = page_table.shape[1]
    past = n_pages * page
    topk = min(TOPK_MAX, (past + 1) // 4)
    n_slots = -(-topk // SUBLANES) * SUBLANES
    qi3 = qi.reshape(bd, n_idx_heads, d_i)
    wi3 = wi[:, :n_idx_heads].reshape(bd, n_idx_heads, 1)
    wi_scale = (n_idx_heads * d_i) ** -0.5
    g = _pages_per_step(n_pages, 16)
    scores = _paged_call(
        functools.partial(_idx_score_kernel, layer=layer, wi_scale=wi_scale), "idx_scores", page_table, cache_i, g,
        [qi3, wi3],
        [pl.BlockSpec((None, n_idx_heads, d_i), lambda b, p, pt: (b, 0, 0)),
         pl.BlockSpec((None, n_idx_heads, 1), lambda b, p, pt: (b, 0, 0))],
        pl.BlockSpec((None, 1, g * page), lambda b, p, pt: (b, 0, p)),
        jax.ShapeDtypeStruct((bd, 1, past), F32))
    per_b = lambda shp: pl.BlockSpec((None,) + shp, lambda b: (b, 0, 0))
    sel_idx, n_sel, mask_new = pl.pallas_call(
        functools.partial(_select_kernel, topk=topk, wi_scale=wi_scale),
        grid=(bd,),
        in_specs=[per_b((n_pages, page)), per_b((n_idx_heads, d_i)), per_b((n_idx_heads, 1)), per_b((1, d_i))],
        out_specs=[per_b((n_slots, LANES)), per_b((1, LANES)), per_b((1, LANES))],
        out_shape=[jax.ShapeDtypeStruct((bd, n_slots, LANES), I32), jax.ShapeDtypeStruct((bd, 1, LANES), F32),
                   jax.ShapeDtypeStruct((bd, 1, LANES), F32)],
        scratch_shapes=[pltpu.VMEM((n_pages, page), F32)],
        compiler_params=_params(("parallel",)),
        name="idx_select",
    )(scores.reshape(bd, n_pages, page), qi3, wi3, ki_new.reshape(bd, 1, d_i))
    per_b2 = lambda shp: pl.BlockSpec((None,) + shp, lambda b, idx, pt: (b, 0, 0))
    hbm = pl.BlockSpec(memory_space=pl.ANY)
    return pl.pallas_call(
        functools.partial(_dsa_gather_kernel, layer=layer, scale=dh ** -0.5),
        grid_spec=pltpu.PrefetchScalarGridSpec(
            num_scalar_prefetch=2, grid=(bd,),
            in_specs=[per_b2((nh, dh)), per_b2((nh, dh)), per_b2((nh, dh)), per_b2((1, LANES)), per_b2((1, LANES)),
                      hbm, hbm],
            out_specs=per_b2((nh, dh)),
            scratch_shapes=[pltpu.VMEM((n_slots, nh, dh), F32), pltpu.VMEM((n_slots, nh, dh), F32),
                            pltpu.SemaphoreType.DMA((2,))]),
        out_shape=jax.ShapeDtypeStruct((bd, nh, dh), MXU_DTYPE),
        compiler_params=_params(("arbitrary",)),
        name="dsa_decode_attn",
    )(sel_idx[:, :, 0], page_table, qc.reshape(bd, nh, dh), kc_new.reshape(bd, nh, dh), vc_new.reshape(bd, nh, dh),
      mask_new, n_sel, cache_k, cache_v).reshape(bd, w)


def _route_kernel(lg_ref, br_ref, e_ref, g_ref, r_ref, cnt_ref, carry_s, *, n_valid, n_groups):
    i = pl.program_id(0)
    ne, tn = lg_ref.shape
    per = ne // n_groups

    @pl.when(i == 0)
    def _():
        carry_s[...] = jnp.zeros(carry_s.shape, F32)

    aff = _sigmoid(lg_ref[...])
    sel = aff + br_ref[...]
    sub = _iota((per, tn), 0).astype(F32)
    best = None
    for gi in range(n_groups):
        s = sel[gi * per:(gi + 1) * per]
        m1 = jnp.max(s, axis=0, keepdims=True)
        i1 = jnp.min(jnp.where(s == m1, sub, float(per)), axis=0, keepdims=True)
        s2 = jnp.where(sub == i1, NEG_INF, s)
        m2 = jnp.max(s2, axis=0, keepdims=True)
        i2 = jnp.min(jnp.where(s2 == m2, sub, float(per)), axis=0, keepdims=True)
        cand = (m1 + m2, float(gi * per) + i1, float(gi * per) + i2)
        if best is None:
            best = cand
        else:
            better = cand[0] > best[0]
            best = tuple(jnp.where(better, c, b) for c, b in zip(cand, best))
    e0, e1 = best[1].astype(I32), best[2].astype(I32)
    eid = _iota((ne, tn), 0)
    valid = (i * tn + _iota((1, tn), 1)) < n_valid
    oh0 = (eid == e0) & valid
    oh1 = (eid == e1) & valid
    a0 = jnp.sum(jnp.where(eid == e0, aff, 0.0), axis=0, keepdims=True)
    a1 = jnp.sum(jnp.where(eid == e1, aff, 0.0), axis=0, keepdims=True)
    tot = a0 + a1
    e_ref[...] = jnp.concatenate([e0, e1], axis=0)
    g_ref[...] = jnp.concatenate([a0 / tot, a1 / tot], axis=0)
    oh = jnp.where(oh0, 1.0, 0.0) + jnp.where(oh1, 1.0, 0.0)
    before = carry_s[:, 0:1] + jnp.dot(_mx(oh), _strict_upper(tn), preferred_element_type=F32)
    r0 = jnp.sum(jnp.where(oh0, before, 0.0), axis=0, keepdims=True)
    r1 = jnp.sum(jnp.where(oh1, before, 0.0), axis=0, keepdims=True)
    r_ref[...] = jnp.concatenate([r0, r1], axis=0).astype(I32)
    carry_s[...] = carry_s[...] + jnp.sum(oh, axis=1, keepdims=True)
    cnt_ref[...] = carry_s[...]


def _route(logits_t, b_router, n_valid):
    ne, mp = logits_t.shape
    tn = min(mp, 256)
    tok = lambda i: (0, i)
    return pl.pallas_call(
        functools.partial(_route_kernel, n_valid=n_valid, n_groups=N_GROUPS),
        grid=(mp // tn,),
        in_specs=[pl.BlockSpec((ne, tn), tok), pl.BlockSpec((ne, 1), lambda i: (0, 0))],
        out_specs=[pl.BlockSpec((TOP_K, tn), tok), pl.BlockSpec((TOP_K, tn), tok), pl.BlockSpec((TOP_K, tn), tok),
                   pl.BlockSpec((ne, LANES), lambda i: (0, 0))],
        out_shape=[jax.ShapeDtypeStruct((TOP_K, mp), I32), jax.ShapeDtypeStruct((TOP_K, mp), F32),
                   jax.ShapeDtypeStruct((TOP_K, mp), I32), jax.ShapeDtypeStruct((ne, LANES), F32)],
        scratch_shapes=[pltpu.VMEM((ne, LANES), F32)],
        compiler_params=_params(("arbitrary",)),
        name="moe_route",
    )(logits_t, b_router.reshape(ne, 1))


def _row_copy(src, s_row, dst, d_row, sem):
    return pltpu.make_async_copy(src.at[pl.ds(s_row, 1)], dst.at[pl.ds(d_row, 1)], sem)


def _expert_kernel(be_ref, nu_ref, src_ref, x_hbm, wg_ref, wu_ref, wd_ref, o_ref, xbuf, sem):
    b = pl.program_id(0)
    n_used = nu_ref[0]
    blk = xbuf.shape[1]

    def fetch(block):
        def issue(r, carry):
            _row_copy(x_hbm, src_ref[block * blk + r], xbuf.at[block % 2], r, sem.at[block % 2]).start()
            return carry
        lax.fori_loop(0, blk, issue, 0, unroll=8)

    @pl.when(b == 0)
    def _():
        fetch(b)

    @pl.when(b + 1 < n_used)
    def _():
        fetch(b + 1)

    @pl.when(b < n_used)
    def _():
        def drain(r, carry):
            _row_copy(x_hbm, 0, xbuf.at[b % 2], 0, sem.at[b % 2]).wait()
            return carry
        lax.fori_loop(0, blk, drain, 0, unroll=8)
        x = _mx(xbuf[b % 2])
        gate = jnp.dot(x, _mx(wg_ref[...]), preferred_element_type=F32)
        up = jnp.dot(x, _mx(wu_ref[...]), preferred_element_type=F32)
        hdn = gate * _sigmoid(gate) * up
        y = jnp.dot(_mx(hdn), _mx(wd_ref[...]), preferred_element_type=F32)
        o_ref[...] = _rnd(y)

    @pl.when(b >= n_used)
    def _():
        o_ref[...] = jnp.zeros(o_ref.shape, F32)


def _experts(blk_e, n_used, src, x, w_gate, w_up, w_down, layer, blk):
    n_rows = src.shape[0]
    d = x.shape[1]
    de = w_gate.shape[3]
    wspec = lambda shp: pl.BlockSpec((None, None) + shp, lambda b, be, nu, sr: (layer, be[b], 0, 0))
    return pl.pallas_call(
        _expert_kernel,
        grid_spec=pltpu.PrefetchScalarGridSpec(
            num_scalar_prefetch=3, grid=(n_rows // blk,),
            in_specs=[pl.BlockSpec(memory_space=pl.ANY), wspec((d, de)), wspec((d, de)), wspec((de, d))],
            out_specs=pl.BlockSpec((blk, d), lambda b, be, nu, sr: (b, 0)),
            scratch_shapes=[pltpu.VMEM((2, blk, d), F32), pltpu.SemaphoreType.DMA((2,))]),
        out_shape=jax.ShapeDtypeStruct((n_rows, d), F32),
        compiler_params=_params(("arbitrary",)),
        name="moe_experts",
    )(blk_e, n_used, src, x, w_gate, w_up, w_down)


def _combine_kernel(dest_ref, y_hbm, x_ref, gt_ref, g_ref, b_ref, xo_ref, xb_ref, rows_s, sem, *, n_tok, alpha):
    tc = x_ref.shape[0]
    base = pl.program_id(0) * tc

    def issue(r, carry):
        for kk in range(TOP_K):
            _row_copy(y_hbm, dest_ref[kk * n_tok + base + r], rows_s.at[kk], r, sem).start()
        return carry

    def drain(r, carry):
        for kk in range(TOP_K):
            _row_copy(y_hbm, 0, rows_s.at[kk], 0, sem).wait()
        return carry

    lax.fori_loop(0, tc, issue, 0, unroll=8)
    lax.fori_loop(0, tc, drain, 0, unroll=8)
    gt = gt_ref[...]
    y = _rnd(gt[:, 0:1]) * rows_s[0] + _rnd(gt[:, 1:2]) * rows_s[1]
    out = _layer_norm(alpha * x_ref[...] + y, g_ref[...], b_ref[...])
    xo_ref[...] = out
    xb_ref[...] = out.astype(xb_ref.dtype)


def _combine(dest_flat, y, x, gates, g, bt, alpha):
    n_tok, d = x.shape
    tc = min(n_tok, 128)
    row = lambda i, dst: (i, 0)
    fixed = lambda i, dst: (0, 0)
    return pl.pallas_call(
        functools.partial(_combine_kernel, n_tok=n_tok, alpha=alpha),
        grid_spec=pltpu.PrefetchScalarGridSpec(
            num_scalar_prefetch=1, grid=(n_tok // tc,),
            in_specs=[pl.BlockSpec(memory_space=pl.ANY), pl.BlockSpec((tc, d), row), pl.BlockSpec((tc, TOP_K), row),
                      pl.BlockSpec((1, d), fixed), pl.BlockSpec((1, d), fixed)],
            out_specs=[pl.BlockSpec((tc, d), row), pl.BlockSpec((tc, d), row)],
            scratch_shapes=[pltpu.VMEM((TOP_K, tc, d), F32), pltpu.SemaphoreType.DMA(())]),
        out_shape=[jax.ShapeDtypeStruct((n_tok, d), F32), jax.ShapeDtypeStruct((n_tok, d), MXU_DTYPE)],
        compiler_params=_params(("arbitrary",)),
        name="moe_combine",
    )(dest_flat, y, x, gates, g.reshape(1, d), bt.reshape(1, d))


def _moe_ln(x, xb, w_router_t, b_router, w_gate, w_up, w_down, layer, g, bt, alpha, n_valid, blk):
    m, d = x.shape
    ne = w_gate.shape[1]
    mp = max(m, LANES)
    logits_t = _mm_nt(w_router_t, xb)
    if mp != m:
        logits_t = jnp.pad(logits_t, ((0, 0), (0, mp - m)))
    eidx, gates, rank, counts = _route(logits_t, b_router, n_valid)
    counts = counts[:, 0].astype(I32)
    padded = (counts + blk - 1) // blk * blk
    pad_end = jnp.cumsum(padded)
    pad_start = pad_end - padded
    n_blocks = -(-(n_valid * TOP_K + ne * (blk - 1)) // blk)
    first_slot = jnp.sum(jnp.where(eidx[:, :m, None] == jnp.arange(ne), pad_start, 0), axis=-1)
    dest = (first_slot + rank[:, :m]).astype(I32)
    n_rows = n_blocks * blk
    live = jnp.arange(m)[None, :] < n_valid
    tok = jnp.broadcast_to(jnp.arange(m, dtype=I32)[None, :], (TOP_K, m))
    src = jnp.zeros((n_rows,), I32).at[jnp.where(live, dest, n_rows).reshape(-1)].set(tok.reshape(-1), mode="drop")
    dest_flat = jnp.where(live, dest, 0).reshape(-1)
    blk_e = jnp.minimum(jnp.searchsorted(pad_end, jnp.arange(n_blocks) * blk, side="right"), ne - 1).astype(I32)
    n_used = (pad_end[-1:] // blk).astype(I32)
    y = _experts(blk_e, n_used, src, x, w_gate, w_up, w_down, layer, blk)
    return _combine(dest_flat, y, x, gates[:, :m].T, g, bt, alpha)


def _rope_tables(pos, half):
    inv = ROPE_THETA ** (-jnp.arange(half, dtype=F32) / half)
    ang = pos.astype(F32)[:, None] * inv[None, :]
    c, s = jnp.cos(ang), jnp.sin(ang)
    reps = LANES // (2 * half)
    return (jnp.tile(jnp.concatenate([c, c], -1), (1, reps)), jnp.tile(jnp.concatenate([-s, s], -1), (1, reps)), half)


def _cols(w, start, width, pad_to=None):
    out = w[:, start:start + width].astype(MXU_DTYPE)
    if pad_to is not None and pad_to > width:
        out = jnp.pad(out, ((0, 0), (0, pad_to - width)))
    return out


def kernel(x_prompt, x_sample, state_mlstm_c, state_mlstm_n, state_mlstm_m, cache_diff_k, cache_diff_v, cache_dsa_k, cache_dsa_v, cache_idx_k, state_ret, page_table, w_in_even, w_out_even, b_igate, b_fgate, g_mlstm, lam_q1, lam_k1, lam_q2, lam_k2, g_subln, w_in_odd, w_out_odd, g_ret, ln_mix_g, ln_mix_b, ln_ffn_g, ln_ffn_b, w_router, b_router, w_gate, w_up, w_down):
    bsz, seq, d_model = x_prompt.shape
    dec_b, dec_seq, _ = x_sample.shape
    assert dec_seq == 1
    depth = w_gate.shape[0]
    alpha = (2 * depth) ** 0.25
    h_a, dh_a = g_mlstm.shape[1:]
    h_b, dv_b = cache_diff_k.shape[3:]
    h_c, dh_c = cache_dsa_k.shape[3:]
    d_i = cache_idx_k.shape[3]
    h_d, dk_d, dv_d = state_ret.shape[2:]
    w_a, w_b, w_c, w_d = h_a * dh_a, h_b * dv_b, h_c * dh_c, h_d * dv_d
    h_i = w_in_odd.shape[2] - (3 * w_c + d_i + 2 * h_d * dk_d + 2 * w_d)
    h_i = h_i // (d_i + 1)
    past_len = page_table.shape[1] * cache_diff_k.shape[2]
    m_p = bsz * seq
    m_s = 2 * SUBLANES

    pos_p = jnp.arange(seq)
    pos_s = jnp.full((m_s,), past_len)
    rope_p = {h: _rope_tables(pos_p, h) for h in (dh_c // 2, d_i // 2)}
    rope_s = {h: _rope_tables(pos_s, h) for h in (dh_c // 2, d_i // 2)}
    log_gamma = jnp.log(1.0 - 2.0 ** (-5.0 - jnp.arange(h_d, dtype=F32)))
    w_router_t = w_router.T.astype(MXU_DTYPE)
    w_gate, w_up, w_down = (a.astype(MXU_DTYPE) for a in (w_gate, w_up, w_down))

    xp = x_prompt.reshape(m_p, d_model)
    xs = jnp.pad(x_sample.reshape(dec_b, d_model), ((0, m_s - dec_b), (0, 0)))
    xp_b, xs_b = xp.astype(MXU_DTYPE), xs.astype(MXU_DTYPE)
    names = ("mlstm_c", "mlstm_n", "mlstm_m", "diff_k", "diff_v", "dsa_k", "dsa_v", "idx_k", "ret")
    new_p = {k: [] for k in names}
    new_s = {k: [] for k in names}

    def pad_tokens(a):
        return jnp.pad(a[:dec_b, None, :], ((0, 0), (0, CHUNK - 1), (0, 0)))

    for l in range(depth):
        j = l // 2
        if l % 2 == 0:
            w = w_in_even[j]
            o_g = 4 * w_a
            o_b = o_g + 2 * h_a
            w_main = _cols(w, 0, 4 * w_a)
            w_gates_t = w[:, o_g:o_b].T.astype(MXU_DTYPE)
            w_qb, w_kb, w_vb = (_cols(w, o_b + i * w_b, w_b) for i in range(3))
            lams = (lam_q1[j], lam_k1[j], lam_q2[j], lam_k2[j])
            lam_init = 0.8 - 0.6 * math.exp(-0.3 * l)
            half = dv_b // 4
            z = _mm(xp_b, w_main).reshape(bsz, seq, 4 * w_a)
            gt = _mm_nt(w_gates_t, xp_b).reshape(2 * h_a, bsz, seq).transpose(1, 0, 2).reshape(bsz, 2 * h_a, 1, seq)
            r3 = lambda a: a.reshape(bsz, seq, a.shape[1])
            qb_m = _mm(xp_b, w_qb, rope=rope_p[half], out_dtype=MXU_DTYPE)
            kb, kb_m = _mm(xp_b, w_kb, rope=rope_p[half], mxu_copy=True)
            vb, vb_m = _mm(xp_b, w_vb, mxu_copy=True)
            zero = lambda *s: jnp.zeros(s, F32)
            h_mix, c_p, n_p, mm_p = _mlstm(z, gt, b_igate[j], b_fgate[j], g_mlstm[j], zero(bsz, h_a, dh_a, dh_a),
                                           zero(bsz, h_a, dh_a), zero(bsz, h_a), CHUNK)
            ob = _diff_attn(r3(qb_m), r3(kb_m), r3(vb_m), lams, g_subln[j], lam_init, h_b)
            mix_a_p, mix_b_p = h_mix.reshape(m_p, w_a), ob.reshape(m_p, w_b)
            st_p = (c_p, n_p.reshape(bsz, h_a, dh_a), mm_p.reshape(bsz, h_a),
                    kb.reshape(bsz, seq, h_b, dv_b), vb.reshape(bsz, seq, h_b, dv_b))
            zs = pad_tokens(_mm(xs_b, w_main))
            gts = _mm_nt(w_gates_t, xs_b)[:, :dec_b].T
            gts = jnp.pad(gts[:, :, None, None], ((0, 0), (0, 0), (0, 0), (0, CHUNK - 1)))
            qbs = _mm(xs_b, w_qb, rope=rope_s[half])[:dec_b]
            kbs = _mm(xs_b, w_kb, rope=rope_s[half])[:dec_b]
            vbs = _mm(xs_b, w_vb)[:dec_b]
            hs_mix, c_s, n_s, mm_s = _mlstm(zs, gts, b_igate[j], b_fgate[j], g_mlstm[j], state_mlstm_c[j],
                                            state_mlstm_n[j], state_mlstm_m[j], 1)
            obs = _diff_decode(qbs, kbs, vbs, cache_diff_k, cache_diff_v, j, page_table, lams, g_subln[j], lam_init, h_b)
            pad_rows = lambda a: jnp.pad(a, ((0, m_s - dec_b), (0, 0)))
            mix_a_s, mix_b_s = pad_rows(hs_mix[:, 0, :]), pad_rows(obs)
            st_s = (c_s, n_s.reshape(dec_b, h_a, dh_a), mm_s.reshape(dec_b, h_a),
                    kbs.reshape(dec_b, 1, h_b, dv_b), vbs.reshape(dec_b, 1, h_b, dv_b))
            w_out = w_out_even[j].astype(MXU_DTYPE)
            keys = names[:5]
        else:
            w = w_in_odd[j]
            o_qi = 3 * w_c
            o_ki = o_qi + h_i * d_i
            o_wi = o_ki + d_i
            o_qd = o_wi + h_i
            o_vd = o_qd + 2 * h_d * dk_d
            w_qc, w_kc, w_vc = (_cols(w, i * w_c, w_c) for i in range(3))
            w_qi = _cols(w, o_qi, h_i * d_i)
            w_ki = _cols(w, o_ki, d_i, pad_to=LANES)
            w_wi = _cols(w, o_wi, h_i, pad_to=LANES)
            w_qkd = _cols(w, o_qd, 2 * h_d * dk_d)
            w_vgd = _cols(w, o_vd, 2 * w_d)
            hc, hi, hd = dh_c // 2, d_i // 2, dk_d // 2

            def project(xb_, rope, q_dtype):
                qc = _mm(xb_, w_qc, rope=rope[hc], out_dtype=q_dtype)
                kc, kc_m = _mm(xb_, w_kc, rope=rope[hc], mxu_copy=True)
                vc, vc_m = _mm(xb_, w_vc, mxu_copy=True)
                qi = _mm(xb_, w_qi, rope=rope[hi], out_dtype=q_dtype)
                ki = _mm(xb_, w_ki, rope=rope[hi], n_out=d_i)
                wi = _mm(xb_, w_wi)
                qkd = _mm(xb_, w_qkd, rope=rope[hd])
                vgd = _mm(xb_, w_vgd)
                return (qc, kc, vc, qi, ki, wi, qkd, vgd), (kc_m, vc_m)

            (qc, kc, vc, qi, ki, wi, qkd, vgd), (kc_m, vc_m) = project(xp_b, rope_p, MXU_DTYPE)
            r3 = lambda a: a.reshape(bsz, seq, a.shape[1])
            oc = _dsa_attn(r3(qc), r3(kc_m), r3(vc_m), r3(qi), r3(ki), r3(wi), h_c, h_i)
            od, s_p = _retention(r3(qkd), r3(vgd), log_gamma, g_ret[j], jnp.zeros((bsz, h_d, dk_d, dv_d), F32), CHUNK)
            mix_a_p, mix_b_p = oc.reshape(m_p, w_c), od.reshape(m_p, w_d)
            st_p = (kc.reshape(bsz, seq, h_c, dh_c), vc.reshape(bsz, seq, h_c, dh_c), ki.reshape(bsz, seq, d_i), s_p)
            qc, kc, vc, qi, ki, wi, qkd, vgd = (a[:dec_b] for a in project(xs_b, rope_s, F32)[0])
            ocs = _dsa_decode(qc, kc, vc, qi, ki, wi, cache_dsa_k, cache_dsa_v, cache_idx_k, j, page_table, h_c, h_i)
            ods, s_s = _retention(pad_tokens(qkd), pad_tokens(vgd), log_gamma, g_ret[j], state_ret[j], 1)
            pad_rows = lambda a: jnp.pad(a, ((0, m_s - dec_b), (0, 0)))
            mix_a_s, mix_b_s = pad_rows(ocs), pad_rows(ods[:, 0, :])
            st_s = (kc.reshape(dec_b, 1, h_c, dh_c), vc.reshape(dec_b, 1, h_c, dh_c), ki.reshape(dec_b, 1, d_i), s_s)
            w_out = w_out_odd[j].astype(MXU_DTYPE)
            keys = names[5:]
        for k, a, b in zip(keys, st_p, st_s):
            new_p[k].append(a)
            new_s[k].append(b)
        xp, xp_b = _proj_ln(mix_a_p, mix_b_p, w_out, xp, ln_mix_g[l], ln_mix_b[l], alpha)
        xs, xs_b = _proj_ln(mix_a_s, mix_b_s, w_out, xs, ln_mix_g[l], ln_mix_b[l], alpha)
        moe_w = (w_router_t, b_router, w_gate, w_up, w_down, l, ln_ffn_g[l], ln_ffn_b[l], alpha)
        xp, xp_b = _moe_ln(xp, xp_b, *moe_w, m_p, 256)
        xs, xs_b = _moe_ln(xs, xs_b, *moe_w, dec_b, 16)

    out = [xp.reshape(bsz, seq, d_model), xs[:dec_b].reshape(dec_b, 1, d_model)]
    for k in names:
        out += [jnp.stack(new_p[k]), jnp.stack(new_s[k])]
    return tuple(out)
```

```python
import functools
import math

import jax
import jax.numpy as jnp
from jax import lax
from jax.experimental import pallas as pl
from jax.experimental.pallas import tpu as pltpu

F32 = jnp.float32
I32 = jnp.int32
MXU_DTYPE = jnp.bfloat16

N_GROUPS = 4
TOP_K = 2
TOPK_MAX = 256
CHUNK = 128
ROPE_THETA = 10000.0
EPS = 1e-5

LANES = 128
SUBLANES = 8
VMEM_LIMIT_BYTES = 52 * 1024 * 1024

NEG_INF = float("-inf")


def _mx(x):
    return x.astype(MXU_DTYPE)


def _rnd(x):
    return x.astype(MXU_DTYPE).astype(F32)


def _dot(a, b):
    return jnp.dot(_mx(a), _mx(b), preferred_element_type=F32)


def _dot_nt(a, b):
    return lax.dot_general(_mx(a), _mx(b), (((1,), (1,)), ((), ())), preferred_element_type=F32)


def _params(sem, vmem=VMEM_LIMIT_BYTES):
    return pltpu.CompilerParams(dimension_semantics=sem, vmem_limit_bytes=vmem)


def _iota(shape, axis):
    return lax.broadcasted_iota(I32, shape, axis)


def _sigmoid(x):
    return 1.0 / (1.0 + jnp.exp(-x))


def _lane_fold(x, op):
    acc = x[:, 0:LANES]
    for j in range(1, x.shape[1] // LANES):
        acc = op(acc, x[:, j * LANES:(j + 1) * LANES])
    return acc


def _row_max(x):
    return jnp.max(_lane_fold(x, jnp.maximum), axis=1, keepdims=True)


def _row_sum(x):
    return jnp.sum(_lane_fold(x, jnp.add), axis=1, keepdims=True)


def _row_to_col(row, n):
    eye = _iota((n, n), 0) == _iota((n, n), 1)
    return jnp.sum(jnp.where(eye, jnp.broadcast_to(row, (n, n)), 0.0), axis=1, keepdims=True)


def _head_norm(x, g):
    mu = jnp.mean(x, axis=-1, keepdims=True)
    xc = x - mu
    var = jnp.mean(xc * xc, axis=-1, keepdims=True)
    return xc * lax.rsqrt(var + EPS) * g


def _layer_norm(z, g, b):
    mu = jnp.mean(z, axis=-1, keepdims=True)
    zc = z - mu
    var = jnp.mean(zc * zc, axis=-1, keepdims=True)
    return zc * lax.rsqrt(var + EPS) * g + b


def _rope_apply(a, cos, sin, half):
    if 2 * half == LANES:
        r = pltpu.roll(a, half, 1)
    else:
        lane = _iota(a.shape, 1)
        r = jnp.where((lane % (2 * half)) < half, pltpu.roll(a, LANES - half, 1), pltpu.roll(a, half, 1))
    return a * cos + r * sin


def _mm_kernel(x_ref, w_ref, *rest, rope_half, n_outs):
    o_refs = rest[len(rest) - n_outs:]
    acc = jnp.dot(x_ref[...], w_ref[...], preferred_element_type=F32)
    if rope_half:
        cos = rest[0][...]
        sin = rest[1][...]
        n_out = o_refs[0].shape[1]
        for j in range(acc.shape[1] // LANES):
            res = _rope_apply(acc[:, j * LANES:(j + 1) * LANES], cos, sin, rope_half)
            nw = min(LANES, n_out - j * LANES)
            for o_ref in o_refs:
                o_ref[:, j * LANES:j * LANES + nw] = res[:, :nw].astype(o_ref.dtype)
    else:
        for o_ref in o_refs:
            o_ref[...] = acc.astype(o_ref.dtype)


def _mm(x, w, rope=None, n_out=None, out_dtype=F32, mxu_copy=False):
    m, k = x.shape
    n = w.shape[1]
    n_out = n if n_out is None else n_out
    tm = min(m, 1024) if rope is None else min(m, 1024, rope[0].shape[0])
    tn = min(n, 512)
    assert m % tm == 0 and n % tn == 0 and (n_out == n or n == tn)
    in_specs = [pl.BlockSpec((tm, k), lambda i, j: (i, 0)), pl.BlockSpec((k, tn), lambda i, j: (0, j))]
    args = [x, w]
    half = 0
    if rope is not None:
        cos, sin, half = rope
        nt = cos.shape[0] // tm
        assert cos.shape[0] % tm == 0
        in_specs += [pl.BlockSpec((tm, LANES), lambda i, j: (i % nt, 0))] * 2
        args += [cos, sin]
    out_spec = pl.BlockSpec((tm, min(tn, n_out)), lambda i, j: (i, j))
    dtypes = (out_dtype, MXU_DTYPE) if mxu_copy else (out_dtype,)
    outs = pl.pallas_call(
        functools.partial(_mm_kernel, rope_half=half, n_outs=len(dtypes)),
        grid=(m // tm, n // tn),
        in_specs=in_specs,
        out_specs=[out_spec] * len(dtypes),
        out_shape=[jax.ShapeDtypeStruct((m, n_out), dt) for dt in dtypes],
        compiler_params=_params(("parallel", "parallel")),
        name="proj",
    )(*args)
    return outs if mxu_copy else outs[0]


def _mm_nt_kernel(w_ref, x_ref, o_ref):
    o_ref[...] = lax.dot_general(w_ref[...], x_ref[...], (((1,), (1,)), ((), ())), preferred_element_type=F32)


def _mm_nt(w_t, x):
    n, k = w_t.shape
    m = x.shape[0]
    tm = min(m, 512)
    return pl.pallas_call(
        _mm_nt_kernel,
        grid=(m // tm,),
        in_specs=[pl.BlockSpec((n, k), lambda i: (0, 0)), pl.BlockSpec((tm, k), lambda i: (i, 0))],
        out_specs=pl.BlockSpec((n, tm), lambda i: (0, i)),
        out_shape=jax.ShapeDtypeStruct((n, m), F32),
        compiler_params=_params(("parallel",)),
        name="proj_t",
    )(w_t, x)


def _proj_ln_kernel(a_ref, b_ref, wa_ref, wb_ref, x_ref, g_ref, bt_ref, xo_ref, xb_ref, *, alpha):
    y = jnp.dot(a_ref[...], wa_ref[...], preferred_element_type=F32)
    y = y + jnp.dot(b_ref[...], wb_ref[...], preferred_element_type=F32)
    out = _layer_norm(alpha * x_ref[...] + y, g_ref[...], bt_ref[...])
    xo_ref[...] = out
    xb_ref[...] = out.astype(xb_ref.dtype)


def _proj_ln(a, b, w_out, x, g, bt, alpha):
    m, d = x.shape
    ka = a.shape[1]
    tm = min(m, 256)
    wa, wb = w_out[:ka], w_out[ka:]
    row = lambda i: (i, 0)
    fixed = lambda i: (0, 0)
    return pl.pallas_call(
        functools.partial(_proj_ln_kernel, alpha=alpha),
        grid=(m // tm,),
        in_specs=[pl.BlockSpec((tm, ka), row), pl.BlockSpec((tm, b.shape[1]), row),
                  pl.BlockSpec(wa.shape, fixed), pl.BlockSpec(wb.shape, fixed),
                  pl.BlockSpec((tm, d), row), pl.BlockSpec((1, d), fixed), pl.BlockSpec((1, d), fixed)],
        out_specs=[pl.BlockSpec((tm, d), row), pl.BlockSpec((tm, d), row)],
        out_shape=[jax.ShapeDtypeStruct((m, d), F32), jax.ShapeDtypeStruct((m, d), MXU_DTYPE)],
        compiler_params=_params(("parallel",)),
        name="out_proj_ln",
    )(a, b, wa, wb, x, g.reshape(1, d), bt.reshape(1, d))


def _lane_scan(x, op, fill):
    n = x.shape[1]
    lane = _iota(x.shape, 1)
    s = 1
    while s < n:
        x = op(x, jnp.where(lane >= s, pltpu.roll(x, s, 1), fill))
        s *= 2
    return x


def _mlstm_head(q, k, v, o_gate, ig, fpre, g, c_prev, n_prev, m_prev, valid_len):
    L = q.shape[0]
    lf = jnp.minimum(fpre, 0.0) - jnp.log1p(jnp.exp(-jnp.abs(fpre)))
    if valid_len < L:
        live = _iota((1, L), 1) < valid_len
        ig = jnp.where(live, ig, NEG_INF)
        lf = jnp.where(live, lf, 0.0)
    a_row = _lane_scan(lf, jnp.add, 0.0)
    b_row = ig - a_row
    run_row = jnp.maximum(_lane_scan(b_row, jnp.maximum, NEG_INF), m_prev)
    run_col = _row_to_col(run_row, L)
    causal = _iota((L, L), 1) <= _iota((L, L), 0)
    dmat = jnp.where(causal, jnp.exp(b_row - run_col), 0.0)
    w_inter = jnp.exp(m_prev - run_col)
    s = _dot_nt(q, k) * dmat
    num = w_inter * _dot(q, c_prev) + _dot(s, v)
    qn = jnp.sum(_rnd(q) * _rnd(n_prev), axis=1, keepdims=True)
    den = w_inter * qn + jnp.sum(s, axis=1, keepdims=True)
    a_col = _row_to_col(a_row, L)
    m_t = a_col + run_col
    h = num / jnp.maximum(jnp.abs(den), jnp.exp(-m_t))
    h = _head_norm(h, g) * _sigmoid(o_gate)
    a_last = a_row[:, L - 1:L]
    m_new = a_last + run_row[:, L - 1:L]
    w_c = jnp.exp(a_last + m_prev - m_new)
    ws_row = jnp.exp(a_last - a_row + ig - m_new)
    ws_col = _row_to_col(ws_row, L)
    kw = k * ws_col
    c_new = w_c * c_prev + _dot(kw.T, v)
    n_new = w_c * n_prev + jnp.sum(_rnd(ws_col) * _rnd(k), axis=0, keepdims=True)
    return h, c_new, n_new, m_new


def _mlstm_kernel(q_ref, k_ref, v_ref, o_ref, ig_ref, fg_ref, big_ref, bfg_ref, g_ref, c0_ref, n0_ref, m0_ref,
                  h_ref, c_ref, n_ref, m_ref, c_s, n_s, m_s, *, valid_len, scale):
    ci = pl.program_id(1)

    @pl.when(ci == 0)
    def _():
        c_s[...] = c0_ref[...]
        n_s[...] = n0_ref[...]
        m_s[...] = m0_ref[...]

    nh, dh, _ = c_s.shape
    for hh in range(nh):
        cols = slice(hh * dh, (hh + 1) * dh)
        h, c_new, n_new, m_new = _mlstm_head(
            q_ref[:, cols], k_ref[:, cols] * scale, v_ref[:, cols], o_ref[:, cols],
            ig_ref[hh] + big_ref[hh], fg_ref[hh] + bfg_ref[hh], g_ref[hh], c_s[hh], n_s[hh], m_s[hh], valid_len)
        h_ref[:, cols] = h.astype(h_ref.dtype)
        c_s[hh] = c_new
        n_s[hh] = n_new
        m_s[hh] = m_new

    @pl.when(ci == pl.num_programs(1) - 1)
    def _():
        c_ref[...] = c_s[...]
        n_ref[...] = n_s[...]
        m_ref[...] = m_s[...]


def _mlstm(z, gates_t, b_ig, b_fg, g_mh, c0, n0, m0, valid_len):
    bsz, t, _ = z.shape
    nh, dh = g_mh.shape
    L = CHUNK
    nc = t // L
    w = nh * dh
    assert dh == LANES and t % L == 0
    col = lambda part: pl.BlockSpec((None, L, w), lambda b, c: (b, c, part))
    gate = lambda part: pl.BlockSpec((None, nh, 1, L), lambda b, c: (b, part, 0, c))
    per_head = lambda shp: pl.BlockSpec((nh,) + shp, lambda b, c: (0, 0, 0))
    state = lambda shp: pl.BlockSpec((None, nh) + shp, lambda b, c: (b, 0, 0, 0))
    return pl.pallas_call(
        functools.partial(_mlstm_kernel, valid_len=valid_len, scale=dh ** -0.5),
        grid=(bsz, nc),
        in_specs=[col(0), col(1), col(2), col(3), gate(0), gate(1),
                  per_head((1, 1)), per_head((1, 1)), per_head((1, dh)),
                  state((dh, dh)), state((1, dh)), state((1, 1))],
        out_specs=[pl.BlockSpec((None, L, w), lambda b, c: (b, c, 0)),
                   state((dh, dh)), state((1, dh)), state((1, 1))],
        out_shape=[jax.ShapeDtypeStruct((bsz, t, w), MXU_DTYPE),
                   jax.ShapeDtypeStruct((bsz, nh, dh, dh), F32),
                   jax.ShapeDtypeStruct((bsz, nh, 1, dh), F32),
                   jax.ShapeDtypeStruct((bsz, nh, 1, 1), F32)],
        scratch_shapes=[pltpu.VMEM((nh, dh, dh), F32), pltpu.VMEM((nh, 1, dh), F32), pltpu.VMEM((nh, 1, 1), F32)],
        compiler_params=_params(("parallel", "arbitrary")),
        name="mlstm",
    )(z, z, z, z, gates_t, gates_t, b_ig.reshape(nh, 1, 1), b_fg.reshape(nh, 1, 1), g_mh.reshape(nh, 1, dh),
      c0, n0.reshape(bsz, nh, 1, dh), m0.reshape(bsz, nh, 1, 1))


def _ret_kernel(q_ref, k_ref, v_ref, gd_ref, lg_ref, g_ref, s0_ref, o_ref, s_ref, s_s, *, true_len, scale):
    ci = pl.program_id(1)

    @pl.when(ci == 0)
    def _():
        s_s[...] = s0_ref[...]

    L = q_ref.shape[0]
    nh, dk, dv = s_s.shape
    t_col = _iota((L, 1), 0).astype(F32)
    diff = (_iota((L, L), 0) - _iota((L, L), 1)).astype(F32)
    causal = diff >= 0.0
    chunk_len = float(min(L, true_len))
    for hh in range(nh):
        q = q_ref[:, hh * dk:(hh + 1) * dk]
        k = k_ref[:, hh * dk:(hh + 1) * dk] * scale
        v = v_ref[:, hh * dv:(hh + 1) * dv]
        lg = lg_ref[hh]
        decay = jnp.where(causal, jnp.exp(jnp.where(causal, diff, 0.0) * lg), 0.0)
        s_prev = s_s[hh]
        inner = _dot_nt(q, k) * decay
        o = _dot(inner, v) + jnp.exp((t_col + 1.0) * lg) * _dot(q, s_prev)
        ws_col = jnp.where(t_col < chunk_len, jnp.exp((chunk_len - 1.0 - t_col) * lg), 0.0)
        s_s[hh] = jnp.exp(chunk_len * lg) * s_prev + _dot((k * ws_col).T, v)
        gd = gd_ref[:, hh * dv:(hh + 1) * dv]
        o_ref[:, hh * dv:(hh + 1) * dv] = (_head_norm(o, g_ref[hh]) * (gd * _sigmoid(gd))).astype(o_ref.dtype)

    @pl.when(ci == pl.num_programs(1) - 1)
    def _():
        s_ref[...] = s_s[...]


def _retention(qk, vg, log_gamma, g_ret, s0, true_len):
    bsz, t, _ = qk.shape
    nh, dv = g_ret.shape
    dk = s0.shape[2]
    L = CHUNK
    nc = t // L
    assert dk % LANES == 0 and dv % LANES == 0
    half = lambda wdt, part: pl.BlockSpec((None, L, nh * wdt), lambda b, c: (b, c, part))
    return pl.pallas_call(
        functools.partial(_ret_kernel, true_len=true_len, scale=dk ** -0.5),
        grid=(bsz, nc),
        in_specs=[half(dk, 0), half(dk, 1), half(dv, 0), half(dv, 1),
                  pl.BlockSpec((nh, 1, 1), lambda b, c: (0, 0, 0)),
                  pl.BlockSpec((nh, 1, dv), lambda b, c: (0, 0, 0)),
                  pl.BlockSpec((None, nh, dk, dv), lambda b, c: (b, 0, 0, 0))],
        out_specs=[pl.BlockSpec((None, L, nh * dv), lambda b, c: (b, c, 0)),
                   pl.BlockSpec((None, nh, dk, dv), lambda b, c: (b, 0, 0, 0))],
        out_shape=[jax.ShapeDtypeStruct((bsz, t, nh * dv), MXU_DTYPE),
                   jax.ShapeDtypeStruct((bsz, nh, dk, dv), F32)],
        scratch_shapes=[pltpu.VMEM((nh, dk, dv), F32)],
        compiler_params=_params(("parallel", "arbitrary")),
        name="retention",
    )(qk, qk, vg, vg, log_gamma.reshape(nh, 1, 1), g_ret.reshape(nh, 1, dv), s0)


def _lambda(lq1, lk1, lq2, lk2, lam_init):
    s1 = jnp.sum(lq1 * lk1, axis=1, keepdims=True)
    s2 = jnp.sum(lq2 * lk2, axis=1, keepdims=True)
    return jnp.exp(s1) - jnp.exp(s2) + lam_init


def _rms_sub(o, g, lam_init):
    return o * lax.rsqrt(jnp.mean(o * o, axis=-1, keepdims=True) + EPS) * g * (1.0 - lam_init)


def _diff_kernel(qi_ref, kj_ref, q_ref, k_ref, v_ref, lq1_ref, lk1_ref, lq2_ref, lk2_ref, g_ref, o_ref,
                 m_s, l_s, acc_s, *, lam_init, scale):
    qi = qi_ref[pl.program_id(2)]
    kj = kj_ref[pl.program_id(2)]
    tq, dv = q_ref.shape
    tk = k_ref.shape[0]
    dh = dv // 2

    @pl.when(kj == 0)
    def _():
        m_s[...] = jnp.full(m_s.shape, NEG_INF, F32)
        l_s[...] = jnp.zeros(l_s.shape, F32)
        acc_s[...] = jnp.zeros(acc_s.shape, F32)

    qpos = qi * tq + _iota((tq, tk), 0)
    kpos = kj * tk + _iota((tq, tk), 1)
    ok = kpos <= qpos
    vb = _mx(v_ref[...])
    for c in range(2):
        s = _dot_nt(q_ref[:, c * dh:(c + 1) * dh], k_ref[:, c * dh:(c + 1) * dh]) * scale
        s = jnp.where(ok, s, NEG_INF)
        m_old = m_s[c]
        m_new = jnp.maximum(m_old, _row_max(s))
        alpha = jnp.exp(m_old - m_new)
        p = jnp.exp(s - m_new)
        l_s[c] = alpha * l_s[c] + _row_sum(p)
        acc_s[c] = alpha * acc_s[c] + jnp.dot(_mx(p), vb, preferred_element_type=F32)
        m_s[c] = m_new

    @pl.when(kj == qi)
    def _():
        lam = _lambda(lq1_ref[...], lk1_ref[...], lq2_ref[...], lk2_ref[...], lam_init)
        o = acc_s[0] / l_s[0] - lam * (acc_s[1] / l_s[1])
        o_ref[...] = _rms_sub(o, g_ref[...], lam_init).astype(o_ref.dtype)


def _diff_attn(qb, kb, vb, lams, g_sub, lam_init, nh):
    bsz, t, w = qb.shape
    dv = w // nh
    dh = dv // 2
    tq = tk = min(t, 512)
    nq = t // tq
    pairs = [(i, j) for i in range(nq) for j in range(i + 1)]
    qi_tab = jnp.asarray([p[0] for p in pairs], I32)
    kj_tab = jnp.asarray([p[1] for p in pairs], I32)
    vec = pl.BlockSpec((1, dh), lambda b, h, p, qt, kt: (0, 0))
    qo = pl.BlockSpec((None, tq, dv), lambda b, h, p, qt, kt: (b, qt[p], h))
    kv = pl.BlockSpec((None, tk, dv), lambda b, h, p, qt, kt: (b, kt[p], h))
    return pl.pallas_call(
        functools.partial(_diff_kernel, lam_init=lam_init, scale=dh ** -0.5),
        grid_spec=pltpu.PrefetchScalarGridSpec(
            num_scalar_prefetch=2, grid=(bsz, nh, len(pairs)),
            in_specs=[qo, kv, kv, vec, vec, vec, vec, pl.BlockSpec((1, dv), lambda b, h, p, qt, kt: (0, 0))],
            out_specs=qo,
            scratch_shapes=[pltpu.VMEM((2, tq, 1), F32), pltpu.VMEM((2, tq, 1), F32), pltpu.VMEM((2, tq, dv), F32)]),
        out_shape=jax.ShapeDtypeStruct((bsz, t, w), MXU_DTYPE),
        compiler_params=_params(("parallel", "parallel", "arbitrary")),
        name="diff_attn",
    )(qi_tab, kj_tab, qb, kb, vb, *[x.reshape(1, dh) for x in lams], g_sub.reshape(1, dv))


def _f32_key(x):
    i = lax.bitcast_convert_type(x, I32)
    return i ^ ((i >> 31) & 0x7FFFFFFF)


KEY_NEG_INF = -2139095041
I32_MIN = -2 ** 31


def _strict_upper(n):
    return jnp.where(_iota((n, n), 0) < _iota((n, n), 1), 1.0, 0.0).astype(MXU_DTYPE)


def _dsa_kernel(qc_ref, kc_ref, vc_ref, qi_ref, ki_ref, wi_ref, o_ref, key_s, m_s, l_s, acc_s,
                *, topk, n_heads, n_idx_heads, scale, wi_scale):
    qb = pl.program_id(1)
    tq = qc_ref.shape[0]
    dh = qc_ref.shape[1] // n_heads
    ck = key_s.shape[2]
    d_i = ki_ref.shape[1]
    q0 = qb * tq
    n_chunks = (q0 + tq + ck - 1) // ck

    def _select():
        qi = qi_ref[...]
        wi = _rnd(wi_ref[...] * wi_scale)
        qpos = q0 + _iota((tq, ck), 0)

        def score_chunk(c, carry):
            start = pl.multiple_of(c * ck, ck)
            kic = ki_ref[pl.ds(start, ck), :]
            acc = jnp.zeros((tq, ck), F32)
            for hh in range(n_idx_heads):
                sc = jnp.maximum(_dot_nt(qi[:, hh * d_i:(hh + 1) * d_i], kic), 0.0)
                acc = acc + wi[:, hh:hh + 1] * sc
            acc = acc + 0.0
            kpos = c * ck + _iota((tq, ck), 1)
            key_s[c] = jnp.where(kpos <= qpos, _f32_key(acc), KEY_NEG_INF)
            return carry

        lax.fori_loop(0, n_chunks, score_chunk, 0)

        def count(pred):
            def body(c, acc):
                hit = jnp.where(pred(key_s[c]), 1.0, 0.0)
                part = hit[:, 0:LANES]
                for j in range(1, ck // LANES):
                    part = part + hit[:, j * LANES:(j + 1) * LANES]
                return acc + part
            acc = lax.fori_loop(0, n_chunks, body, jnp.zeros((tq, LANES), F32))
            return jnp.sum(acc, axis=1, keepdims=True)

        def bit_step(it, prefix):
            bit = jnp.left_shift(jnp.int32(1), 31 - it)
            cand = (prefix | bit) ^ I32_MIN
            cnt = count(lambda kk: kk >= cand)
            return jnp.where(cnt >= float(topk), prefix | bit, prefix)

        prefix = lax.fori_loop(0, 32, bit_step, jnp.zeros((tq, 1), I32))
        thr = prefix ^ I32_MIN
        need = float(topk) - count(lambda kk: kk > thr)
        upper = _strict_upper(LANES)

        def mask_chunk(c, taken):
            kk = key_s[c]
            cols = []
            for j in range(ck // LANES):
                kj = kk[:, j * LANES:(j + 1) * LANES]
                eq = kj == thr
                eqf = jnp.where(eq, 1.0, 0.0)
                before = taken + jnp.dot(_mx(eqf), upper, preferred_element_type=F32)
                sel = ((kj > thr) | (eq & (before < need))) & (kj > KEY_NEG_INF)
                cols.append(jnp.where(sel, 0.0, NEG_INF))
                taken = taken + jnp.sum(eqf, axis=1, keepdims=True)
            key_s[c] = lax.bitcast_convert_type(jnp.concatenate(cols, axis=1), I32)
            return taken

        lax.fori_loop(0, n_chunks, mask_chunk, jnp.zeros((tq, 1), F32))

    _select()

    m_s[...] = jnp.full(m_s.shape, NEG_INF, F32)
    l_s[...] = jnp.zeros(l_s.shape, F32)
    acc_s[...] = jnp.zeros(acc_s.shape, F32)

    def attend(c, carry):
        start = pl.multiple_of(c * ck, ck)
        bias = lax.bitcast_convert_type(key_s[c], F32)
        for h in range(n_heads):
            cols = slice(h * dh, (h + 1) * dh)
            s = _dot_nt(qc_ref[:, cols], kc_ref[pl.ds(start, ck), cols]) * scale + bias
            m_old = m_s[h]
            m_new = jnp.maximum(m_old, _row_max(s))
            m_safe = jnp.where(m_new == NEG_INF, 0.0, m_new)
            alpha = jnp.exp(m_old - m_safe)
            p = jnp.exp(s - m_safe)
            l_s[h] = alpha * l_s[h] + _row_sum(p)
            acc_s[h] = alpha * acc_s[h] + _dot(p, vc_ref[pl.ds(start, ck), cols])
            m_s[h] = m_new
        return carry

    lax.fori_loop(0, n_chunks, attend, 0)
    for h in range(n_heads):
        o_ref[:, h * dh:(h + 1) * dh] = (acc_s[h] / l_s[h]).astype(o_ref.dtype)


def _dsa_attn(qc, kc, vc, qi, ki, wi, nh, n_idx_heads):
    bsz, t, w = qc.shape
    dh = w // nh
    d_i = ki.shape[2]
    tq = min(t, 256)
    ck = min(t, 512)
    topk = min(TOPK_MAX, t // 4)
    rows = lambda wdt: pl.BlockSpec((None, tq, wdt), lambda b, i: (b, i, 0))
    whole = lambda wdt: pl.BlockSpec((None, t, wdt), lambda b, i: (b, 0, 0), pipeline_mode=pl.Buffered(1))
    return pl.pallas_call(
        functools.partial(_dsa_kernel, topk=topk, n_heads=nh, n_idx_heads=n_idx_heads, scale=dh ** -0.5,
                          wi_scale=(n_idx_heads * d_i) ** -0.5),
        grid=(bsz, t // tq),
        in_specs=[rows(w), whole(w), whole(w), rows(qi.shape[2]), whole(d_i), rows(wi.shape[2])],
        out_specs=rows(w),
        out_shape=jax.ShapeDtypeStruct((bsz, t, w), MXU_DTYPE),
        scratch_shapes=[pltpu.VMEM((t // ck, tq, ck), I32),
                        pltpu.VMEM((nh, tq, 1), F32), pltpu.VMEM((nh, tq, 1), F32), pltpu.VMEM((nh, tq, dh), F32)],
        compiler_params=_params(("parallel", "arbitrary")),
        name="dsa_attn",
    )(qc, kc, vc, qi, ki, wi)


def _page_copy(cache_hbm, layer, phys, buf, slot, i, sem):
    return pltpu.make_async_copy(cache_hbm.at[layer, phys], buf.at[slot, i], sem.at[slot])


def _paged_step(cache_hbm, layer, pt_ref, buf, sem):
    g = buf.shape[1]
    per_b = pl.num_programs(1)
    step = pl.program_id(0) * per_b + pl.program_id(1)

    def fetch(t):
        b = t // per_b
        s = t % per_b
        for i in range(g):
            _page_copy(cache_hbm, layer, pt_ref[b, s * g + i], buf, t % 2, i, sem).start()

    @pl.when(step == 0)
    def _():
        fetch(step)

    @pl.when(step + 1 < pl.num_programs(0) * per_b)
    def _():
        fetch(step + 1)

    slot = step % 2
    for i in range(g):
        _page_copy(cache_hbm, layer, 0, buf, slot, i, sem).wait()
    return slot


def _paged_call(body, name, page_table, cache, g, in_arrays, in_specs, out_spec, out_shape):
    bd, n_pages = page_table.shape
    assert n_pages % g == 0
    return pl.pallas_call(
        body,
        grid_spec=pltpu.PrefetchScalarGridSpec(
            num_scalar_prefetch=1, grid=(bd, n_pages // g),
            in_specs=in_specs + [pl.BlockSpec(memory_space=pl.ANY)],
            out_specs=out_spec,
            scratch_shapes=[pltpu.VMEM((2, g) + cache.shape[2:], cache.dtype), pltpu.SemaphoreType.DMA((2,))]),
        out_shape=out_shape,
        compiler_params=_params(("arbitrary", "arbitrary")),
        name=name,
    )(page_table, *in_arrays, cache)


def _pages_per_step(n_pages, want):
    g = min(want, n_pages)
    while n_pages % g:
        g -= 1
    return g


def _key_rows(x, nh):
    bd = x.shape[0]
    return x.reshape(bd, nh, 2, LANES).transpose(0, 2, 1, 3).reshape(bd, 2 * nh, LANES)


def _diff_qk_kernel(pt_ref, q_ref, k_hbm, o_ref, buf, sem, *, layer):
    slot = _paged_step(k_hbm, layer, pt_ref, buf, sem)
    _, g, rows, _ = buf.shape
    for i in range(g):
        o_ref[:, i * rows:(i + 1) * rows] = _dot_nt(q_ref[...], buf[slot, i])


def _diff_pv_kernel(pt_ref, a_ref, v_hbm, o_ref, buf, sem, *, layer):
    slot = _paged_step(v_hbm, layer, pt_ref, buf, sem)
    _, g, rows, _ = buf.shape

    @pl.when(pl.program_id(1) == 0)
    def _():
        o_ref[...] = jnp.zeros(o_ref.shape, F32)

    acc = jnp.zeros(o_ref.shape, F32)
    for i in range(g):
        acc = acc + _dot(a_ref[:, i * rows:(i + 1) * rows], buf[slot, i])
    o_ref[...] += acc


def _diff_softmax_kernel(s_ref, q_ref, kn_ref, lq1_ref, lk1_ref, lq2_ref, lk2_ref, a_ref, an_ref,
                         *, lam_init, scale, nh):
    n_rows, width = s_ref.shape
    ch = min(width, 64 * LANES)
    real = (_iota((n_rows, ch), 1) % n_rows) == _iota((n_rows, ch), 0)
    chunks = [slice(i * ch, (i + 1) * ch) for i in range(width // ch)]

    def scores(sl):
        return jnp.where(real, s_ref[:, sl] * scale, NEG_INF)

    s_new = jnp.sum(_rnd(q_ref[...]) * _rnd(kn_ref[...]), axis=1, keepdims=True) * scale
    m = s_new
    for sl in chunks:
        m = jnp.maximum(m, jnp.max(scores(sl), axis=1, keepdims=True))
    e_new = jnp.exp(s_new - m)
    z = e_new
    for sl in chunks:
        z = z + jnp.sum(jnp.exp(scores(sl) - m), axis=1, keepdims=True)
    lam = _lambda(lq1_ref[...], lk1_ref[...], lq2_ref[...], lk2_ref[...], lam_init)
    p_new = e_new / z
    for sl in chunks:
        p = jnp.exp(scores(sl) - m) / z
        p0, p1 = p[:nh], p[nh:]
        a_ref[:, sl] = jnp.concatenate([p0 - lam * pltpu.roll(p1, ch - nh, 1),
                                        pltpu.roll(p0, nh, 1) - lam * p1], axis=0)
    a_new = p_new[:nh] - lam * p_new[nh:]
    an_ref[...] = jnp.broadcast_to(jnp.concatenate([a_new, a_new], axis=0), an_ref.shape)


def _diff_final_kernel(acc_ref, an_ref, vn_ref, g_ref, o_ref, *, lam_init):
    n_rows, wdt = acc_ref.shape
    nh = n_rows // 2
    full = acc_ref[...] + _rnd(an_ref[:, 0:1]) * _rnd(vn_ref[...])
    sq = jnp.sum(full * full, axis=1, keepdims=True)
    ms = (sq + jnp.concatenate([sq[nh:], sq[:nh]], axis=0)) / (2.0 * wdt)
    o_ref[...] = (full * lax.rsqrt(ms + EPS) * g_ref[...] * (1.0 - lam_init)).astype(o_ref.dtype)


def _diff_decode(qb, kb_new, vb_new, cache_k, cache_v, layer, page_table, lams, g_sub, lam_init, nh):
    bd, w = qb.shape
    dv = w // nh
    dh = dv // 2
    assert dh == LANES and 2 * nh == SUBLANES
    n_layers, n_pool, page = cache_k.shape[:3]
    n_pages = page_table.shape[1]
    rows = page * 2 * nh
    width = n_pages * rows
    g = _pages_per_step(n_pages, 8)

    def key_row_view(cache):
        c6 = cache.reshape(n_layers, n_pool, page, nh, 2, LANES)
        return c6.transpose(0, 1, 2, 4, 3, 5).reshape(n_layers, n_pool, rows, LANES)

    q8, k8, v8 = _key_rows(qb, nh), _key_rows(kb_new, nh), _key_rows(vb_new, nh)
    whole = lambda wdt: pl.BlockSpec((None, SUBLANES, wdt), lambda b, p, pt: (b, 0, 0))
    chunk = pl.BlockSpec((None, SUBLANES, g * rows), lambda b, p, pt: (b, 0, p))
    s = _paged_call(
        functools.partial(_diff_qk_kernel, layer=layer), "diff_decode_qk", page_table, key_row_view(cache_k), g,
        [q8], [whole(LANES)], chunk, jax.ShapeDtypeStruct((bd, SUBLANES, width), F32))
    per_b = lambda shp: pl.BlockSpec((None,) + shp, lambda b: (b, 0, 0))
    vec = pl.BlockSpec((1, dh), lambda b: (0, 0))
    a, a_new = pl.pallas_call(
        functools.partial(_diff_softmax_kernel, lam_init=lam_init, scale=dh ** -0.5, nh=nh),
        grid=(bd,),
        in_specs=[per_b((SUBLANES, width)), per_b((SUBLANES, LANES)), per_b((SUBLANES, LANES)), vec, vec, vec, vec],
        out_specs=[per_b((SUBLANES, width)), per_b((SUBLANES, LANES))],
        out_shape=[jax.ShapeDtypeStruct((bd, SUBLANES, width), F32), jax.ShapeDtypeStruct((bd, SUBLANES, LANES), F32)],
        compiler_params=_params(("parallel",)),
        name="diff_decode_softmax",
    )(s, q8, k8, *[x.reshape(1, dh) for x in lams])
    acc = _paged_call(
        functools.partial(_diff_pv_kernel, layer=layer), "diff_decode_pv", page_table, key_row_view(cache_v), g,
        [a], [chunk], whole(LANES), jax.ShapeDtypeStruct((bd, SUBLANES, LANES), F32))
    g8 = jnp.repeat(g_sub.reshape(2, 1, LANES), nh, axis=1).reshape(SUBLANES, LANES)
    out8 = pl.pallas_call(
        functools.partial(_diff_final_kernel, lam_init=lam_init),
        grid=(bd,),
        in_specs=[per_b((SUBLANES, LANES))] * 3 + [pl.BlockSpec((SUBLANES, LANES), lambda b: (0, 0))],
        out_specs=per_b((SUBLANES, LANES)),
        out_shape=jax.ShapeDtypeStruct((bd, SUBLANES, LANES), MXU_DTYPE),
        compiler_params=_params(("parallel",)),
        name="diff_decode_final",
    )(acc, a_new, v8, g8)
    return out8.reshape(bd, 2, nh, LANES).transpose(0, 2, 1, 3).reshape(bd, w)


def _idx_score_kernel(pt_ref, qi_ref, wi_ref, ki_hbm, o_ref, buf, sem, *, layer, wi_scale):
    slot = _paged_step(ki_hbm, layer, pt_ref, buf, sem)
    _, g, page, _ = buf.shape
    w = _rnd(wi_ref[...] * wi_scale)
    for i in range(g):
        sc = jnp.maximum(_dot_nt(qi_ref[...], buf[slot, i]), 0.0)
        o_ref[:, i * page:(i + 1) * page] = jnp.sum(w * _rnd(sc), axis=0, keepdims=True) + 0.0


def _select_kernel(sc_ref, qi_ref, wi_ref, kn_ref, idx_ref, nsel_ref, mnew_ref, pos_s, *, topk, wi_scale):
    npg, page = sc_ref.shape
    sc_new = jnp.maximum(jnp.sum(_rnd(qi_ref[...]) * _rnd(kn_ref[...]), axis=1, keepdims=True), 0.0)
    s_new = jnp.sum(_rnd(wi_ref[...] * wi_scale) * _rnd(sc_new), axis=0, keepdims=True) + 0.0
    keys = _f32_key(sc_ref[...])
    key_new = _f32_key(s_new)

    def total(x):
        return jnp.sum(jnp.sum(x, axis=1, keepdims=True), axis=0, keepdims=True)

    def count(pred):
        return total(jnp.where(pred(keys), 1.0, 0.0)) + jnp.where(pred(key_new), 1.0, 0.0)

    def bit_step(it, prefix):
        bit = jnp.left_shift(jnp.int32(1), 31 - it)
        cand = (prefix | bit) ^ I32_MIN
        return jnp.where(count(lambda kk: kk >= cand) >= float(topk), prefix | bit, prefix)

    thr = lax.fori_loop(0, 32, bit_step, jnp.zeros((1, 1), I32)) ^ I32_MIN
    need = float(topk) - count(lambda kk: kk > thr)
    strict_upper = _strict_upper(page)
    strict_lower = jnp.where(_iota((npg, npg), 1) < _iota((npg, npg), 0), 1.0, 0.0).astype(MXU_DTYPE)

    def count_before(flag):
        in_row = jnp.dot(_mx(flag), strict_upper, preferred_element_type=F32)
        row_tot = jnp.broadcast_to(jnp.sum(flag, axis=1, keepdims=True), (npg, page))
        return in_row + jnp.dot(strict_lower, _mx(row_tot), preferred_element_type=F32)

    eq = keys == thr
    eqf = jnp.where(eq, 1.0, 0.0)
    sel = ((keys > thr) | (eq & (count_before(eqf) < need))) & (keys > KEY_NEG_INF)
    sel_new = ((key_new > thr) | ((key_new == thr) & (total(eqf) < need))) & (key_new > KEY_NEG_INF)
    mnew_ref[...] = jnp.broadcast_to(jnp.where(sel_new, 1.0, 0.0), mnew_ref.shape)
    self = jnp.where(sel, 1.0, 0.0)
    pos_s[...] = jnp.where(sel, count_before(self), -1.0)
    n_slots = idx_ref.shape[0]
    slot_id = _iota((n_slots, page), 0).astype(F32)
    lane = _iota((n_slots, page), 1).astype(F32)

    def gather_page(p, acc):
        hit = pos_s[pl.ds(p, 1), :] == slot_id
        return acc + jnp.where(hit, jnp.asarray(p * page, F32) + lane, 0.0)

    acc = lax.fori_loop(0, npg, gather_page, jnp.zeros((n_slots, page), F32))
    idx_ref[...] = jnp.broadcast_to(jnp.sum(acc, axis=1, keepdims=True).astype(I32), idx_ref.shape)
    nsel_ref[...] = jnp.broadcast_to(total(self), nsel_ref.shape)


def _dsa_gather_kernel(idx_ref, pt_ref, q_ref, kn_ref, vn_ref, mnew_ref, nsel_ref, k_hbm, v_hbm, o_ref,
                       kbuf, vbuf, sem, *, layer, scale):
    b = pl.program_id(0)
    n_slots, nh, dh = kbuf.shape
    page = k_hbm.shape[2]

    def row(cache, i, buf, r, s):
        return pltpu.make_async_copy(cache.at[layer, pt_ref[b, i // page], pl.ds(i % page, 1)],
                                     buf.at[pl.ds(r, 1)], sem.at[s])

    def issue(r, carry):
        i = idx_ref[b, r]
        row(k_hbm, i, kbuf, r, 0).start()
        row(v_hbm, i, vbuf, r, 1).start()
        return carry

    def drain(r, carry):
        row(k_hbm, 0, kbuf, 0, 0).wait()
        row(v_hbm, 0, vbuf, 0, 1).wait()
        return carry

    lax.fori_loop(0, n_slots, issue, 0)
    lax.fori_loop(0, n_slots, drain, 0)
    q = q_ref[...]
    head = _iota((nh, n_slots), 0)
    s = jnp.zeros((nh, n_slots), F32)
    for h in range(nh):
        s = jnp.where(head == h, _dot_nt(q, kbuf[:, h, :]), s)
    live = _iota((nh, n_slots), 1).astype(F32) < nsel_ref[:, 0:1]
    s = jnp.where(live, s * scale, NEG_INF)
    s_new = jnp.sum(_rnd(q) * _rnd(kn_ref[...]), axis=1, keepdims=True) * scale
    s_new = jnp.where(mnew_ref[:, 0:1] > 0.0, s_new, NEG_INF)
    m = jnp.maximum(jnp.max(s, axis=1, keepdims=True), s_new)
    e = jnp.exp(s - m)
    e_new = jnp.exp(s_new - m)
    z = jnp.sum(e, axis=1, keepdims=True) + e_new
    p = e / z
    o = jnp.zeros((nh, dh), F32)
    head = _iota((nh, dh), 0)
    for h in range(nh):
        o = jnp.where(head == h, _dot(p, vbuf[:, h, :]), o)
    o = o + _rnd(e_new / z) * _rnd(vn_ref[...])
    o_ref[...] = o.astype(o_ref.dtype)


def _dsa_decode(qc, kc_new, vc_new, qi, ki_new, wi, cache_k, cache_v, cache_i, layer, page_table, nh, n_idx_heads):
    bd, w = qc.shape
    dh = w // nh
    d_i = ki_new.shape[1]
    page = cache_k.shape[2]
    n_pages = page_table.shape[1]
    past = n_pages * page
    topk = min(TOPK_MAX, (past + 1) // 4)
    n_slots = -(-topk // SUBLANES) * SUBLANES
    qi3 = qi.reshape(bd, n_idx_heads, d_i)
    wi3 = wi[:, :n_idx_heads].reshape(bd, n_idx_heads, 1)
    wi_scale = (n_idx_heads * d_i) ** -0.5
    g = _pages_per_step(n_pages, 16)
    scores = _paged_call(
        functools.partial(_idx_score_kernel, layer=layer, wi_scale=wi_scale), "idx_scores", page_table, cache_i, g,
        [qi3, wi3],
        [pl.BlockSpec((None, n_idx_heads, d_i), lambda b, p, pt: (b, 0, 0)),
         pl.BlockSpec((None, n_idx_heads, 1), lambda b, p, pt: (b, 0, 0))],
        pl.BlockSpec((None, 1, g * page), lambda b, p, pt: (b, 0, p)),
        jax.ShapeDtypeStruct((bd, 1, past), F32))
    per_b = lambda shp: pl.BlockSpec((None,) + shp, lambda b: (b, 0, 0))
    sel_idx, n_sel, mask_new = pl.pallas_call(
        functools.partial(_select_kernel, topk=topk, wi_scale=wi_scale),
        grid=(bd,),
        in_specs=[per_b((n_pages, page)), per_b((n_idx_heads, d_i)), per_b((n_idx_heads, 1)), per_b((1, d_i))],
        out_specs=[per_b((n_slots, LANES)), per_b((1, LANES)), per_b((1, LANES))],
        out_shape=[jax.ShapeDtypeStruct((bd, n_slots, LANES), I32), jax.ShapeDtypeStruct((bd, 1, LANES), F32),
                   jax.ShapeDtypeStruct((bd, 1, LANES), F32)],
        scratch_shapes=[pltpu.VMEM((n_pages, page), F32)],
        compiler_params=_params(("parallel",)),
        name="idx_select",
    )(scores.reshape(bd, n_pages, page), qi3, wi3, ki_new.reshape(bd, 1, d_i))
    per_b2 = lambda shp: pl.BlockSpec((None,) + shp, lambda b, idx, pt: (b, 0, 0))
    hbm = pl.BlockSpec(memory_space=pl.ANY)
    return pl.pallas_call(
        functools.partial(_dsa_gather_kernel, layer=layer, scale=dh ** -0.5),
        grid_spec=pltpu.PrefetchScalarGridSpec(
            num_scalar_prefetch=2, grid=(bd,),
            in_specs=[per_b2((nh, dh)), per_b2((nh, dh)), per_b2((nh, dh)), per_b2((1, LANES)), per_b2((1, LANES)),
                      hbm, hbm],
            out_specs=per_b2((nh, dh)),
            scratch_shapes=[pltpu.VMEM((n_slots, nh, dh), F32), pltpu.VMEM((n_slots, nh, dh), F32),
                            pltpu.SemaphoreType.DMA((2,))]),
        out_shape=jax.ShapeDtypeStruct((bd, nh, dh), MXU_DTYPE),
        compiler_params=_params(("arbitrary",)),
        name="dsa_decode_attn",
    )(sel_idx[:, :, 0], page_table, qc.reshape(bd, nh, dh), kc_new.reshape(bd, nh, dh), vc_new.reshape(bd, nh, dh),
      mask_new, n_sel, cache_k, cache_v).reshape(bd, w)


def _route_kernel(lg_ref, br_ref, e_ref, g_ref, r_ref, cnt_ref, carry_s, *, n_valid, n_groups):
    i = pl.program_id(0)
    ne, tn = lg_ref.shape
    per = ne // n_groups

    @pl.when(i == 0)
    def _():
        carry_s[...] = jnp.zeros(carry_s.shape, F32)

    aff = _sigmoid(lg_ref[...])
    sel = aff + br_ref[...]
    sub = _iota((per, tn), 0).astype(F32)
    best = None
    for gi in range(n_groups):
        s = sel[gi * per:(gi + 1) * per]
        m1 = jnp.max(s, axis=0, keepdims=True)
        i1 = jnp.min(jnp.where(s == m1, sub, float(per)), axis=0, keepdims=True)
        s2 = jnp.where(sub == i1, NEG_INF, s)
        m2 = jnp.max(s2, axis=0, keepdims=True)
        i2 = jnp.min(jnp.where(s2 == m2, sub, float(per)), axis=0, keepdims=True)
        cand = (m1 + m2, float(gi * per) + i1, float(gi * per) + i2)
        if best is None:
            best = cand
        else:
            better = cand[0] > best[0]
            best = tuple(jnp.where(better, c, b) for c, b in zip(cand, best))
    e0, e1 = best[1].astype(I32), best[2].astype(I32)
    eid = _iota((ne, tn), 0)
    valid = (i * tn + _iota((1, tn), 1)) < n_valid
    oh0 = (eid == e0) & valid
    oh1 = (eid == e1) & valid
    a0 = jnp.sum(jnp.where(eid == e0, aff, 0.0), axis=0, keepdims=True)
    a1 = jnp.sum(jnp.where(eid == e1, aff, 0.0), axis=0, keepdims=True)
    tot = a0 + a1
    e_ref[...] = jnp.concatenate([e0, e1], axis=0)
    g_ref[...] = jnp.concatenate([a0 / tot, a1 / tot], axis=0)
    oh = jnp.where(oh0, 1.0, 0.0) + jnp.where(oh1, 1.0, 0.0)
    before = carry_s[:, 0:1] + jnp.dot(_mx(oh), _strict_upper(tn), preferred_element_type=F32)
    r0 = jnp.sum(jnp.where(oh0, before, 0.0), axis=0, keepdims=True)
    r1 = jnp.sum(jnp.where(oh1, before, 0.0), axis=0, keepdims=True)
    r_ref[...] = jnp.concatenate([r0, r1], axis=0).astype(I32)
    carry_s[...] = carry_s[...] + jnp.sum(oh, axis=1, keepdims=True)
    cnt_ref[...] = carry_s[...]


def _route(logits_t, b_router, n_valid):
    ne, mp = logits_t.shape
    tn = min(mp, 256)
    tok = lambda i: (0, i)
    return pl.pallas_call(
        functools.partial(_route_kernel, n_valid=n_valid, n_groups=N_GROUPS),
        grid=(mp // tn,),
        in_specs=[pl.BlockSpec((ne, tn), tok), pl.BlockSpec((ne, 1), lambda i: (0, 0))],
        out_specs=[pl.BlockSpec((TOP_K, tn), tok), pl.BlockSpec((TOP_K, tn), tok), pl.BlockSpec((TOP_K, tn), tok),
                   pl.BlockSpec((ne, LANES), lambda i: (0, 0))],
        out_shape=[jax.ShapeDtypeStruct((TOP_K, mp), I32), jax.ShapeDtypeStruct((TOP_K, mp), F32),
                   jax.ShapeDtypeStruct((TOP_K, mp), I32), jax.ShapeDtypeStruct((ne, LANES), F32)],
        scratch_shapes=[pltpu.VMEM((ne, LANES), F32)],
        compiler_params=_params(("arbitrary",)),
        name="moe_route",
    )(logits_t, b_router.reshape(ne, 1))


def _row_copy(src, s_row, dst, d_row, sem):
    return pltpu.make_async_copy(src.at[pl.ds(s_row, 1)], dst.at[pl.ds(d_row, 1)], sem)


def _expert_kernel(be_ref, nu_ref, src_ref, x_hbm, wg_ref, wu_ref, wd_ref, o_ref, xbuf, sem):
    b = pl.program_id(0)
    n_used = nu_ref[0]
    blk = xbuf.shape[1]

    def fetch(block):
        def issue(r, carry):
            _row_copy(x_hbm, src_ref[block * blk + r], xbuf.at[block % 2], r, sem.at[block % 2]).start()
            return carry
        lax.fori_loop(0, blk, issue, 0, unroll=8)

    @pl.when(b == 0)
    def _():
        fetch(b)

    @pl.when(b + 1 < n_used)
    def _():
        fetch(b + 1)

    @pl.when(b < n_used)
    def _():
        def drain(r, carry):
            _row_copy(x_hbm, 0, xbuf.at[b % 2], 0, sem.at[b % 2]).wait()
            return carry
        lax.fori_loop(0, blk, drain, 0, unroll=8)
        x = _mx(xbuf[b % 2])
        gate = jnp.dot(x, _mx(wg_ref[...]), preferred_element_type=F32)
        up = jnp.dot(x, _mx(wu_ref[...]), preferred_element_type=F32)
        hdn = gate * _sigmoid(gate) * up
        y = jnp.dot(_mx(hdn), _mx(wd_ref[...]), preferred_element_type=F32)
        o_ref[...] = _rnd(y)

    @pl.when(b >= n_used)
    def _():
        o_ref[...] = jnp.zeros(o_ref.shape, F32)


def _experts(blk_e, n_used, src, x, w_gate, w_up, w_down, layer, blk):
    n_rows = src.shape[0]
    d = x.shape[1]
    de = w_gate.shape[3]
    wspec = lambda shp: pl.BlockSpec((None, None) + shp, lambda b, be, nu, sr: (layer, be[b], 0, 0))
    return pl.pallas_call(
        _expert_kernel,
        grid_spec=pltpu.PrefetchScalarGridSpec(
            num_scalar_prefetch=3, grid=(n_rows // blk,),
            in_specs=[pl.BlockSpec(memory_space=pl.ANY), wspec((d, de)), wspec((d, de)), wspec((de, d))],
            out_specs=pl.BlockSpec((blk, d), lambda b, be, nu, sr: (b, 0)),
            scratch_shapes=[pltpu.VMEM((2, blk, d), F32), pltpu.SemaphoreType.DMA((2,))]),
        out_shape=jax.ShapeDtypeStruct((n_rows, d), F32),
        compiler_params=_params(("arbitrary",)),
        name="moe_experts",
    )(blk_e, n_used, src, x, w_gate, w_up, w_down)


def _combine_kernel(dest_ref, y_hbm, x_ref, gt_ref, g_ref, b_ref, xo_ref, xb_ref, rows_s, sem, *, n_tok, alpha):
    tc = x_ref.shape[0]
    base = pl.program_id(0) * tc

    def issue(r, carry):
        for kk in range(TOP_K):
            _row_copy(y_hbm, dest_ref[kk * n_tok + base + r], rows_s.at[kk], r, sem).start()
        return carry

    def drain(r, carry):
        for kk in range(TOP_K):
            _row_copy(y_hbm, 0, rows_s.at[kk], 0, sem).wait()
        return carry

    lax.fori_loop(0, tc, issue, 0, unroll=8)
    lax.fori_loop(0, tc, drain, 0, unroll=8)
    gt = gt_ref[...]
    y = _rnd(gt[:, 0:1]) * rows_s[0] + _rnd(gt[:, 1:2]) * rows_s[1]
    out = _layer_norm(alpha * x_ref[...] + y, g_ref[...], b_ref[...])
    xo_ref[...] = out
    xb_ref[...] = out.astype(xb_ref.dtype)


def _combine(dest_flat, y, x, gates, g, bt, alpha):
    n_tok, d = x.shape
    tc = min(n_tok, 128)
    row = lambda i, dst: (i, 0)
    fixed = lambda i, dst: (0, 0)
    return pl.pallas_call(
        functools.partial(_combine_kernel, n_tok=n_tok, alpha=alpha),
        grid_spec=pltpu.PrefetchScalarGridSpec(
            num_scalar_prefetch=1, grid=(n_tok // tc,),
            in_specs=[pl.BlockSpec(memory_space=pl.ANY), pl.BlockSpec((tc, d), row), pl.BlockSpec((tc, TOP_K), row),
                      pl.BlockSpec((1, d), fixed), pl.BlockSpec((1, d), fixed)],
            out_specs=[pl.BlockSpec((tc, d), row), pl.BlockSpec((tc, d), row)],
            scratch_shapes=[pltpu.VMEM((TOP_K, tc, d), F32), pltpu.SemaphoreType.DMA(())]),
        out_shape=[jax.ShapeDtypeStruct((n_tok, d), F32), jax.ShapeDtypeStruct((n_tok, d), MXU_DTYPE)],
        compiler_params=_params(("arbitrary",)),
        name="moe_combine",
    )(dest_flat, y, x, gates, g.reshape(1, d), bt.reshape(1, d))


def _moe_ln(x, xb, w_router_t, b_router, w_gate, w_up, w_down, layer, g, bt, alpha, n_valid, blk):
    m, d = x.shape
    ne = w_gate.shape[1]
    mp = max(m, LANES)
    logits_t = _mm_nt(w_router_t, xb)
    if mp != m:
        logits_t = jnp.pad(logits_t, ((0, 0), (0, mp - m)))
    eidx, gates, rank, counts = _route(logits_t, b_router, n_valid)
    counts = counts[:, 0].astype(I32)
    padded = (counts + blk - 1) // blk * blk
    pad_end = jnp.cumsum(padded)
    pad_start = pad_end - padded
    n_blocks = -(-(n_valid * TOP_K + ne * (blk - 1)) // blk)
    first_slot = jnp.sum(jnp.where(eidx[:, :m, None] == jnp.arange(ne), pad_start, 0), axis=-1)
    dest = (first_slot + rank[:, :m]).astype(I32)
    n_rows = n_blocks * blk
    live = jnp.arange(m)[None, :] < n_valid
    tok = jnp.broadcast_to(jnp.arange(m, dtype=I32)[None, :], (TOP_K, m))
    src = jnp.zeros((n_rows,), I32).at[jnp.where(live, dest, n_rows).reshape(-1)].set(tok.reshape(-1), mode="drop")
    dest_flat = jnp.where(live, dest, 0).reshape(-1)
    blk_e = jnp.minimum(jnp.searchsorted(pad_end, jnp.arange(n_blocks) * blk, side="right"), ne - 1).astype(I32)
    n_used = (pad_end[-1:] // blk).astype(I32)
    y = _experts(blk_e, n_used, src, x, w_gate, w_up, w_down, layer, blk)
    return _combine(dest_flat, y, x, gates[:, :m].T, g, bt, alpha)


def _rope_tables(pos, half):
    inv = ROPE_THETA ** (-jnp.arange(half, dtype=F32) / half)
    ang = pos.astype(F32)[:, None] * inv[None, :]
    c, s = jnp.cos(ang), jnp.sin(ang)
    reps = LANES // (2 * half)
    return (jnp.tile(jnp.concatenate([c, c], -1), (1, reps)), jnp.tile(jnp.concatenate([-s, s], -1), (1, reps)), half)


def _cols(w, start, width, pad_to=None):
    out = w[:, start:start + width].astype(MXU_DTYPE)
    if pad_to is not None and pad_to > width:
        out = jnp.pad(out, ((0, 0), (0, pad_to - width)))
    return out


def kernel(x_prompt, x_sample, state_mlstm_c, state_mlstm_n, state_mlstm_m, cache_diff_k, cache_diff_v, cache_dsa_k, cache_dsa_v, cache_idx_k, state_ret, page_table, w_in_even, w_out_even, b_igate, b_fgate, g_mlstm, lam_q1, lam_k1, lam_q2, lam_k2, g_subln, w_in_odd, w_out_odd, g_ret, ln_mix_g, ln_mix_b, ln_ffn_g, ln_ffn_b, w_router, b_router, w_gate, w_up, w_down):
    bsz, seq, d_model = x_prompt.shape
    dec_b, dec_seq, _ = x_sample.shape
    assert dec_seq == 1
    depth = w_gate.shape[0]
    alpha = (2 * depth) ** 0.25
    h_a, dh_a = g_mlstm.shape[1:]
    h_b, dv_b = cache_diff_k.shape[3:]
    h_c, dh_c = cache_dsa_k.shape[3:]
    d_i = cache_idx_k.shape[3]
    h_d, dk_d, dv_d = state_ret.shape[2:]
    w_a, w_b, w_c, w_d = h_a * dh_a, h_b * dv_b, h_c * dh_c, h_d * dv_d
    h_i = w_in_odd.shape[2] - (3 * w_c + d_i + 2 * h_d * dk_d + 2 * w_d)
    h_i = h_i // (d_i + 1)
    past_len = page_table.shape[1] * cache_diff_k.shape[2]
    m_p = bsz * seq
    m_s = 2 * SUBLANES

    pos_p = jnp.arange(seq)
    pos_s = jnp.full((m_s,), past_len)
    rope_p = {h: _rope_tables(pos_p, h) for h in (dh_c // 2, d_i // 2)}
    rope_s = {h: _rope_tables(pos_s, h) for h in (dh_c // 2, d_i // 2)}
    log_gamma = jnp.log(1.0 - 2.0 ** (-5.0 - jnp.arange(h_d, dtype=F32)))
    w_router_t = w_router.T.astype(MXU_DTYPE)
    w_gate, w_up, w_down = (a.astype(MXU_DTYPE) for a in (w_gate, w_up, w_down))

    xp = x_prompt.reshape(m_p, d_model)
    xs = jnp.pad(x_sample.reshape(dec_b, d_model), ((0, m_s - dec_b), (0, 0)))
    xp_b, xs_b = xp.astype(MXU_DTYPE), xs.astype(MXU_DTYPE)
    names = ("mlstm_c", "mlstm_n", "mlstm_m", "diff_k", "diff_v", "dsa_k", "dsa_v", "idx_k", "ret")
    new_p = {k: [] for k in names}
    new_s = {k: [] for k in names}

    def pad_tokens(a):
        return jnp.pad(a[:dec_b, None, :], ((0, 0), (0, CHUNK - 1), (0, 0)))

    for l in range(depth):
        j = l // 2
        if l % 2 == 0:
            w = w_in_even[j]
            o_g = 4 * w_a
            o_b = o_g + 2 * h_a
            w_main = _cols(w, 0, 4 * w_a)
            w_gates_t = w[:, o_g:o_b].T.astype(MXU_DTYPE)
            w_qb, w_kb, w_vb = (_cols(w, o_b + i * w_b, w_b) for i in range(3))
            lams = (lam_q1[j], lam_k1[j], lam_q2[j], lam_k2[j])
            lam_init = 0.8 - 0.6 * math.exp(-0.3 * l)
            half = dv_b // 4
            z = _mm(xp_b, w_main).reshape(bsz, seq, 4 * w_a)
            gt = _mm_nt(w_gates_t, xp_b).reshape(2 * h_a, bsz, seq).transpose(1, 0, 2).reshape(bsz, 2 * h_a, 1, seq)
            r3 = lambda a: a.reshape(bsz, seq, a.shape[1])
            qb_m = _mm(xp_b, w_qb, rope=rope_p[half], out_dtype=MXU_DTYPE)
            kb, kb_m = _mm(xp_b, w_kb, rope=rope_p[half], mxu_copy=True)
            vb, vb_m = _mm(xp_b, w_vb, mxu_copy=True)
            zero = lambda *s: jnp.zeros(s, F32)
            h_mix, c_p, n_p, mm_p = _mlstm(z, gt, b_igate[j], b_fgate[j], g_mlstm[j], zero(bsz, h_a, dh_a, dh_a),
                                           zero(bsz, h_a, dh_a), zero(bsz, h_a), CHUNK)
            ob = _diff_attn(r3(qb_m), r3(kb_m), r3(vb_m), lams, g_subln[j], lam_init, h_b)
            mix_a_p, mix_b_p = h_mix.reshape(m_p, w_a), ob.reshape(m_p, w_b)
            st_p = (c_p, n_p.reshape(bsz, h_a, dh_a), mm_p.reshape(bsz, h_a),
                    kb.reshape(bsz, seq, h_b, dv_b), vb.reshape(bsz, seq, h_b, dv_b))
            zs = pad_tokens(_mm(xs_b, w_main))
            gts = _mm_nt(w_gates_t, xs_b)[:, :dec_b].T
            gts = jnp.pad(gts[:, :, None, None], ((0, 0), (0, 0), (0, 0), (0, CHUNK - 1)))
            qbs = _mm(xs_b, w_qb, rope=rope_s[half])[:dec_b]
            kbs = _mm(xs_b, w_kb, rope=rope_s[half])[:dec_b]
            vbs = _mm(xs_b, w_vb)[:dec_b]
            hs_mix, c_s, n_s, mm_s = _mlstm(zs, gts, b_igate[j], b_fgate[j], g_mlstm[j], state_mlstm_c[j],
                                            state_mlstm_n[j], state_mlstm_m[j], 1)
            obs = _diff_decode(qbs, kbs, vbs, cache_diff_k, cache_diff_v, j, page_table, lams, g_subln[j], lam_init, h_b)
            pad_rows = lambda a: jnp.pad(a, ((0, m_s - dec_b), (0, 0)))
            mix_a_s, mix_b_s = pad_rows(hs_mix[:, 0, :]), pad_rows(obs)
            st_s = (c_s, n_s.reshape(dec_b, h_a, dh_a), mm_s.reshape(dec_b, h_a),
                    kbs.reshape(dec_b, 1, h_b, dv_b), vbs.reshape(dec_b, 1, h_b, dv_b))
            w_out = w_out_even[j].astype(MXU_DTYPE)
            keys = names[:5]
        else:
            w = w_in_odd[j]
            o_qi = 3 * w_c
            o_ki = o_qi + h_i * d_i
            o_wi = o_ki + d_i
            o_qd = o_wi + h_i
            o_vd = o_qd + 2 * h_d * dk_d
            w_qc, w_kc, w_vc = (_cols(w, i * w_c, w_c) for i in range(3))
            w_qi = _cols(w, o_qi, h_i * d_i)
            w_ki = _cols(w, o_ki, d_i, pad_to=LANES)
            w_wi = _cols(w, o_wi, h_i, pad_to=LANES)
            w_qkd = _cols(w, o_qd, 2 * h_d * dk_d)
            w_vgd = _cols(w, o_vd, 2 * w_d)
            hc, hi, hd = dh_c // 2, d_i // 2, dk_d // 2

            def project(xb_, rope, q_dtype):
                qc = _mm(xb_, w_qc, rope=rope[hc], out_dtype=q_dtype)
                kc, kc_m = _mm(xb_, w_kc, rope=rope[hc], mxu_copy=True)
                vc, vc_m = _mm(xb_, w_vc, mxu_copy=True)
                qi = _mm(xb_, w_qi, rope=rope[hi], out_dtype=q_dtype)
                ki = _mm(xb_, w_ki, rope=rope[hi], n_out=d_i)
                wi = _mm(xb_, w_wi)
                qkd = _mm(xb_, w_qkd, rope=rope[hd])
                vgd = _mm(xb_, w_vgd)
                return (qc, kc, vc, qi, ki, wi, qkd, vgd), (kc_m, vc_m)

            (qc, kc, vc, qi, ki, wi, qkd, vgd), (kc_m, vc_m) = project(xp_b, rope_p, MXU_DTYPE)
            r3 = lambda a: a.reshape(bsz, seq, a.shape[1])
            oc = _dsa_attn(r3(qc), r3(kc_m), r3(vc_m), r3(qi), r3(ki), r3(wi), h_c, h_i)
            od, s_p = _retention(r3(qkd), r3(vgd), log_gamma, g_ret[j], jnp.zeros((bsz, h_d, dk_d, dv_d), F32), CHUNK)
            mix_a_p, mix_b_p = oc.reshape(m_p, w_c), od.reshape(m_p, w_d)
            st_p = (kc.reshape(bsz, seq, h_c, dh_c), vc.reshape(bsz, seq, h_c, dh_c), ki.reshape(bsz, seq, d_i), s_p)
            qc, kc, vc, qi, ki, wi, qkd, vgd = (a[:dec_b] for a in project(xs_b, rope_s, F32)[0])
            ocs = _dsa_decode(qc, kc, vc, qi, ki, wi, cache_dsa_k, cache_dsa_v, cache_idx_k, j, page_table, h_c, h_i)
            ods, s_s = _retention(pad_tokens(qkd), pad_tokens(vgd), log_gamma, g_ret[j], state_ret[j], 1)
            pad_rows = lambda a: jnp.pad(a, ((0, m_s - dec_b), (0, 0)))
            mix_a_s, mix_b_s = pad_rows(ocs), pad_rows(ods[:, 0, :])
            st_s = (kc.reshape(dec_b, 1, h_c, dh_c), vc.reshape(dec_b, 1, h_c, dh_c), ki.reshape(dec_b, 1, d_i), s_s)
            w_out = w_out_odd[j].astype(MXU_DTYPE)
            keys = names[5:]
        for k, a, b in zip(keys, st_p, st_s):
            new_p[k].append(a)
            new_s[k].append(b)
        xp, xp_b = _proj_ln(mix_a_p, mix_b_p, w_out, xp, ln_mix_g[l], ln_mix_b[l], alpha)
        xs, xs_b = _proj_ln(mix_a_s, mix_b_s, w_out, xs, ln_mix_g[l], ln_mix_b[l], alpha)
        moe_w = (w_router_t, b_router, w_gate, w_up, w_down, l, ln_ffn_g[l], ln_ffn_b[l], alpha)
        xp, xp_b = _moe_ln(xp, xp_b, *moe_w, m_p, 256)
        xs, xs_b = _moe_ln(xs, xs_b, *moe_w, dec_b, 16)

    out = [xp.reshape(bsz, seq, d_model), xs[:dec_b].reshape(dec_b, 1, d_model)]
    for k in names:
        out += [jnp.stack(new_p[k]), jnp.stack(new_s[k])]
    return tuple(out)
```

```python
import functools
import math

import jax
import jax.numpy as jnp
from jax import lax
from jax.experimental import pallas as pl
from jax.experimental.pallas import tpu as pltpu

F32 = jnp.float32
I32 = jnp.int32
MXU_DTYPE = jnp.bfloat16

N_GROUPS = 4
TOP_K = 2
TOPK_MAX = 256
CHUNK = 128
ROPE_THETA = 10000.0
EPS = 1e-5

LANES = 128
SUBLANES = 8
VMEM_LIMIT_BYTES = 52 * 1024 * 1024

NEG_INF = float("-inf")


def _mx(x):
    return x.astype(MXU_DTYPE)


def _rnd(x):
    return x.astype(MXU_DTYPE).astype(F32)


def _dot(a, b):
    return jnp.dot(_mx(a), _mx(b), preferred_element_type=F32)


def _dot_nt(a, b):
    return lax.dot_general(_mx(a), _mx(b), (((1,), (1,)), ((), ())), preferred_element_type=F32)


def _params(sem, vmem=VMEM_LIMIT_BYTES):
    return pltpu.CompilerParams(dimension_semantics=sem, vmem_limit_bytes=vmem)


def _iota(shape, axis):
    return lax.broadcasted_iota(I32, shape, axis)


def _sigmoid(x):
    return 1.0 / (1.0 + jnp.exp(-x))


def _lane_fold(x, op):
    acc = x[:, 0:LANES]
    for j in range(1, x.shape[1] // LANES):
        acc = op(acc, x[:, j * LANES:(j + 1) * LANES])
    return acc


def _row_max(x):
    return jnp.max(_lane_fold(x, jnp.maximum), axis=1, keepdims=True)


def _row_sum(x):
    return jnp.sum(_lane_fold(x, jnp.add), axis=1, keepdims=True)


def _row_to_col(row, n):
    eye = _iota((n, n), 0) == _iota((n, n), 1)
    return jnp.sum(jnp.where(eye, jnp.broadcast_to(row, (n, n)), 0.0), axis=1, keepdims=True)


def _head_norm(x, g):
    mu = jnp.mean(x, axis=-1, keepdims=True)
    xc = x - mu
    var = jnp.mean(xc * xc, axis=-1, keepdims=True)
    return xc * lax.rsqrt(var + EPS) * g


def _layer_norm(z, g, b):
    mu = jnp.mean(z, axis=-1, keepdims=True)
    zc = z - mu
    var = jnp.mean(zc * zc, axis=-1, keepdims=True)
    return zc * lax.rsqrt(var + EPS) * g + b


def _rope_apply(a, cos, sin, half):
    if 2 * half == LANES:
        r = pltpu.roll(a, half, 1)
    else:
        lane = _iota(a.shape, 1)
        r = jnp.where((lane % (2 * half)) < half, pltpu.roll(a, LANES - half, 1), pltpu.roll(a, half, 1))
    return a * cos + r * sin


def _mm_kernel(x_ref, w_ref, *rest, rope_half, n_outs):
    o_refs = rest[len(rest) - n_outs:]
    acc = jnp.dot(x_ref[...], w_ref[...], preferred_element_type=F32)
    if rope_half:
        cos = rest[0][...]
        sin = rest[1][...]
        n_out = o_refs[0].shape[1]
        for j in range(acc.shape[1] // LANES):
            res = _rope_apply(acc[:, j * LANES:(j + 1) * LANES], cos, sin, rope_half)
            nw = min(LANES, n_out - j * LANES)
            for o_ref in o_refs:
                o_ref[:, j * LANES:j * LANES + nw] = res[:, :nw].astype(o_ref.dtype)
    else:
        for o_ref in o_refs:
            o_ref[...] = acc.astype(o_ref.dtype)


def _mm(x, w, rope=None, n_out=None, out_dtype=F32, mxu_copy=False):
    m, k = x.shape
    n = w.shape[1]
    n_out = n if n_out is None else n_out
    tm = min(m, 1024) if rope is None else min(m, 1024, rope[0].shape[0])
    tn = min(n, 512)
    assert m % tm == 0 and n % tn == 0 and (n_out == n or n == tn)
    in_specs = [pl.BlockSpec((tm, k), lambda i, j: (i, 0)), pl.BlockSpec((k, tn), lambda i, j: (0, j))]
    args = [x, w]
    half = 0
    if rope is not None:
        cos, sin, half = rope
        nt = cos.shape[0] // tm
        assert cos.shape[0] % tm == 0
        in_specs += [pl.BlockSpec((tm, LANES), lambda i, j: (i % nt, 0))] * 2
        args += [cos, sin]
    out_spec = pl.BlockSpec((tm, min(tn, n_out)), lambda i, j: (i, j))
    dtypes = (out_dtype, MXU_DTYPE) if mxu_copy else (out_dtype,)
    outs = pl.pallas_call(
        functools.partial(_mm_kernel, rope_half=half, n_outs=len(dtypes)),
        grid=(m // tm, n // tn),
        in_specs=in_specs,
        out_specs=[out_spec] * len(dtypes),
        out_shape=[jax.ShapeDtypeStruct((m, n_out), dt) for dt in dtypes],
        compiler_params=_params(("parallel", "parallel")),
        name="proj",
    )(*args)
    return outs if mxu_copy else outs[0]


def _mm_nt_kernel(w_ref, x_ref, o_ref):
    o_ref[...] = lax.dot_general(w_ref[...], x_ref[...], (((1,), (1,)), ((), ())), preferred_element_type=F32)


def _mm_nt(w_t, x):
    n, k = w_t.shape
    m = x.shape[0]
    tm = min(m, 512)
    return pl.pallas_call(
        _mm_nt_kernel,
        grid=(m // tm,),
        in_specs=[pl.BlockSpec((n, k), lambda i: (0, 0)), pl.BlockSpec((tm, k), lambda i: (i, 0))],
        out_specs=pl.BlockSpec((n, tm), lambda i: (0, i)),
        out_shape=jax.ShapeDtypeStruct((n, m), F32),
        compiler_params=_params(("parallel",)),
        name="proj_t",
    )(w_t, x)


def _proj_ln_kernel(a_ref, b_ref, wa_ref, wb_ref, x_ref, g_ref, bt_ref, xo_ref, xb_ref, *, alpha):
    y = jnp.dot(a_ref[...], wa_ref[...], preferred_element_type=F32)
    y = y + jnp.dot(b_ref[...], wb_ref[...], preferred_element_type=F32)
    out = _layer_norm(alpha * x_ref[...] + y, g_ref[...], bt_ref[...])
    xo_ref[...] = out
    xb_ref[...] = out.astype(xb_ref.dtype)


def _proj_ln(a, b, w_out, x, g, bt, alpha):
    m, d = x.shape
    ka = a.shape[1]
    tm = min(m, 256)
    wa, wb = w_out[:ka], w_out[ka:]
    row = lambda i: (i, 0)
    fixed = lambda i: (0, 0)
    return pl.pallas_call(
        functools.partial(_proj_ln_kernel, alpha=alpha),
        grid=(m // tm,),
        in_specs=[pl.BlockSpec((tm, ka), row), pl.BlockSpec((tm, b.shape[1]), row),
                  pl.BlockSpec(wa.shape, fixed), pl.BlockSpec(wb.shape, fixed),
                  pl.BlockSpec((tm, d), row), pl.BlockSpec((1, d), fixed), pl.BlockSpec((1, d), fixed)],
        out_specs=[pl.BlockSpec((tm, d), row), pl.BlockSpec((tm, d), row)],
        out_shape=[jax.ShapeDtypeStruct((m, d), F32), jax.ShapeDtypeStruct((m, d), MXU_DTYPE)],
        compiler_params=_params(("parallel",)),
        name="out_proj_ln",
    )(a, b, wa, wb, x, g.reshape(1, d), bt.reshape(1, d))


def _lane_scan(x, op, fill):
    n = x.shape[1]
    lane = _iota(x.shape, 1)
    s = 1
    while s < n:
        x = op(x, jnp.where(lane >= s, pltpu.roll(x, s, 1), fill))
        s *= 2
    return x


def _mlstm_head(q, k, v, o_gate, ig, fpre, g, c_prev, n_prev, m_prev, valid_len):
    L = q.shape[0]
    lf = jnp.minimum(fpre, 0.0) - jnp.log1p(jnp.exp(-jnp.abs(fpre)))
    if valid_len < L:
        live = _iota((1, L), 1) < valid_len
        ig = jnp.where(live, ig, NEG_INF)
        lf = jnp.where(live, lf, 0.0)
    a_row = _lane_scan(lf, jnp.add, 0.0)
    b_row = ig - a_row
    run_row = jnp.maximum(_lane_scan(b_row, jnp.maximum, NEG_INF), m_prev)
    run_col = _row_to_col(run_row, L)
    causal = _iota((L, L), 1) <= _iota((L, L), 0)
    dmat = jnp.where(causal, jnp.exp(b_row - run_col), 0.0)
    w_inter = jnp.exp(m_prev - run_col)
    s = _dot_nt(q, k) * dmat
    num = w_inter * _dot(q, c_prev) + _dot(s, v)
    qn = jnp.sum(_rnd(q) * _rnd(n_prev), axis=1, keepdims=True)
    den = w_inter * qn + jnp.sum(s, axis=1, keepdims=True)
    a_col = _row_to_col(a_row, L)
    m_t = a_col + run_col
    h = num / jnp.maximum(jnp.abs(den), jnp.exp(-m_t))
    h = _head_norm(h, g) * _sigmoid(o_gate)
    a_last = a_row[:, L - 1:L]
    m_new = a_last + run_row[:, L - 1:L]
    w_c = jnp.exp(a_last + m_prev - m_new)
    ws_row = jnp.exp(a_last - a_row + ig - m_new)
    ws_col = _row_to_col(ws_row, L)
    kw = k * ws_col
    c_new = w_c * c_prev + _dot(kw.T, v)
    n_new = w_c * n_prev + jnp.sum(_rnd(ws_col) * _rnd(k), axis=0, keepdims=True)
    return h, c_new, n_new, m_new


def _mlstm_kernel(q_ref, k_ref, v_ref, o_ref, ig_ref, fg_ref, big_ref, bfg_ref, g_ref, c0_ref, n0_ref, m0_ref,
                  h_ref, c_ref, n_ref, m_ref, c_s, n_s, m_s, *, valid_len, scale):
    ci = pl.program_id(1)

    @pl.when(ci == 0)
    def _():
        c_s[...] = c0_ref[...]
        n_s[...] = n0_ref[...]
        m_s[...] = m0_ref[...]

    nh, dh, _ = c_s.shape
    for hh in range(nh):
        cols = slice(hh * dh, (hh + 1) * dh)
        h, c_new, n_new, m_new = _mlstm_head(
            q_ref[:, cols], k_ref[:, cols] * scale, v_ref[:, cols], o_ref[:, cols],
            ig_ref[hh] + big_ref[hh], fg_ref[hh] + bfg_ref[hh], g_ref[hh], c_s[hh], n_s[hh], m_s[hh], valid_len)
        h_ref[:, cols] = h.astype(h_ref.dtype)
        c_s[hh] = c_new
        n_s[hh] = n_new
        m_s[hh] = m_new

    @pl.when(ci == pl.num_programs(1) - 1)
    def _():
        c_ref[...] = c_s[...]
        n_ref[...] = n_s[...]
        m_ref[...] = m_s[...]


def _mlstm(z, gates_t, b_ig, b_fg, g_mh, c0, n0, m0, valid_len):
    bsz, t, _ = z.shape
    nh, dh = g_mh.shape
    L = CHUNK
    nc = t // L
    w = nh * dh
    assert dh == LANES and t % L == 0
    col = lambda part: pl.BlockSpec((None, L, w), lambda b, c: (b, c, part))
    gate = lambda part: pl.BlockSpec((None, nh, 1, L), lambda b, c: (b, part, 0, c))
    per_head = lambda shp: pl.BlockSpec((nh,) + shp, lambda b, c: (0, 0, 0))
    state = lambda shp: pl.BlockSpec((None, nh) + shp, lambda b, c: (b, 0, 0, 0))
    return pl.pallas_call(
        functools.partial(_mlstm_kernel, valid_len=valid_len, scale=dh ** -0.5),
        grid=(bsz, nc),
        in_specs=[col(0), col(1), col(2), col(3), gate(0), gate(1),
                  per_head((1, 1)), per_head((1, 1)), per_head((1, dh)),
                  state((dh, dh)), state((1, dh)), state((1, 1))],
        out_specs=[pl.BlockSpec((None, L, w), lambda b, c: (b, c, 0)),
                   state((dh, dh)), state((1, dh)), state((1, 1))],
        out_shape=[jax.ShapeDtypeStruct((bsz, t, w), MXU_DTYPE),
                   jax.ShapeDtypeStruct((bsz, nh, dh, dh), F32),
                   jax.ShapeDtypeStruct((bsz, nh, 1, dh), F32),
                   jax.ShapeDtypeStruct((bsz, nh, 1, 1), F32)],
        scratch_shapes=[pltpu.VMEM((nh, dh, dh), F32), pltpu.VMEM((nh, 1, dh), F32), pltpu.VMEM((nh, 1, 1), F32)],
        compiler_params=_params(("parallel", "arbitrary")),
        name="mlstm",
    )(z, z, z, z, gates_t, gates_t, b_ig.reshape(nh, 1, 1), b_fg.reshape(nh, 1, 1), g_mh.reshape(nh, 1, dh),
      c0, n0.reshape(bsz, nh, 1, dh), m0.reshape(bsz, nh, 1, 1))


def _ret_kernel(q_ref, k_ref, v_ref, gd_ref, lg_ref, g_ref, s0_ref, o_ref, s_ref, s_s, *, true_len, scale):
    ci = pl.program_id(1)

    @pl.when(ci == 0)
    def _():
        s_s[...] = s0_ref[...]

    L = q_ref.shape[0]
    nh, dk, dv = s_s.shape
    t_col = _iota((L, 1), 0).astype(F32)
    diff = (_iota((L, L), 0) - _iota((L, L), 1)).astype(F32)
    causal = diff >= 0.0
    chunk_len = float(min(L, true_len))
    for hh in range(nh):
        q = q_ref[:, hh * dk:(hh + 1) * dk]
        k = k_ref[:, hh * dk:(hh + 1) * dk] * scale
        v = v_ref[:, hh * dv:(hh + 1) * dv]
        lg = lg_ref[hh]
        decay = jnp.where(causal, jnp.exp(jnp.where(causal, diff, 0.0) * lg), 0.0)
        s_prev = s_s[hh]
        inner = _dot_nt(q, k) * decay
        o = _dot(inner, v) + jnp.exp((t_col + 1.0) * lg) * _dot(q, s_prev)
        ws_col = jnp.where(t_col < chunk_len, jnp.exp((chunk_len - 1.0 - t_col) * lg), 0.0)
        s_s[hh] = jnp.exp(chunk_len * lg) * s_prev + _dot((k * ws_col).T, v)
        gd = gd_ref[:, hh * dv:(hh + 1) * dv]
        o_ref[:, hh * dv:(hh + 1) * dv] = (_head_norm(o, g_ref[hh]) * (gd * _sigmoid(gd))).astype(o_ref.dtype)

    @pl.when(ci == pl.num_programs(1) - 1)
    def _():
        s_ref[...] = s_s[...]


def _retention(qk, vg, log_gamma, g_ret, s0, true_len):
    bsz, t, _ = qk.shape
    nh, dv = g_ret.shape
    dk = s0.shape[2]
    L = CHUNK
    nc = t // L
    assert dk % LANES == 0 and dv % LANES == 0
    half = lambda wdt, part: pl.BlockSpec((None, L, nh * wdt), lambda b, c: (b, c, part))
    return pl.pallas_call(
        functools.partial(_ret_kernel, true_len=true_len, scale=dk ** -0.5),
        grid=(bsz, nc),
        in_specs=[half(dk, 0), half(dk, 1), half(dv, 0), half(dv, 1),
                  pl.BlockSpec((nh, 1, 1), lambda b, c: (0, 0, 0)),
                  pl.BlockSpec((nh, 1, dv), lambda b, c: (0, 0, 0)),
                  pl.BlockSpec((None, nh, dk, dv), lambda b, c: (b, 0, 0, 0))],
        out_specs=[pl.BlockSpec((None, L, nh * dv), lambda b, c: (b, c, 0)),
                   pl.BlockSpec((None, nh, dk, dv), lambda b, c: (b, 0, 0, 0))],
        out_shape=[jax.ShapeDtypeStruct((bsz, t, nh * dv), MXU_DTYPE),
                   jax.ShapeDtypeStruct((bsz, nh, dk, dv), F32)],
        scratch_shapes=[pltpu.VMEM((nh, dk, dv), F32)],
        compiler_params=_params(("parallel", "arbitrary")),
        name="retention",
    )(qk, qk, vg, vg, log_gamma.reshape(nh, 1, 1), g_ret.reshape(nh, 1, dv), s0)


def _lambda(lq1, lk1, lq2, lk2, lam_init):
    s1 = jnp.sum(lq1 * lk1, axis=1, keepdims=True)
    s2 = jnp.sum(lq2 * lk2, axis=1, keepdims=True)
    return jnp.exp(s1) - jnp.exp(s2) + lam_init


def _rms_sub(o, g, lam_init):
    return o * lax.rsqrt(jnp.mean(o * o, axis=-1, keepdims=True) + EPS) * g * (1.0 - lam_init)


def _diff_kernel(qi_ref, kj_ref, q_ref, k_ref, v_ref, lq1_ref, lk1_ref, lq2_ref, lk2_ref, g_ref, o_ref,
                 m_s, l_s, acc_s, *, lam_init, scale):
    qi = qi_ref[pl.program_id(2)]
    kj = kj_ref[pl.program_id(2)]
    tq, dv = q_ref.shape
    tk = k_ref.shape[0]
    dh = dv // 2

    @pl.when(kj == 0)
    def _():
        m_s[...] = jnp.full(m_s.shape, NEG_INF, F32)
        l_s[...] = jnp.zeros(l_s.shape, F32)
        acc_s[...] = jnp.zeros(acc_s.shape, F32)

    qpos = qi * tq + _iota((tq, tk), 0)
    kpos = kj * tk + _iota((tq, tk), 1)
    ok = kpos <= qpos
    vb = _mx(v_ref[...])
    for c in range(2):
        s = _dot_nt(q_ref[:, c * dh:(c + 1) * dh], k_ref[:, c * dh:(c + 1) * dh]) * scale
        s = jnp.where(ok, s, NEG_INF)
        m_old = m_s[c]
        m_new = jnp.maximum(m_old, _row_max(s))
        alpha = jnp.exp(m_old - m_new)
        p = jnp.exp(s - m_new)
        l_s[c] = alpha * l_s[c] + _row_sum(p)
        acc_s[c] = alpha * acc_s[c] + jnp.dot(_mx(p), vb, preferred_element_type=F32)
        m_s[c] = m_new

    @pl.when(kj == qi)
    def _():
        lam = _lambda(lq1_ref[...], lk1_ref[...], lq2_ref[...], lk2_ref[...], lam_init)
        o = acc_s[0] / l_s[0] - lam * (acc_s[1] / l_s[1])
        o_ref[...] = _rms_sub(o, g_ref[...], lam_init).astype(o_ref.dtype)


def _diff_attn(qb, kb, vb, lams, g_sub, lam_init, nh):
    bsz, t, w = qb.shape
    dv = w // nh
    dh = dv // 2
    tq = tk = min(t, 1024)
    nq = t // tq
    pairs = [(i, j) for i in range(nq) for j in range(i + 1)]
    qi_tab = jnp.asarray([p[0] for p in pairs], I32)
    kj_tab = jnp.asarray([p[1] for p in pairs], I32)
    vec = pl.BlockSpec((1, dh), lambda b, h, p, qt, kt: (0, 0))
    qo = pl.BlockSpec((None, tq, dv), lambda b, h, p, qt, kt: (b, qt[p], h))
    kv = pl.BlockSpec((None, tk, dv), lambda b, h, p, qt, kt: (b, kt[p], h))
    return pl.pallas_call(
        functools.partial(_diff_kernel, lam_init=lam_init, scale=dh ** -0.5),
        grid_spec=pltpu.PrefetchScalarGridSpec(
            num_scalar_prefetch=2, grid=(bsz, nh, len(pairs)),
            in_specs=[qo, kv, kv, vec, vec, vec, vec, pl.BlockSpec((1, dv), lambda b, h, p, qt, kt: (0, 0))],
            out_specs=qo,
            scratch_shapes=[pltpu.VMEM((2, tq, 1), F32), pltpu.VMEM((2, tq, 1), F32), pltpu.VMEM((2, tq, dv), F32)]),
        out_shape=jax.ShapeDtypeStruct((bsz, t, w), MXU_DTYPE),
        compiler_params=_params(("parallel", "parallel", "arbitrary")),
        name="diff_attn",
    )(qi_tab, kj_tab, qb, kb, vb, *[x.reshape(1, dh) for x in lams], g_sub.reshape(1, dv))


def _f32_key(x):
    i = lax.bitcast_convert_type(x, I32)
    return i ^ ((i >> 31) & 0x7FFFFFFF)


KEY_NEG_INF = -2139095041
I32_MIN = -2 ** 31


def _strict_upper(n):
    return jnp.where(_iota((n, n), 0) < _iota((n, n), 1), 1.0, 0.0).astype(MXU_DTYPE)


def _dsa_kernel(qc_ref, kc_ref, vc_ref, qi_ref, ki_ref, wi_ref, o_ref, key_s, m_s, l_s, acc_s,
                *, topk, n_heads, n_idx_heads, scale, wi_scale):
    qb = pl.program_id(1)
    tq = qc_ref.shape[0]
    dh = qc_ref.shape[1] // n_heads
    ck = key_s.shape[2]
    d_i = ki_ref.shape[1]
    q0 = qb * tq
    n_chunks = (q0 + tq + ck - 1) // ck

    def _select():
        qi = qi_ref[...]
        wi = _rnd(wi_ref[...] * wi_scale)
        qpos = q0 + _iota((tq, ck), 0)

        def score_chunk(c, carry):
            start = pl.multiple_of(c * ck, ck)
            kic = ki_ref[pl.ds(start, ck), :]
            acc = jnp.zeros((tq, ck), F32)
            for hh in range(n_idx_heads):
                sc = jnp.maximum(_dot_nt(qi[:, hh * d_i:(hh + 1) * d_i], kic), 0.0)
                acc = acc + wi[:, hh:hh + 1] * sc
            acc = acc + 0.0
            kpos = c * ck + _iota((tq, ck), 1)
            key_s[c] = jnp.where(kpos <= qpos, _f32_key(acc), KEY_NEG_INF)
            return carry

        lax.fori_loop(0, n_chunks, score_chunk, 0)

        def count(pred):
            def body(c, acc):
                hit = jnp.where(pred(key_s[c]), 1.0, 0.0)
                part = hit[:, 0:LANES]
                for j in range(1, ck // LANES):
                    part = part + hit[:, j * LANES:(j + 1) * LANES]
                return acc + part
            acc = lax.fori_loop(0, n_chunks, body, jnp.zeros((tq, LANES), F32))
            return jnp.sum(acc, axis=1, keepdims=True)

        def bit_step(it, prefix):
            bit = jnp.left_shift(jnp.int32(1), 31 - it)
            cand = (prefix | bit) ^ I32_MIN
            cnt = count(lambda kk: kk >= cand)
            return jnp.where(cnt >= float(topk), prefix | bit, prefix)

        prefix = lax.fori_loop(0, 32, bit_step, jnp.zeros((tq, 1), I32))
        thr = prefix ^ I32_MIN
        need = float(topk) - count(lambda kk: kk > thr)
        upper = _strict_upper(LANES)

        def mask_chunk(c, taken):
            kk = key_s[c]
            cols = []
            for j in range(ck // LANES):
                kj = kk[:, j * LANES:(j + 1) * LANES]
                eq = kj == thr
                eqf = jnp.where(eq, 1.0, 0.0)
                before = taken + jnp.dot(_mx(eqf), upper, preferred_element_type=F32)
                sel = ((kj > thr) | (eq & (before < need))) & (kj > KEY_NEG_INF)
                cols.append(jnp.where(sel, 0.0, NEG_INF))
                taken = taken + jnp.sum(eqf, axis=1, keepdims=True)
            key_s[c] = lax.bitcast_convert_type(jnp.concatenate(cols, axis=1), I32)
            return taken

        lax.fori_loop(0, n_chunks, mask_chunk, jnp.zeros((tq, 1), F32))

    _select()

    m_s[...] = jnp.full(m_s.shape, NEG_INF, F32)
    l_s[...] = jnp.zeros(l_s.shape, F32)
    acc_s[...] = jnp.zeros(acc_s.shape, F32)

    def attend(c, carry):
        start = pl.multiple_of(c * ck, ck)
        bias = lax.bitcast_convert_type(key_s[c], F32)
        for h in range(n_heads):
            cols = slice(h * dh, (h + 1) * dh)
            s = _dot_nt(qc_ref[:, cols], kc_ref[pl.ds(start, ck), cols]) * scale + bias
            m_old = m_s[h]
            m_new = jnp.maximum(m_old, _row_max(s))
            m_safe = jnp.where(m_new == NEG_INF, 0.0, m_new)
            alpha = jnp.exp(m_old - m_safe)
            p = jnp.exp(s - m_safe)
            l_s[h] = alpha * l_s[h] + _row_sum(p)
            acc_s[h] = alpha * acc_s[h] + _dot(p, vc_ref[pl.ds(start, ck), cols])
            m_s[h] = m_new
        return carry

    lax.fori_loop(0, n_chunks, attend, 0)
    for h in range(n_heads):
        o_ref[:, h * dh:(h + 1) * dh] = (acc_s[h] / l_s[h]).astype(o_ref.dtype)


def _dsa_attn(qc, kc, vc, qi, ki, wi, nh, n_idx_heads):
    bsz, t, w = qc.shape
    dh = w // nh
    d_i = ki.shape[2]
    tq = min(t, 256)
    ck = min(t, 512)
    topk = min(TOPK_MAX, t // 4)
    rows = lambda wdt: pl.BlockSpec((None, tq, wdt), lambda b, i: (b, i, 0))
    whole = lambda wdt: pl.BlockSpec((None, t, wdt), lambda b, i: (b, 0, 0), pipeline_mode=pl.Buffered(1))
    return pl.pallas_call(
        functools.partial(_dsa_kernel, topk=topk, n_heads=nh, n_idx_heads=n_idx_heads, scale=dh ** -0.5,
                          wi_scale=(n_idx_heads * d_i) ** -0.5),
        grid=(bsz, t // tq),
        in_specs=[rows(w), whole(w), whole(w), rows(qi.shape[2]), whole(d_i), rows(wi.shape[2])],
        out_specs=rows(w),
        out_shape=jax.ShapeDtypeStruct((bsz, t, w), MXU_DTYPE),
        scratch_shapes=[pltpu.VMEM((t // ck, tq, ck), I32),
                        pltpu.VMEM((nh, tq, 1), F32), pltpu.VMEM((nh, tq, 1), F32), pltpu.VMEM((nh, tq, dh), F32)],
        compiler_params=_params(("parallel", "arbitrary")),
        name="dsa_attn",
    )(qc, kc, vc, qi, ki, wi)


def _page_copy(cache_hbm, layer, phys, buf, slot, i, sem):
    return pltpu.make_async_copy(cache_hbm.at[layer, phys], buf.at[slot, i], sem.at[slot])


def _paged_step(cache_hbm, layer, pt_ref, buf, sem):
    g = buf.shape[1]
    per_b = pl.num_programs(1)
    step = pl.program_id(0) * per_b + pl.program_id(1)

    def fetch(t):
        b = t // per_b
        s = t % per_b
        for i in range(g):
            _page_copy(cache_hbm, layer, pt_ref[b, s * g + i], buf, t % 2, i, sem).start()

    @pl.when(step == 0)
    def _():
        fetch(step)

    @pl.when(step + 1 < pl.num_programs(0) * per_b)
    def _():
        fetch(step + 1)

    slot = step % 2
    for i in range(g):
        _page_copy(cache_hbm, layer, 0, buf, slot, i, sem).wait()
    return slot


def _paged_call(body, name, page_table, cache, g, in_arrays, in_specs, out_spec, out_shape):
    bd, n_pages = page_table.shape
    assert n_pages % g == 0
    return pl.pallas_call(
        body,
        grid_spec=pltpu.PrefetchScalarGridSpec(
            num_scalar_prefetch=1, grid=(bd, n_pages // g),
            in_specs=in_specs + [pl.BlockSpec(memory_space=pl.ANY)],
            out_specs=out_spec,
            scratch_shapes=[pltpu.VMEM((2, g) + cache.shape[2:], cache.dtype), pltpu.SemaphoreType.DMA((2,))]),
        out_shape=out_shape,
        compiler_params=_params(("arbitrary", "arbitrary")),
        name=name,
    )(page_table, *in_arrays, cache)


def _pages_per_step(n_pages, want):
    g = min(want, n_pages)
    while n_pages % g:
        g -= 1
    return g


def _key_rows(x, nh):
    bd = x.shape[0]
    return x.reshape(bd, nh, 2, LANES).transpose(0, 2, 1, 3).reshape(bd, 2 * nh, LANES)


def _diff_qk_kernel(pt_ref, q_ref, k_hbm, o_ref, buf, sem, *, layer):
    slot = _paged_step(k_hbm, layer, pt_ref, buf, sem)
    _, g, rows, _ = buf.shape
    for i in range(g):
        o_ref[:, i * rows:(i + 1) * rows] = _dot_nt(q_ref[...], buf[slot, i])


def _diff_pv_kernel(pt_ref, a_ref, v_hbm, o_ref, buf, sem, *, layer):
    slot = _paged_step(v_hbm, layer, pt_ref, buf, sem)
    _, g, rows, _ = buf.shape

    @pl.when(pl.program_id(1) == 0)
    def _():
        o_ref[...] = jnp.zeros(o_ref.shape, F32)

    acc = jnp.zeros(o_ref.shape, F32)
    for i in range(g):
        acc = acc + _dot(a_ref[:, i * rows:(i + 1) * rows], buf[slot, i])
    o_ref[...] += acc


def _diff_softmax_kernel(s_ref, q_ref, kn_ref, lq1_ref, lk1_ref, lq2_ref, lk2_ref, a_ref, an_ref,
                         *, lam_init, scale, nh):
    n_rows, width = s_ref.shape
    ch = min(width, 64 * LANES)
    real = (_iota((n_rows, ch), 1) % n_rows) == _iota((n_rows, ch), 0)
    chunks = [slice(i * ch, (i + 1) * ch) for i in range(width // ch)]

    def scores(sl):
        return jnp.where(real, s_ref[:, sl] * scale, NEG_INF)

    s_new = jnp.sum(_rnd(q_ref[...]) * _rnd(kn_ref[...]), axis=1, keepdims=True) * scale
    m = s_new
    for sl in chunks:
        m = jnp.maximum(m, jnp.max(scores(sl), axis=1, keepdims=True))
    e_new = jnp.exp(s_new - m)
    z = e_new
    for sl in chunks:
        z = z + jnp.sum(jnp.exp(scores(sl) - m), axis=1, keepdims=True)
    lam = _lambda(lq1_ref[...], lk1_ref[...], lq2_ref[...], lk2_ref[...], lam_init)
    p_new = e_new / z
    for sl in chunks:
        p = jnp.exp(scores(sl) - m) / z
        p0, p1 = p[:nh], p[nh:]
        a_ref[:, sl] = jnp.concatenate([p0 - lam * pltpu.roll(p1, ch - nh, 1),
                                        pltpu.roll(p0, nh, 1) - lam * p1], axis=0)
    a_new = p_new[:nh] - lam * p_new[nh:]
    an_ref[...] = jnp.broadcast_to(jnp.concatenate([a_new, a_new], axis=0), an_ref.shape)


def _diff_final_kernel(acc_ref, an_ref, vn_ref, g_ref, o_ref, *, lam_init):
    n_rows, wdt = acc_ref.shape
    nh = n_rows // 2
    full = acc_ref[...] + _rnd(an_ref[:, 0:1]) * _rnd(vn_ref[...])
    sq = jnp.sum(full * full, axis=1, keepdims=True)
    ms = (sq + jnp.concatenate([sq[nh:], sq[:nh]], axis=0)) / (2.0 * wdt)
    o_ref[...] = (full * lax.rsqrt(ms + EPS) * g_ref[...] * (1.0 - lam_init)).astype(o_ref.dtype)


def _diff_decode(qb, kb_new, vb_new, cache_k, cache_v, layer, page_table, lams, g_sub, lam_init, nh):
    bd, w = qb.shape
    dv = w // nh
    dh = dv // 2
    assert dh == LANES and 2 * nh == SUBLANES
    n_layers, n_pool, page = cache_k.shape[:3]
    n_pages = page_table.shape[1]
    rows = page * 2 * nh
    width = n_pages * rows
    g = _pages_per_step(n_pages, 8)

    def key_row_view(cache):
        c6 = cache.reshape(n_layers, n_pool, page, nh, 2, LANES)
        return c6.transpose(0, 1, 2, 4, 3, 5).reshape(n_layers, n_pool, rows, LANES)

    q8, k8, v8 = _key_rows(qb, nh), _key_rows(kb_new, nh), _key_rows(vb_new, nh)
    whole = lambda wdt: pl.BlockSpec((None, SUBLANES, wdt), lambda b, p, pt: (b, 0, 0))
    chunk = pl.BlockSpec((None, SUBLANES, g * rows), lambda b, p, pt: (b, 0, p))
    s = _paged_call(
        functools.partial(_diff_qk_kernel, layer=layer), "diff_decode_qk", page_table, key_row_view(cache_k), g,
        [q8], [whole(LANES)], chunk, jax.ShapeDtypeStruct((bd, SUBLANES, width), F32))
    per_b = lambda shp: pl.BlockSpec((None,) + shp, lambda b: (b, 0, 0))
    vec = pl.BlockSpec((1, dh), lambda b: (0, 0))
    a, a_new = pl.pallas_call(
        functools.partial(_diff_softmax_kernel, lam_init=lam_init, scale=dh ** -0.5, nh=nh),
        grid=(bd,),
        in_specs=[per_b((SUBLANES, width)), per_b((SUBLANES, LANES)), per_b((SUBLANES, LANES)), vec, vec, vec, vec],
        out_specs=[per_b((SUBLANES, width)), per_b((SUBLANES, LANES))],
        out_shape=[jax.ShapeDtypeStruct((bd, SUBLANES, width), F32), jax.ShapeDtypeStruct((bd, SUBLANES, LANES), F32)],
        compiler_params=_params(("parallel",)),
        name="diff_decode_softmax",
    )(s, q8, k8, *[x.reshape(1, dh) for x in lams])
    acc = _paged_call(
        functools.partial(_diff_pv_kernel, layer=layer), "diff_decode_pv", page_table, key_row_view(cache_v), g,
        [a], [chunk], whole(LANES), jax.ShapeDtypeStruct((bd, SUBLANES, LANES), F32))
    g8 = jnp.repeat(g_sub.reshape(2, 1, LANES), nh, axis=1).reshape(SUBLANES, LANES)
    out8 = pl.pallas_call(
        functools.partial(_diff_final_kernel, lam_init=lam_init),
        grid=(bd,),
        in_specs=[per_b((SUBLANES, LANES))] * 3 + [pl.BlockSpec((SUBLANES, LANES), lambda b: (0, 0))],
        out_specs=per_b((SUBLANES, LANES)),
        out_shape=jax.ShapeDtypeStruct((bd, SUBLANES, LANES), MXU_DTYPE),
        compiler_params=_params(("parallel",)),
        name="diff_decode_final",
    )(acc, a_new, v8, g8)
    return out8.reshape(bd, 2, nh, LANES).transpose(0, 2, 1, 3).reshape(bd, w)


def _idx_score_kernel(pt_ref, qi_ref, wi_ref, ki_hbm, o_ref, buf, sem, *, layer, wi_scale):
    slot = _paged_step(ki_hbm, layer, pt_ref, buf, sem)
    _, g, page, _ = buf.shape
    w = _rnd(wi_ref[...] * wi_scale)
    for i in range(g):
        sc = jnp.maximum(_dot_nt(qi_ref[...], buf[slot, i]), 0.0)
        o_ref[:, i * page:(i + 1) * page] = jnp.sum(w * _rnd(sc), axis=0, keepdims=True) + 0.0


def _select_kernel(sc_ref, qi_ref, wi_ref, kn_ref, idx_ref, nsel_ref, mnew_ref, pos_s, *, topk, wi_scale):
    npg, page = sc_ref.shape
    sc_new = jnp.maximum(jnp.sum(_rnd(qi_ref[...]) * _rnd(kn_ref[...]), axis=1, keepdims=True), 0.0)
    s_new = jnp.sum(_rnd(wi_ref[...] * wi_scale) * _rnd(sc_new), axis=0, keepdims=True) + 0.0
    keys = _f32_key(sc_ref[...])
    key_new = _f32_key(s_new)

    def total(x):
        return jnp.sum(jnp.sum(x, axis=1, keepdims=True), axis=0, keepdims=True)

    def count(pred):
        return total(jnp.where(pred(keys), 1.0, 0.0)) + jnp.where(pred(key_new), 1.0, 0.0)

    def bit_step(it, prefix):
        bit = jnp.left_shift(jnp.int32(1), 31 - it)
        cand = (prefix | bit) ^ I32_MIN
        return jnp.where(count(lambda kk: kk >= cand) >= float(topk), prefix | bit, prefix)

    thr = lax.fori_loop(0, 32, bit_step, jnp.zeros((1, 1), I32)) ^ I32_MIN
    need = float(topk) - count(lambda kk: kk > thr)
    strict_upper = _strict_upper(page)
    strict_lower = jnp.where(_iota((npg, npg), 1) < _iota((npg, npg), 0), 1.0, 0.0).astype(MXU_DTYPE)

    def count_before(flag):
        in_row = jnp.dot(_mx(flag), strict_upper, preferred_element_type=F32)
        row_tot = jnp.broadcast_to(jnp.sum(flag, axis=1, keepdims=True), (npg, page))
        return in_row + jnp.dot(strict_lower, _mx(row_tot), preferred_element_type=F32)

    eq = keys == thr
    eqf = jnp.where(eq, 1.0, 0.0)
    sel = ((keys > thr) | (eq & (count_before(eqf) < need))) & (keys > KEY_NEG_INF)
    sel_new = ((key_new > thr) | ((key_new == thr) & (total(eqf) < need))) & (key_new > KEY_NEG_INF)
    mnew_ref[...] = jnp.broadcast_to(jnp.where(sel_new, 1.0, 0.0), mnew_ref.shape)
    self = jnp.where(sel, 1.0, 0.0)
    pos_s[...] = jnp.where(sel, count_before(self), -1.0)
    n_slots = idx_ref.shape[0]
    slot_id = _iota((n_slots, page), 0).astype(F32)
    lane = _iota((n_slots, page), 1).astype(F32)

    def gather_page(p, acc):
        hit = pos_s[pl.ds(p, 1), :] == slot_id
        return acc + jnp.where(hit, jnp.asarray(p * page, F32) + lane, 0.0)

    acc = lax.fori_loop(0, npg, gather_page, jnp.zeros((n_slots, page), F32))
    idx_ref[...] = jnp.broadcast_to(jnp.sum(acc, axis=1, keepdims=True).astype(I32), idx_ref.shape)
    nsel_ref[...] = jnp.broadcast_to(total(self), nsel_ref.shape)


def _dsa_gather_kernel(idx_ref, pt_ref, q_ref, kn_ref, vn_ref, mnew_ref, nsel_ref, k_hbm, v_hbm, o_ref,
                       kbuf, vbuf, sem, *, layer, scale):
    b = pl.program_id(0)
    n_slots, nh, dh = kbuf.shape
    page = k_hbm.shape[2]

    def row(cache, i, buf, r, s):
        return pltpu.make_async_copy(cache.at[layer, pt_ref[b, i // page], pl.ds(i % page, 1)],
                                     buf.at[pl.ds(r, 1)], sem.at[s])

    def issue(r, carry):
        i = idx_ref[b, r]
        row(k_hbm, i, kbuf, r, 0).start()
        row(v_hbm, i, vbuf, r, 1).start()
        return carry

    def drain(r, carry):
        row(k_hbm, 0, kbuf, 0, 0).wait()
        row(v_hbm, 0, vbuf, 0, 1).wait()
        return carry

    lax.fori_loop(0, n_slots, issue, 0)
    lax.fori_loop(0, n_slots, drain, 0)
    q = q_ref[...]
    head = _iota((nh, n_slots), 0)
    s = jnp.zeros((nh, n_slots), F32)
    for h in range(nh):
        s = jnp.where(head == h, _dot_nt(q, kbuf[:, h, :]), s)
    live = _iota((nh, n_slots), 1).astype(F32) < nsel_ref[:, 0:1]
    s = jnp.where(live, s * scale, NEG_INF)
    s_new = jnp.sum(_rnd(q) * _rnd(kn_ref[...]), axis=1, keepdims=True) * scale
    s_new = jnp.where(mnew_ref[:, 0:1] > 0.0, s_new, NEG_INF)
    m = jnp.maximum(jnp.max(s, axis=1, keepdims=True), s_new)
    e = jnp.exp(s - m)
    e_new = jnp.exp(s_new - m)
    z = jnp.sum(e, axis=1, keepdims=True) + e_new
    p = e / z
    o = jnp.zeros((nh, dh), F32)
    head = _iota((nh, dh), 0)
    for h in range(nh):
        o = jnp.where(head == h, _dot(p, vbuf[:, h, :]), o)
    o = o + _rnd(e_new / z) * _rnd(vn_ref[...])
    o_ref[...] = o.astype(o_ref.dtype)


def _dsa_decode(qc, kc_new, vc_new, qi, ki_new, wi, cache_k, cache_v, cache_i, layer, page_table, nh, n_idx_heads):
    bd, w = qc.shape
    dh = w // nh
    d_i = ki_new.shape[1]
    page = cache_k.shape[2]
    n_pages = page_table.shape[1]
    past = n_pages * page
    topk = min(TOPK_MAX, (past + 1) // 4)
    n_slots = -(-topk // SUBLANES) * SUBLANES
    qi3 = qi.reshape(bd, n_idx_heads, d_i)
    wi3 = wi[:, :n_idx_heads].reshape(bd, n_idx_heads, 1)
    wi_scale = (n_idx_heads * d_i) ** -0.5
    g = _pages_per_step(n_pages, 16)
    scores = _paged_call(
        functools.partial(_idx_score_kernel, layer=layer, wi_scale=wi_scale), "idx_scores", page_table, cache_i, g,
        [qi3, wi3],
        [pl.BlockSpec((None, n_idx_heads, d_i), lambda b, p, pt: (b, 0, 0)),
         pl.BlockSpec((None, n_idx_heads, 1), lambda b, p, pt: (b, 0, 0))],
        pl.BlockSpec((None, 1, g * page), lambda b, p, pt: (b, 0, p)),
        jax.ShapeDtypeStruct((bd, 1, past), F32))
    per_b = lambda shp: pl.BlockSpec((None,) + shp, lambda b: (b, 0, 0))
    sel_idx, n_sel, mask_new = pl.pallas_call(
        functools.partial(_select_kernel, topk=topk, wi_scale=wi_scale),
        grid=(bd,),
        in_specs=[per_b((n_pages, page)), per_b((n_idx_heads, d_i)), per_b((n_idx_heads, 1)), per_b((1, d_i))],
        out_specs=[per_b((n_slots, LANES)), per_b((1, LANES)), per_b((1, LANES))],
        out_shape=[jax.ShapeDtypeStruct((bd, n_slots, LANES), I32), jax.ShapeDtypeStruct((bd, 1, LANES), F32),
                   jax.ShapeDtypeStruct((bd, 1, LANES), F32)],
        scratch_shapes=[pltpu.VMEM((n_pages, page), F32)],
        compiler_params=_params(("parallel",)),
        name="idx_select",
    )(scores.reshape(bd, n_pages, page), qi3, wi3, ki_new.reshape(bd, 1, d_i))
    per_b2 = lambda shp: pl.BlockSpec((None,) + shp, lambda b, idx, pt: (b, 0, 0))
    hbm = pl.BlockSpec(memory_space=pl.ANY)
    return pl.pallas_call(
        functools.partial(_dsa_gather_kernel, layer=layer, scale=dh ** -0.5),
        grid_spec=pltpu.PrefetchScalarGridSpec(
            num_scalar_prefetch=2, grid=(bd,),
            in_specs=[per_b2((nh, dh)), per_b2((nh, dh)), per_b2((nh, dh)), per_b2((1, LANES)), per_b2((1, LANES)),
                      hbm, hbm],
            out_specs=per_b2((nh, dh)),
            scratch_shapes=[pltpu.VMEM((n_slots, nh, dh), F32), pltpu.VMEM((n_slots, nh, dh), F32),
                            pltpu.SemaphoreType.DMA((2,))]),
        out_shape=jax.ShapeDtypeStruct((bd, nh, dh), MXU_DTYPE),
        compiler_params=_params(("arbitrary",)),
        name="dsa_decode_attn",
    )(sel_idx[:, :, 0], page_table, qc.reshape(bd, nh, dh), kc_new.reshape(bd, nh, dh), vc_new.reshape(bd, nh, dh),
      mask_new, n_sel, cache_k, cache_v).reshape(bd, w)


def _route_kernel(lg_ref, br_ref, e_ref, g_ref, r_ref, cnt_ref, carry_s, *, n_valid, n_groups):
    i = pl.program_id(0)
    ne, tn = lg_ref.shape
    per = ne // n_groups

    @pl.when(i == 0)
    def _():
        carry_s[...] = jnp.zeros(carry_s.shape, F32)

    aff = _sigmoid(lg_ref[...])
    sel = aff + br_ref[...]
    sub = _iota((per, tn), 0).astype(F32)
    best = None
    for gi in range(n_groups):
        s = sel[gi * per:(gi + 1) * per]
        m1 = jnp.max(s, axis=0, keepdims=True)
        i1 = jnp.min(jnp.where(s == m1, sub, float(per)), axis=0, keepdims=True)
        s2 = jnp.where(sub == i1, NEG_INF, s)
        m2 = jnp.max(s2, axis=0, keepdims=True)
        i2 = jnp.min(jnp.where(s2 == m2, sub, float(per)), axis=0, keepdims=True)
        cand = (m1 + m2, float(gi * per) + i1, float(gi * per) + i2)
        if best is None:
            best = cand
        else:
            better = cand[0] > best[0]
            best = tuple(jnp.where(better, c, b) for c, b in zip(cand, best))
    e0, e1 = best[1].astype(I32), best[2].astype(I32)
    eid = _iota((ne, tn), 0)
    valid = (i * tn + _iota((1, tn), 1)) < n_valid
    oh0 = (eid == e0) & valid
    oh1 = (eid == e1) & valid
    a0 = jnp.sum(jnp.where(eid == e0, aff, 0.0), axis=0, keepdims=True)
    a1 = jnp.sum(jnp.where(eid == e1, aff, 0.0), axis=0, keepdims=True)
    tot = a0 + a1
    e_ref[...] = jnp.concatenate([e0, e1], axis=0)
    g_ref[...] = jnp.concatenate([a0 / tot, a1 / tot], axis=0)
    oh = jnp.where(oh0, 1.0, 0.0) + jnp.where(oh1, 1.0, 0.0)
    before = carry_s[:, 0:1] + jnp.dot(_mx(oh), _strict_upper(tn), preferred_element_type=F32)
    r0 = jnp.sum(jnp.where(oh0, before, 0.0), axis=0, keepdims=True)
    r1 = jnp.sum(jnp.where(oh1, before, 0.0), axis=0, keepdims=True)
    r_ref[...] = jnp.concatenate([r0, r1], axis=0).astype(I32)
    carry_s[...] = carry_s[...] + jnp.sum(oh, axis=1, keepdims=True)
    cnt_ref[...] = carry_s[...]


def _route(logits_t, b_router, n_valid):
    ne, mp = logits_t.shape
    tn = min(mp, 256)
    tok = lambda i: (0, i)
    return pl.pallas_call(
        functools.partial(_route_kernel, n_valid=n_valid, n_groups=N_GROUPS),
        grid=(mp // tn,),
        in_specs=[pl.BlockSpec((ne, tn), tok), pl.BlockSpec((ne, 1), lambda i: (0, 0))],
        out_specs=[pl.BlockSpec((TOP_K, tn), tok), pl.BlockSpec((TOP_K, tn), tok), pl.BlockSpec((TOP_K, tn), tok),
                   pl.BlockSpec((ne, LANES), lambda i: (0, 0))],
        out_shape=[jax.ShapeDtypeStruct((TOP_K, mp), I32), jax.ShapeDtypeStruct((TOP_K, mp), F32),
                   jax.ShapeDtypeStruct((TOP_K, mp), I32), jax.ShapeDtypeStruct((ne, LANES), F32)],
        scratch_shapes=[pltpu.VMEM((ne, LANES), F32)],
        compiler_params=_params(("arbitrary",)),
        name="moe_route",
    )(logits_t, b_router.reshape(ne, 1))


def _row_copy(src, s_row, dst, d_row, sem):
    return pltpu.make_async_copy(src.at[pl.ds(s_row, 1)], dst.at[pl.ds(d_row, 1)], sem)


def _expert_kernel(be_ref, nu_ref, src_ref, x_hbm, wg_ref, wu_ref, wd_ref, o_ref, xbuf, wg_s, wu_s, wd_s, sem):
    b = pl.program_id(0)
    n_used = nu_ref[0]
    blk = xbuf.shape[1]

    def fetch(block):
        def issue(r, carry):
            _row_copy(x_hbm, src_ref[block * blk + r], xbuf.at[block % 2], r, sem.at[block % 2]).start()
            return carry
        lax.fori_loop(0, blk, issue, 0, unroll=8)

    @pl.when(b == 0)
    def _():
        fetch(b)

    @pl.when(b + 1 < n_used)
    def _():
        fetch(b + 1)

    @pl.when(b < n_used)
    def _():
        def drain(r, carry):
            _row_copy(x_hbm, 0, xbuf.at[b % 2], 0, sem.at[b % 2]).wait()
            return carry
        lax.fori_loop(0, blk, drain, 0, unroll=8)

        @pl.when((b == 0) | (be_ref[b] != be_ref[jnp.maximum(b - 1, 0)]))
        def _():
            wg_s[...] = _mx(wg_ref[...])
            wu_s[...] = _mx(wu_ref[...])
            wd_s[...] = _mx(wd_ref[...])

        x = _mx(xbuf[b % 2])
        gate = jnp.dot(x, wg_s[...], preferred_element_type=F32)
        up = jnp.dot(x, wu_s[...], preferred_element_type=F32)
        hdn = gate * _sigmoid(gate) * up
        y = jnp.dot(_mx(hdn), wd_s[...], preferred_element_type=F32)
        o_ref[...] = _rnd(y)

    @pl.when(b >= n_used)
    def _():
        o_ref[...] = jnp.zeros(o_ref.shape, F32)


def _experts(blk_e, n_used, src, x, w_gate, w_up, w_down, layer, blk):
    n_rows = src.shape[0]
    d = x.shape[1]
    de = w_gate.shape[3]
    wspec = lambda shp: pl.BlockSpec((None, None) + shp, lambda b, be, nu, sr: (layer, be[b], 0, 0))
    return pl.pallas_call(
        _expert_kernel,
        grid_spec=pltpu.PrefetchScalarGridSpec(
            num_scalar_prefetch=3, grid=(n_rows // blk,),
            in_specs=[pl.BlockSpec(memory_space=pl.ANY), wspec((d, de)), wspec((d, de)), wspec((de, d))],
            out_specs=pl.BlockSpec((blk, d), lambda b, be, nu, sr: (b, 0)),
            scratch_shapes=[pltpu.VMEM((2, blk, d), F32), pltpu.VMEM((d, de), MXU_DTYPE), pltpu.VMEM((d, de), MXU_DTYPE),
                            pltpu.VMEM((de, d), MXU_DTYPE), pltpu.SemaphoreType.DMA((2,))]),
        out_shape=jax.ShapeDtypeStruct((n_rows, d), F32),
        compiler_params=_params(("arbitrary",)),
        name="moe_experts",
    )(blk_e, n_used, src, x, w_gate, w_up, w_down)


def _combine_kernel(dest_ref, y_hbm, x_ref, gt_ref, g_ref, b_ref, xo_ref, xb_ref, rows_s, sem, *, n_tok, alpha):
    tc = x_ref.shape[0]
    base = pl.program_id(0) * tc

    def issue(r, carry):
        for kk in range(TOP_K):
            _row_copy(y_hbm, dest_ref[kk * n_tok + base + r], rows_s.at[kk], r, sem).start()
        return carry

    def drain(r, carry):
        for kk in range(TOP_K):
            _row_copy(y_hbm, 0, rows_s.at[kk], 0, sem).wait()
        return carry

    lax.fori_loop(0, tc, issue, 0, unroll=8)
    lax.fori_loop(0, tc, drain, 0, unroll=8)
    gt = gt_ref[...]
    y = _rnd(gt[:, 0:1]) * rows_s[0] + _rnd(gt[:, 1:2]) * rows_s[1]
    out = _layer_norm(alpha * x_ref[...] + y, g_ref[...], b_ref[...])
    xo_ref[...] = out
    xb_ref[...] = out.astype(xb_ref.dtype)


def _combine(dest_flat, y, x, gates, g, bt, alpha):
    n_tok, d = x.shape
    tc = min(n_tok, 128)
    row = lambda i, dst: (i, 0)
    fixed = lambda i, dst: (0, 0)
    return pl.pallas_call(
        functools.partial(_combine_kernel, n_tok=n_tok, alpha=alpha),
        grid_spec=pltpu.PrefetchScalarGridSpec(
            num_scalar_prefetch=1, grid=(n_tok // tc,),
            in_specs=[pl.BlockSpec(memory_space=pl.ANY), pl.BlockSpec((tc, d), row), pl.BlockSpec((tc, TOP_K), row),
                      pl.BlockSpec((1, d), fixed), pl.BlockSpec((1, d), fixed)],
            out_specs=[pl.BlockSpec((tc, d), row), pl.BlockSpec((tc, d), row)],
            scratch_shapes=[pltpu.VMEM((TOP_K, tc, d), F32), pltpu.SemaphoreType.DMA(())]),
        out_shape=[jax.ShapeDtypeStruct((n_tok, d), F32), jax.ShapeDtypeStruct((n_tok, d), MXU_DTYPE)],
        compiler_params=_params(("arbitrary",)),
        name="moe_combine",
    )(dest_flat, y, x, gates, g.reshape(1, d), bt.reshape(1, d))


def _moe_ln(x, xb, w_router_t, b_router, w_gate, w_up, w_down, layer, g, bt, alpha, n_valid, blk):
    m, d = x.shape
    ne = w_gate.shape[1]
    mp = max(m, LANES)
    logits_t = _mm_nt(w_router_t, xb)
    if mp != m:
        logits_t = jnp.pad(logits_t, ((0, 0), (0, mp - m)))
    eidx, gates, rank, counts = _route(logits_t, b_router, n_valid)
    counts = counts[:, 0].astype(I32)
    padded = (counts + blk - 1) // blk * blk
    pad_end = jnp.cumsum(padded)
    pad_start = pad_end - padded
    n_blocks = -(-(n_valid * TOP_K + ne * (blk - 1)) // blk)
    first_slot = jnp.sum(jnp.where(eidx[:, :m, None] == jnp.arange(ne), pad_start, 0), axis=-1)
    dest = (first_slot + rank[:, :m]).astype(I32)
    n_rows = n_blocks * blk
    live = jnp.arange(m)[None, :] < n_valid
    tok = jnp.broadcast_to(jnp.arange(m, dtype=I32)[None, :], (TOP_K, m))
    src = jnp.zeros((n_rows,), I32).at[jnp.where(live, dest, n_rows).reshape(-1)].set(tok.reshape(-1), mode="drop")
    dest_flat = jnp.where(live, dest, 0).reshape(-1)
    blk_e = jnp.minimum(jnp.searchsorted(pad_end, jnp.arange(n_blocks) * blk, side="right"), ne - 1).astype(I32)
    n_used = (pad_end[-1:] // blk).astype(I32)
    y = _experts(blk_e, n_used, src, x, w_gate, w_up, w_down, layer, blk)
    return _combine(dest_flat, y, x, gates[:, :m].T, g, bt, alpha)


def _rope_tables(pos, half):
    inv = ROPE_THETA ** (-jnp.arange(half, dtype=F32) / half)
    ang = pos.astype(F32)[:, None] * inv[None, :]
    c, s = jnp.cos(ang), jnp.sin(ang)
    reps = LANES // (2 * half)
    return (jnp.tile(jnp.concatenate([c, c], -1), (1, reps)), jnp.tile(jnp.concatenate([-s, s], -1), (1, reps)), half)


def _cols(w, start, width, pad_to=None):
    out = w[:, start:start + width].astype(MXU_DTYPE)
    if pad_to is not None and pad_to > width:
        out = jnp.pad(out, ((0, 0), (0, pad_to - width)))
    return out


def kernel(x_prompt, x_sample, state_mlstm_c, state_mlstm_n, state_mlstm_m, cache_diff_k, cache_diff_v, cache_dsa_k, cache_dsa_v, cache_idx_k, state_ret, page_table, w_in_even, w_out_even, b_igate, b_fgate, g_mlstm, lam_q1, lam_k1, lam_q2, lam_k2, g_subln, w_in_odd, w_out_odd, g_ret, ln_mix_g, ln_mix_b, ln_ffn_g, ln_ffn_b, w_router, b_router, w_gate, w_up, w_down):
    bsz, seq, d_model = x_prompt.shape
    dec_b, dec_seq, _ = x_sample.shape
    assert dec_seq == 1
    depth = w_gate.shape[0]
    alpha = (2 * depth) ** 0.25
    h_a, dh_a = g_mlstm.shape[1:]
    h_b, dv_b = cache_diff_k.shape[3:]
    h_c, dh_c = cache_dsa_k.shape[3:]
    d_i = cache_idx_k.shape[3]
    h_d, dk_d, dv_d = state_ret.shape[2:]
    w_a, w_b, w_c, w_d = h_a * dh_a, h_b * dv_b, h_c * dh_c, h_d * dv_d
    h_i = w_in_odd.shape[2] - (3 * w_c + d_i + 2 * h_d * dk_d + 2 * w_d)
    h_i = h_i // (d_i + 1)
    past_len = page_table.shape[1] * cache_diff_k.shape[2]
    m_p = bsz * seq
    m_s = 2 * SUBLANES

    pos_p = jnp.arange(seq)
    pos_s = jnp.full((m_s,), past_len)
    rope_p = {h: _rope_tables(pos_p, h) for h in (dh_c // 2, d_i // 2)}
    rope_s = {h: _rope_tables(pos_s, h) for h in (dh_c // 2, d_i // 2)}
    log_gamma = jnp.log(1.0 - 2.0 ** (-5.0 - jnp.arange(h_d, dtype=F32)))
    w_router_t = w_router.T.astype(MXU_DTYPE)

    xp = x_prompt.reshape(m_p, d_model)
    xs = jnp.pad(x_sample.reshape(dec_b, d_model), ((0, m_s - dec_b), (0, 0)))
    xp_b, xs_b = xp.astype(MXU_DTYPE), xs.astype(MXU_DTYPE)
    names = ("mlstm_c", "mlstm_n", "mlstm_m", "diff_k", "diff_v", "dsa_k", "dsa_v", "idx_k", "ret")
    new_p = {k: [] for k in names}
    new_s = {k: [] for k in names}

    def pad_tokens(a):
        return jnp.pad(a[:dec_b, None, :], ((0, 0), (0, CHUNK - 1), (0, 0)))

    for l in range(depth):
        j = l // 2
        if l % 2 == 0:
            w = w_in_even[j]
            o_g = 4 * w_a
            o_b = o_g + 2 * h_a
            w_main = _cols(w, 0, 4 * w_a)
            w_gates_t = w[:, o_g:o_b].T.astype(MXU_DTYPE)
            w_qb, w_kb, w_vb = (_cols(w, o_b + i * w_b, w_b) for i in range(3))
            lams = (lam_q1[j], lam_k1[j], lam_q2[j], lam_k2[j])
            lam_init = 0.8 - 0.6 * math.exp(-0.3 * l)
            half = dv_b // 4
            z = _mm(xp_b, w_main).reshape(bsz, seq, 4 * w_a)
            gt = _mm_nt(w_gates_t, xp_b).reshape(2 * h_a, bsz, seq).transpose(1, 0, 2).reshape(bsz, 2 * h_a, 1, seq)
            r3 = lambda a: a.reshape(bsz, seq, a.shape[1])
            qb_m = _mm(xp_b, w_qb, rope=rope_p[half], out_dtype=MXU_DTYPE)
            kb, kb_m = _mm(xp_b, w_kb, rope=rope_p[half], mxu_copy=True)
            vb, vb_m = _mm(xp_b, w_vb, mxu_copy=True)
            zero = lambda *s: jnp.zeros(s, F32)
            h_mix, c_p, n_p, mm_p = _mlstm(z, gt, b_igate[j], b_fgate[j], g_mlstm[j], zero(bsz, h_a, dh_a, dh_a),
                                           zero(bsz, h_a, dh_a), zero(bsz, h_a), CHUNK)
            ob = _diff_attn(r3(qb_m), r3(kb_m), r3(vb_m), lams, g_subln[j], lam_init, h_b)
            mix_a_p, mix_b_p = h_mix.reshape(m_p, w_a), ob.reshape(m_p, w_b)
            st_p = (c_p, n_p.reshape(bsz, h_a, dh_a), mm_p.reshape(bsz, h_a),
                    kb.reshape(bsz, seq, h_b, dv_b), vb.reshape(bsz, seq, h_b, dv_b))
            zs = pad_tokens(_mm(xs_b, w_main))
            gts = _mm_nt(w_gates_t, xs_b)[:, :dec_b].T
            gts = jnp.pad(gts[:, :, None, None], ((0, 0), (0, 0), (0, 0), (0, CHUNK - 1)))
            qbs = _mm(xs_b, w_qb, rope=rope_s[half])[:dec_b]
            kbs = _mm(xs_b, w_kb, rope=rope_s[half])[:dec_b]
            vbs = _mm(xs_b, w_vb)[:dec_b]
            hs_mix, c_s, n_s, mm_s = _mlstm(zs, gts, b_igate[j], b_fgate[j], g_mlstm[j], state_mlstm_c[j],
                                            state_mlstm_n[j], state_mlstm_m[j], 1)
            obs = _diff_decode(qbs, kbs, vbs, cache_diff_k, cache_diff_v, j, page_table, lams, g_subln[j], lam_init, h_b)
            pad_rows = lambda a: jnp.pad(a, ((0, m_s - dec_b), (0, 0)))
            mix_a_s, mix_b_s = pad_rows(hs_mix[:, 0, :]), pad_rows(obs)
            st_s = (c_s, n_s.reshape(dec_b, h_a, dh_a), mm_s.reshape(dec_b, h_a),
                    kbs.reshape(dec_b, 1, h_b, dv_b), vbs.reshape(dec_b, 1, h_b, dv_b))
            w_out = w_out_even[j].astype(MXU_DTYPE)
            keys = names[:5]
        else:
            w = w_in_odd[j]
            o_qi = 3 * w_c
            o_ki = o_qi + h_i * d_i
            o_wi = o_ki + d_i
            o_qd = o_wi + h_i
            o_vd = o_qd + 2 * h_d * dk_d
            w_qc, w_kc, w_vc = (_cols(w, i * w_c, w_c) for i in range(3))
            w_qi = _cols(w, o_qi, h_i * d_i)
            w_ki = _cols(w, o_ki, d_i, pad_to=LANES)
            w_wi = _cols(w, o_wi, h_i, pad_to=LANES)
            w_qkd = _cols(w, o_qd, 2 * h_d * dk_d)
            w_vgd = _cols(w, o_vd, 2 * w_d)
            hc, hi, hd = dh_c // 2, d_i // 2, dk_d // 2

            def project(xb_, rope, q_dtype):
                qc = _mm(xb_, w_qc, rope=rope[hc], out_dtype=q_dtype)
                kc, kc_m = _mm(xb_, w_kc, rope=rope[hc], mxu_copy=True)
                vc, vc_m = _mm(xb_, w_vc, mxu_copy=True)
                qi = _mm(xb_, w_qi, rope=rope[hi], out_dtype=q_dtype)
                ki = _mm(xb_, w_ki, rope=rope[hi], n_out=d_i)
                wi = _mm(xb_, w_wi)
                qkd = _mm(xb_, w_qkd, rope=rope[hd])
                vgd = _mm(xb_, w_vgd)
                return (qc, kc, vc, qi, ki, wi, qkd, vgd), (kc_m, vc_m)

            (qc, kc, vc, qi, ki, wi, qkd, vgd), (kc_m, vc_m) = project(xp_b, rope_p, MXU_DTYPE)
            r3 = lambda a: a.reshape(bsz, seq, a.shape[1])
            oc = _dsa_attn(r3(qc), r3(kc_m), r3(vc_m), r3(qi), r3(ki), r3(wi), h_c, h_i)
            od, s_p = _retention(r3(qkd), r3(vgd), log_gamma, g_ret[j], jnp.zeros((bsz, h_d, dk_d, dv_d), F32), CHUNK)
            mix_a_p, mix_b_p = oc.reshape(m_p, w_c), od.reshape(m_p, w_d)
            st_p = (kc.reshape(bsz, seq, h_c, dh_c), vc.reshape(bsz, seq, h_c, dh_c), ki.reshape(bsz, seq, d_i), s_p)
            qc, kc, vc, qi, ki, wi, qkd, vgd = (a[:dec_b] for a in project(xs_b, rope_s, F32)[0])
            ocs = _dsa_decode(qc, kc, vc, qi, ki, wi, cache_dsa_k, cache_dsa_v, cache_idx_k, j, page_table, h_c, h_i)
            ods, s_s = _retention(pad_tokens(qkd), pad_tokens(vgd), log_gamma, g_ret[j], state_ret[j], 1)
            pad_rows = lambda a: jnp.pad(a, ((0, m_s - dec_b), (0, 0)))
            mix_a_s, mix_b_s = pad_rows(ocs), pad_rows(ods[:, 0, :])
            st_s = (kc.reshape(dec_b, 1, h_c, dh_c), vc.reshape(dec_b, 1, h_c, dh_c), ki.reshape(dec_b, 1, d_i), s_s)
            w_out = w_out_odd[j].astype(MXU_DTYPE)
            keys = names[5:]
        for k, a, b in zip(keys, st_p, st_s):
            new_p[k].append(a)
            new_s[k].append(b)
        xp, xp_b = _proj_ln(mix_a_p, mix_b_p, w_out, xp, ln_mix_g[l], ln_mix_b[l], alpha)
        xs, xs_b = _proj_ln(mix_a_s, mix_b_s, w_out, xs, ln_mix_g[l], ln_mix_b[l], alpha)
        moe_w = (w_router_t, b_router, w_gate, w_up, w_down, l, ln_ffn_g[l], ln_ffn_b[l], alpha)
        xp, xp_b = _moe_ln(xp, xp_b, *moe_w, m_p, 256)
        xs, xs_b = _moe_ln(xs, xs_b, *moe_w, dec_b, 16)

    out = [xp.reshape(bsz, seq, d_model), xs[:dec_b].reshape(dec_b, 1, d_model)]
    for k in names:
        out += [jnp.stack(new_p[k]), jnp.stack(new_s[k])]
    return tuple(out)
```

```python
import functools
import math

import jax
import jax.numpy as jnp
from jax import lax
from jax.experimental import pallas as pl
from jax.experimental.pallas import tpu as pltpu

F32 = jnp.float32
I32 = jnp.int32
MXU_DTYPE = jnp.bfloat16

N_GROUPS = 4
TOP_K = 2
TOPK_MAX = 256
CHUNK = 128
ROPE_THETA = 10000.0
EPS = 1e-5

LANES = 128
SUBLANES = 8
VMEM_LIMIT_BYTES = 52 * 1024 * 1024

NEG_INF = float("-inf")


def _mx(x):
    return x.astype(MXU_DTYPE)


def _rnd(x):
    return x.astype(MXU_DTYPE).astype(F32)


def _dot(a, b):
    return jnp.dot(_mx(a), _mx(b), preferred_element_type=F32)


def _dot_nt(a, b):
    return lax.dot_general(_mx(a), _mx(b), (((1,), (1,)), ((), ())), preferred_element_type=F32)


def _params(sem, vmem=VMEM_LIMIT_BYTES):
    return pltpu.CompilerParams(dimension_semantics=sem, vmem_limit_bytes=vmem)


def _iota(shape, axis):
    return lax.broadcasted_iota(I32, shape, axis)


def _sigmoid(x):
    return 1.0 / (1.0 + jnp.exp(-x))


def _lane_fold(x, op):
    acc = x[:, 0:LANES]
    for j in range(1, x.shape[1] // LANES):
        acc = op(acc, x[:, j * LANES:(j + 1) * LANES])
    return acc


def _row_max(x):
    return jnp.max(_lane_fold(x, jnp.maximum), axis=1, keepdims=True)


def _row_sum(x):
    return jnp.sum(_lane_fold(x, jnp.add), axis=1, keepdims=True)


def _row_to_col(row, n):
    eye = _iota((n, n), 0) == _iota((n, n), 1)
    return jnp.sum(jnp.where(eye, jnp.broadcast_to(row, (n, n)), 0.0), axis=1, keepdims=True)


def _head_norm(x, g):
    mu = jnp.mean(x, axis=-1, keepdims=True)
    xc = x - mu
    var = jnp.mean(xc * xc, axis=-1, keepdims=True)
    return xc * lax.rsqrt(var + EPS) * g


def _layer_norm(z, g, b):
    mu = jnp.mean(z, axis=-1, keepdims=True)
    zc = z - mu
    var = jnp.mean(zc * zc, axis=-1, keepdims=True)
    return zc * lax.rsqrt(var + EPS) * g + b


def _rope_apply(a, cos, sin, half):
    if 2 * half == LANES:
        r = pltpu.roll(a, half, 1)
    else:
        lane = _iota(a.shape, 1)
        r = jnp.where((lane % (2 * half)) < half, pltpu.roll(a, LANES - half, 1), pltpu.roll(a, half, 1))
    return a * cos + r * sin


def _mm_kernel(x_ref, w_ref, *rest, rope_half, n_outs):
    o_refs = rest[len(rest) - n_outs:]
    acc = jnp.dot(x_ref[...], w_ref[...], preferred_element_type=F32)
    if rope_half:
        cos = rest[0][...]
        sin = rest[1][...]
        n_out = o_refs[0].shape[1]
        for j in range(acc.shape[1] // LANES):
            res = _rope_apply(acc[:, j * LANES:(j + 1) * LANES], cos, sin, rope_half)
            nw = min(LANES, n_out - j * LANES)
            for o_ref in o_refs:
                o_ref[:, j * LANES:j * LANES + nw] = res[:, :nw].astype(o_ref.dtype)
    else:
        for o_ref in o_refs:
            o_ref[...] = acc.astype(o_ref.dtype)


def _mm(x, w, rope=None, n_out=None, out_dtype=F32, mxu_copy=False):
    m, k = x.shape
    n = w.shape[1]
    n_out = n if n_out is None else n_out
    tm = min(m, 1024) if rope is None else min(m, 1024, rope[0].shape[0])
    tn = min(n, 1024)
    assert m % tm == 0 and n % tn == 0 and (n_out == n or n == tn)
    in_specs = [pl.BlockSpec((tm, k), lambda i, j: (i, 0)), pl.BlockSpec((k, tn), lambda i, j: (0, j))]
    args = [x, w]
    half = 0
    if rope is not None:
        cos, sin, half = rope
        nt = cos.shape[0] // tm
        assert cos.shape[0] % tm == 0
        in_specs += [pl.BlockSpec((tm, LANES), lambda i, j: (i % nt, 0))] * 2
        args += [cos, sin]
    out_spec = pl.BlockSpec((tm, min(tn, n_out)), lambda i, j: (i, j))
    dtypes = (out_dtype, MXU_DTYPE) if mxu_copy else (out_dtype,)
    outs = pl.pallas_call(
        functools.partial(_mm_kernel, rope_half=half, n_outs=len(dtypes)),
        grid=(m // tm, n // tn),
        in_specs=in_specs,
        out_specs=[out_spec] * len(dtypes),
        out_shape=[jax.ShapeDtypeStruct((m, n_out), dt) for dt in dtypes],
        compiler_params=_params(("parallel", "parallel")),
        name="proj",
    )(*args)
    return outs if mxu_copy else outs[0]


def _mm_nt_kernel(w_ref, x_ref, o_ref):
    o_ref[...] = lax.dot_general(w_ref[...], x_ref[...], (((1,), (1,)), ((), ())), preferred_element_type=F32)


def _mm_nt(w_t, x):
    n, k = w_t.shape
    m = x.shape[0]
    tm = min(m, 512)
    return pl.pallas_call(
        _mm_nt_kernel,
        grid=(m // tm,),
        in_specs=[pl.BlockSpec((n, k), lambda i: (0, 0)), pl.BlockSpec((tm, k), lambda i: (i, 0))],
        out_specs=pl.BlockSpec((n, tm), lambda i: (0, i)),
        out_shape=jax.ShapeDtypeStruct((n, m), F32),
        compiler_params=_params(("parallel",)),
        name="proj_t",
    )(w_t, x)


def _proj_ln_kernel(a_ref, b_ref, wa_ref, wb_ref, x_ref, g_ref, bt_ref, xo_ref, xb_ref, *, alpha):
    y = jnp.dot(a_ref[...], wa_ref[...], preferred_element_type=F32)
    y = y + jnp.dot(b_ref[...], wb_ref[...], preferred_element_type=F32)
    out = _layer_norm(alpha * x_ref[...] + y, g_ref[...], bt_ref[...])
    xo_ref[...] = out
    xb_ref[...] = out.astype(xb_ref.dtype)


def _proj_ln(a, b, w_out, x, g, bt, alpha):
    m, d = x.shape
    ka = a.shape[1]
    tm = min(m, 256)
    wa, wb = w_out[:ka], w_out[ka:]
    row = lambda i: (i, 0)
    fixed = lambda i: (0, 0)
    return pl.pallas_call(
        functools.partial(_proj_ln_kernel, alpha=alpha),
        grid=(m // tm,),
        in_specs=[pl.BlockSpec((tm, ka), row), pl.BlockSpec((tm, b.shape[1]), row),
                  pl.BlockSpec(wa.shape, fixed), pl.BlockSpec(wb.shape, fixed),
                  pl.BlockSpec((tm, d), row), pl.BlockSpec((1, d), fixed), pl.BlockSpec((1, d), fixed)],
        out_specs=[pl.BlockSpec((tm, d), row), pl.BlockSpec((tm, d), row)],
        out_shape=[jax.ShapeDtypeStruct((m, d), F32), jax.ShapeDtypeStruct((m, d), MXU_DTYPE)],
        compiler_params=_params(("parallel",)),
        name="out_proj_ln",
    )(a, b, wa, wb, x, g.reshape(1, d), bt.reshape(1, d))


def _lane_scan(x, op, fill):
    n = x.shape[1]
    lane = _iota(x.shape, 1)
    s = 1
    while s < n:
        x = op(x, jnp.where(lane >= s, pltpu.roll(x, s, 1), fill))
        s *= 2
    return x


def _mlstm_head(q, k, v, o_gate, ig, fpre, g, c_prev, n_prev, m_prev, valid_len):
    L = q.shape[0]
    lf = jnp.minimum(fpre, 0.0) - jnp.log1p(jnp.exp(-jnp.abs(fpre)))
    if valid_len < L:
        live = _iota((1, L), 1) < valid_len
        ig = jnp.where(live, ig, NEG_INF)
        lf = jnp.where(live, lf, 0.0)
    a_row = _lane_scan(lf, jnp.add, 0.0)
    b_row = ig - a_row
    run_row = jnp.maximum(_lane_scan(b_row, jnp.maximum, NEG_INF), m_prev)
    run_col = _row_to_col(run_row, L)
    causal = _iota((L, L), 1) <= _iota((L, L), 0)
    dmat = jnp.where(causal, jnp.exp(b_row - run_col), 0.0)
    w_inter = jnp.exp(m_prev - run_col)
    s = _dot_nt(q, k) * dmat
    num = w_inter * _dot(q, c_prev) + _dot(s, v)
    qn = jnp.sum(_rnd(q) * _rnd(n_prev), axis=1, keepdims=True)
    den = w_inter * qn + jnp.sum(s, axis=1, keepdims=True)
    a_col = _row_to_col(a_row, L)
    m_t = a_col + run_col
    h = num / jnp.maximum(jnp.abs(den), jnp.exp(-m_t))
    h = _head_norm(h, g) * _sigmoid(o_gate)
    a_last = a_row[:, L - 1:L]
    m_new = a_last + run_row[:, L - 1:L]
    w_c = jnp.exp(a_last + m_prev - m_new)
    ws_row = jnp.exp(a_last - a_row + ig - m_new)
    ws_col = _row_to_col(ws_row, L)
    kw = k * ws_col
    c_new = w_c * c_prev + _dot(kw.T, v)
    n_new = w_c * n_prev + jnp.sum(_rnd(ws_col) * _rnd(k), axis=0, keepdims=True)
    return h, c_new, n_new, m_new


def _mlstm_kernel(q_ref, k_ref, v_ref, o_ref, ig_ref, fg_ref, big_ref, bfg_ref, g_ref, c0_ref, n0_ref, m0_ref,
                  h_ref, c_ref, n_ref, m_ref, c_s, n_s, m_s, *, valid_len, scale):
    ci = pl.program_id(1)

    @pl.when(ci == 0)
    def _():
        c_s[...] = c0_ref[...]
        n_s[...] = n0_ref[...]
        m_s[...] = m0_ref[...]

    nh, dh, _ = c_s.shape
    for hh in range(nh):
        cols = slice(hh * dh, (hh + 1) * dh)
        h, c_new, n_new, m_new = _mlstm_head(
            q_ref[:, cols], k_ref[:, cols] * scale, v_ref[:, cols], o_ref[:, cols],
            ig_ref[hh] + big_ref[hh], fg_ref[hh] + bfg_ref[hh], g_ref[hh], c_s[hh], n_s[hh], m_s[hh], valid_len)
        h_ref[:, cols] = h.astype(h_ref.dtype)
        c_s[hh] = c_new
        n_s[hh] = n_new
        m_s[hh] = m_new

    @pl.when(ci == pl.num_programs(1) - 1)
    def _():
        c_ref[...] = c_s[...]
        n_ref[...] = n_s[...]
        m_ref[...] = m_s[...]


def _mlstm(z, gates_t, b_ig, b_fg, g_mh, c0, n0, m0, valid_len):
    bsz, t, _ = z.shape
    nh, dh = g_mh.shape
    L = CHUNK
    nc = t // L
    w = nh * dh
    assert dh == LANES and t % L == 0
    col = lambda part: pl.BlockSpec((None, L, w), lambda b, c: (b, c, part))
    gate = lambda part: pl.BlockSpec((None, nh, 1, L), lambda b, c: (b, part, 0, c))
    per_head = lambda shp: pl.BlockSpec((nh,) + shp, lambda b, c: (0, 0, 0))
    state = lambda shp: pl.BlockSpec((None, nh) + shp, lambda b, c: (b, 0, 0, 0))
    return pl.pallas_call(
        functools.partial(_mlstm_kernel, valid_len=valid_len, scale=dh ** -0.5),
        grid=(bsz, nc),
        in_specs=[col(0), col(1), col(2), col(3), gate(0), gate(1),
                  per_head((1, 1)), per_head((1, 1)), per_head((1, dh)),
                  state((dh, dh)), state((1, dh)), state((1, 1))],
        out_specs=[pl.BlockSpec((None, L, w), lambda b, c: (b, c, 0)),
                   state((dh, dh)), state((1, dh)), state((1, 1))],
        out_shape=[jax.ShapeDtypeStruct((bsz, t, w), MXU_DTYPE),
                   jax.ShapeDtypeStruct((bsz, nh, dh, dh), F32),
                   jax.ShapeDtypeStruct((bsz, nh, 1, dh), F32),
                   jax.ShapeDtypeStruct((bsz, nh, 1, 1), F32)],
        scratch_shapes=[pltpu.VMEM((nh, dh, dh), F32), pltpu.VMEM((nh, 1, dh), F32), pltpu.VMEM((nh, 1, 1), F32)],
        compiler_params=_params(("parallel", "arbitrary")),
        name="mlstm",
    )(z, z, z, z, gates_t, gates_t, b_ig.reshape(nh, 1, 1), b_fg.reshape(nh, 1, 1), g_mh.reshape(nh, 1, dh),
      c0, n0.reshape(bsz, nh, 1, dh), m0.reshape(bsz, nh, 1, 1))


def _ret_kernel(q_ref, k_ref, v_ref, gd_ref, lg_ref, g_ref, s0_ref, o_ref, s_ref, s_s, *, true_len, scale):
    ci = pl.program_id(1)

    @pl.when(ci == 0)
    def _():
        s_s[...] = s0_ref[...]

    L = q_ref.shape[0]
    nh, dk, dv = s_s.shape
    t_col = _iota((L, 1), 0).astype(F32)
    diff = (_iota((L, L), 0) - _iota((L, L), 1)).astype(F32)
    causal = diff >= 0.0
    chunk_len = float(min(L, true_len))
    for hh in range(nh):
        q = q_ref[:, hh * dk:(hh + 1) * dk]
        k = k_ref[:, hh * dk:(hh + 1) * dk] * scale
        v = v_ref[:, hh * dv:(hh + 1) * dv]
        lg = lg_ref[hh]
        decay = jnp.where(causal, jnp.exp(jnp.where(causal, diff, 0.0) * lg), 0.0)
        s_prev = s_s[hh]
        inner = _dot_nt(q, k) * decay
        o = _dot(inner, v) + jnp.exp((t_col + 1.0) * lg) * _dot(q, s_prev)
        ws_col = jnp.where(t_col < chunk_len, jnp.exp((chunk_len - 1.0 - t_col) * lg), 0.0)
        s_s[hh] = jnp.exp(chunk_len * lg) * s_prev + _dot((k * ws_col).T, v)
        gd = gd_ref[:, hh * dv:(hh + 1) * dv]
        o_ref[:, hh * dv:(hh + 1) * dv] = (_head_norm(o, g_ref[hh]) * (gd * _sigmoid(gd))).astype(o_ref.dtype)

    @pl.when(ci == pl.num_programs(1) - 1)
    def _():
        s_ref[...] = s_s[...]


def _retention(qk, vg, log_gamma, g_ret, s0, true_len):
    bsz, t, _ = qk.shape
    nh, dv = g_ret.shape
    dk = s0.shape[2]
    L = CHUNK
    nc = t // L
    assert dk % LANES == 0 and dv % LANES == 0
    half = lambda wdt, part: pl.BlockSpec((None, L, nh * wdt), lambda b, c: (b, c, part))
    return pl.pallas_call(
        functools.partial(_ret_kernel, true_len=true_len, scale=dk ** -0.5),
        grid=(bsz, nc),
        in_specs=[half(dk, 0), half(dk, 1), half(dv, 0), half(dv, 1),
                  pl.BlockSpec((nh, 1, 1), lambda b, c: (0, 0, 0)),
                  pl.BlockSpec((nh, 1, dv), lambda b, c: (0, 0, 0)),
                  pl.BlockSpec((None, nh, dk, dv), lambda b, c: (b, 0, 0, 0))],
        out_specs=[pl.BlockSpec((None, L, nh * dv), lambda b, c: (b, c, 0)),
                   pl.BlockSpec((None, nh, dk, dv), lambda b, c: (b, 0, 0, 0))],
        out_shape=[jax.ShapeDtypeStruct((bsz, t, nh * dv), MXU_DTYPE),
                   jax.ShapeDtypeStruct((bsz, nh, dk, dv), F32)],
        scratch_shapes=[pltpu.VMEM((nh, dk, dv), F32)],
        compiler_params=_params(("parallel", "arbitrary")),
        name="retention",
    )(qk, qk, vg, vg, log_gamma.reshape(nh, 1, 1), g_ret.reshape(nh, 1, dv), s0)


def _lambda(lq1, lk1, lq2, lk2, lam_init):
    s1 = jnp.sum(lq1 * lk1, axis=1, keepdims=True)
    s2 = jnp.sum(lq2 * lk2, axis=1, keepdims=True)
    return jnp.exp(s1) - jnp.exp(s2) + lam_init


def _rms_sub(o, g, lam_init):
    return o * lax.rsqrt(jnp.mean(o * o, axis=-1, keepdims=True) + EPS) * g * (1.0 - lam_init)


def _diff_kernel(qi_ref, kj_ref, q_ref, k_ref, v_ref, lq1_ref, lk1_ref, lq2_ref, lk2_ref, g_ref, o_ref,
                 m_s, l_s, acc_s, *, lam_init, scale):
    qi = qi_ref[pl.program_id(2)]
    kj = kj_ref[pl.program_id(2)]
    tq, dv = q_ref.shape
    tk = k_ref.shape[0]
    dh = dv // 2

    @pl.when(kj == 0)
    def _():
        m_s[...] = jnp.full(m_s.shape, NEG_INF, F32)
        l_s[...] = jnp.zeros(l_s.shape, F32)
        acc_s[...] = jnp.zeros(acc_s.shape, F32)

    qpos = qi * tq + _iota((tq, tk), 0)
    kpos = kj * tk + _iota((tq, tk), 1)
    ok = kpos <= qpos
    vb = _mx(v_ref[...])
    for c in range(2):
        s = _dot_nt(q_ref[:, c * dh:(c + 1) * dh], k_ref[:, c * dh:(c + 1) * dh]) * scale
        s = jnp.where(ok, s, NEG_INF)
        m_old = m_s[c]
        m_new = jnp.maximum(m_old, _row_max(s))
        alpha = jnp.exp(m_old - m_new)
        p = jnp.exp(s - m_new)
        l_s[c] = alpha * l_s[c] + _row_sum(p)
        acc_s[c] = alpha * acc_s[c] + jnp.dot(_mx(p), vb, preferred_element_type=F32)
        m_s[c] = m_new

    @pl.when(kj == qi)
    def _():
        lam = _lambda(lq1_ref[...], lk1_ref[...], lq2_ref[...], lk2_ref[...], lam_init)
        o = acc_s[0] / l_s[0] - lam * (acc_s[1] / l_s[1])
        o_ref[...] = _rms_sub(o, g_ref[...], lam_init).astype(o_ref.dtype)


def _diff_attn(qb, kb, vb, lams, g_sub, lam_init, nh):
    bsz, t, w = qb.shape
    dv = w // nh
    dh = dv // 2
    tq = tk = min(t, 1024)
    nq = t // tq
    pairs = [(i, j) for i in range(nq) for j in range(i + 1)]
    qi_tab = jnp.asarray([p[0] for p in pairs], I32)
    kj_tab = jnp.asarray([p[1] for p in pairs], I32)
    vec = pl.BlockSpec((1, dh), lambda b, h, p, qt, kt: (0, 0))
    qo = pl.BlockSpec((None, tq, dv), lambda b, h, p, qt, kt: (b, qt[p], h))
    kv = pl.BlockSpec((None, tk, dv), lambda b, h, p, qt, kt: (b, kt[p], h))
    return pl.pallas_call(
        functools.partial(_diff_kernel, lam_init=lam_init, scale=dh ** -0.5),
        grid_spec=pltpu.PrefetchScalarGridSpec(
            num_scalar_prefetch=2, grid=(bsz, nh, len(pairs)),
            in_specs=[qo, kv, kv, vec, vec, vec, vec, pl.BlockSpec((1, dv), lambda b, h, p, qt, kt: (0, 0))],
            out_specs=qo,
            scratch_shapes=[pltpu.VMEM((2, tq, 1), F32), pltpu.VMEM((2, tq, 1), F32), pltpu.VMEM((2, tq, dv), F32)]),
        out_shape=jax.ShapeDtypeStruct((bsz, t, w), MXU_DTYPE),
        compiler_params=_params(("parallel", "parallel", "arbitrary")),
        name="diff_attn",
    )(qi_tab, kj_tab, qb, kb, vb, *[x.reshape(1, dh) for x in lams], g_sub.reshape(1, dv))


def _f32_key(x):
    i = lax.bitcast_convert_type(x, I32)
    return i ^ ((i >> 31) & 0x7FFFFFFF)


KEY_NEG_INF = -2139095041
I32_MIN = -2 ** 31


def _strict_upper(n):
    return jnp.where(_iota((n, n), 0) < _iota((n, n), 1), 1.0, 0.0).astype(MXU_DTYPE)


def _dsa_kernel(qc_ref, kc_ref, vc_ref, qi_ref, ki_ref, wi_ref, o_ref, key_s, m_s, l_s, acc_s,
                *, topk, n_heads, n_idx_heads, scale, wi_scale):
    qb = pl.program_id(1)
    tq = qc_ref.shape[0]
    dh = qc_ref.shape[1] // n_heads
    ck = key_s.shape[2]
    d_i = ki_ref.shape[1]
    q0 = qb * tq
    n_chunks = (q0 + tq + ck - 1) // ck

    def _select():
        qi = qi_ref[...]
        wi = _rnd(wi_ref[...] * wi_scale)
        qpos = q0 + _iota((tq, ck), 0)

        def score_chunk(c, carry):
            start = pl.multiple_of(c * ck, ck)
            kic = ki_ref[pl.ds(start, ck), :]
            acc = jnp.zeros((tq, ck), F32)
            for hh in range(n_idx_heads):
                sc = jnp.maximum(_dot_nt(qi[:, hh * d_i:(hh + 1) * d_i], kic), 0.0)
                acc = acc + wi[:, hh:hh + 1] * sc
            acc = acc + 0.0
            kpos = c * ck + _iota((tq, ck), 1)
            key_s[c] = jnp.where(kpos <= qpos, _f32_key(acc), KEY_NEG_INF)
            return carry

        lax.fori_loop(0, n_chunks, score_chunk, 0)

        def count(pred):
            def body(c, acc):
                hit = jnp.where(pred(key_s[c]), 1.0, 0.0)
                part = hit[:, 0:LANES]
                for j in range(1, ck // LANES):
                    part = part + hit[:, j * LANES:(j + 1) * LANES]
                return acc + part
            acc = lax.fori_loop(0, n_chunks, body, jnp.zeros((tq, LANES), F32))
            return jnp.sum(acc, axis=1, keepdims=True)

        def bit_step(it, prefix):
            bit = jnp.left_shift(jnp.int32(1), 31 - it)
            cand = (prefix | bit) ^ I32_MIN
            cnt = count(lambda kk: kk >= cand)
            return jnp.where(cnt >= float(topk), prefix | bit, prefix)

        prefix = lax.fori_loop(0, 32, bit_step, jnp.zeros((tq, 1), I32))
        thr = prefix ^ I32_MIN
        need = float(topk) - count(lambda kk: kk > thr)
        upper = _strict_upper(LANES)

        def mask_chunk(c, taken):
            kk = key_s[c]
            cols = []
            for j in range(ck // LANES):
                kj = kk[:, j * LANES:(j + 1) * LANES]
                eq = kj == thr
                eqf = jnp.where(eq, 1.0, 0.0)
                before = taken + jnp.dot(_mx(eqf), upper, preferred_element_type=F32)
                sel = ((kj > thr) | (eq & (before < need))) & (kj > KEY_NEG_INF)
                cols.append(jnp.where(sel, 0.0, NEG_INF))
                taken = taken + jnp.sum(eqf, axis=1, keepdims=True)
            key_s[c] = lax.bitcast_convert_type(jnp.concatenate(cols, axis=1), I32)
            return taken

        lax.fori_loop(0, n_chunks, mask_chunk, jnp.zeros((tq, 1), F32))

    _select()

    m_s[...] = jnp.full(m_s.shape, NEG_INF, F32)
    l_s[...] = jnp.zeros(l_s.shape, F32)
    acc_s[...] = jnp.zeros(acc_s.shape, F32)

    def attend(c, carry):
        start = pl.multiple_of(c * ck, ck)
        bias = lax.bitcast_convert_type(key_s[c], F32)
        for h in range(n_heads):
            cols = slice(h * dh, (h + 1) * dh)
            s = _dot_nt(qc_ref[:, cols], kc_ref[pl.ds(start, ck), cols]) * scale + bias
            m_old = m_s[h]
            m_new = jnp.maximum(m_old, _row_max(s))
            m_safe = jnp.where(m_new == NEG_INF, 0.0, m_new)
            alpha = jnp.exp(m_old - m_safe)
            p = jnp.exp(s - m_safe)
            l_s[h] = alpha * l_s[h] + _row_sum(p)
            acc_s[h] = alpha * acc_s[h] + _dot(p, vc_ref[pl.ds(start, ck), cols])
            m_s[h] = m_new
        return carry

    lax.fori_loop(0, n_chunks, attend, 0)
    for h in range(n_heads):
        o_ref[:, h * dh:(h + 1) * dh] = (acc_s[h] / l_s[h]).astype(o_ref.dtype)


def _dsa_attn(qc, kc, vc, qi, ki, wi, nh, n_idx_heads):
    bsz, t, w = qc.shape
    dh = w // nh
    d_i = ki.shape[2]
    tq = min(t, 256)
    ck = min(t, 1024)
    topk = min(TOPK_MAX, t // 4)
    rows = lambda wdt: pl.BlockSpec((None, tq, wdt), lambda b, i: (b, i, 0))
    whole = lambda wdt: pl.BlockSpec((None, t, wdt), lambda b, i: (b, 0, 0), pipeline_mode=pl.Buffered(1))
    return pl.pallas_call(
        functools.partial(_dsa_kernel, topk=topk, n_heads=nh, n_idx_heads=n_idx_heads, scale=dh ** -0.5,
                          wi_scale=(n_idx_heads * d_i) ** -0.5),
        grid=(bsz, t // tq),
        in_specs=[rows(w), whole(w), whole(w), rows(qi.shape[2]), whole(d_i), rows(wi.shape[2])],
        out_specs=rows(w),
        out_shape=jax.ShapeDtypeStruct((bsz, t, w), MXU_DTYPE),
        scratch_shapes=[pltpu.VMEM((t // ck, tq, ck), I32),
                        pltpu.VMEM((nh, tq, 1), F32), pltpu.VMEM((nh, tq, 1), F32), pltpu.VMEM((nh, tq, dh), F32)],
        compiler_params=_params(("parallel", "arbitrary")),
        name="dsa_attn",
    )(qc, kc, vc, qi, ki, wi)


def _page_copy(cache_hbm, layer, phys, buf, slot, i, sem):
    return pltpu.make_async_copy(cache_hbm.at[layer, phys], buf.at[slot, i], sem.at[slot])


def _paged_step(cache_hbm, layer, pt_ref, buf, sem):
    g = buf.shape[1]
    per_b = pl.num_programs(1)
    step = pl.program_id(0) * per_b + pl.program_id(1)

    def fetch(t):
        b = t // per_b
        s = t % per_b
        for i in range(g):
            _page_copy(cache_hbm, layer, pt_ref[b, s * g + i], buf, t % 2, i, sem).start()

    @pl.when(step == 0)
    def _():
        fetch(step)

    @pl.when(step + 1 < pl.num_programs(0) * per_b)
    def _():
        fetch(step + 1)

    slot = step % 2
    for i in range(g):
        _page_copy(cache_hbm, layer, 0, buf, slot, i, sem).wait()
    return slot


def _paged_call(body, name, page_table, cache, g, in_arrays, in_specs, out_spec, out_shape):
    bd, n_pages = page_table.shape
    assert n_pages % g == 0
    return pl.pallas_call(
        body,
        grid_spec=pltpu.PrefetchScalarGridSpec(
            num_scalar_prefetch=1, grid=(bd, n_pages // g),
            in_specs=in_specs + [pl.BlockSpec(memory_space=pl.ANY)],
            out_specs=out_spec,
            scratch_shapes=[pltpu.VMEM((2, g) + cache.shape[2:], cache.dtype), pltpu.SemaphoreType.DMA((2,))]),
        out_shape=out_shape,
        compiler_params=_params(("arbitrary", "arbitrary")),
        name=name,
    )(page_table, *in_arrays, cache)


def _pages_per_step(n_pages, want):
    g = min(want, n_pages)
    while n_pages % g:
        g -= 1
    return g


def _key_rows(x, nh):
    bd = x.shape[0]
    return x.reshape(bd, nh, 2, LANES).transpose(0, 2, 1, 3).reshape(bd, 2 * nh, LANES)


def _diff_qk_kernel(pt_ref, q_ref, k_hbm, o_ref, buf, sem, *, layer):
    slot = _paged_step(k_hbm, layer, pt_ref, buf, sem)
    _, g, rows, _ = buf.shape
    for i in range(g):
        o_ref[:, i * rows:(i + 1) * rows] = _dot_nt(q_ref[...], buf[slot, i])


def _diff_pv_kernel(pt_ref, a_ref, v_hbm, o_ref, buf, sem, *, layer):
    slot = _paged_step(v_hbm, layer, pt_ref, buf, sem)
    _, g, rows, _ = buf.shape

    @pl.when(pl.program_id(1) == 0)
    def _():
        o_ref[...] = jnp.zeros(o_ref.shape, F32)

    acc = jnp.zeros(o_ref.shape, F32)
    for i in range(g):
        acc = acc + _dot(a_ref[:, i * rows:(i + 1) * rows], buf[slot, i])
    o_ref[...] += acc


def _diff_softmax_kernel(s_ref, q_ref, kn_ref, lq1_ref, lk1_ref, lq2_ref, lk2_ref, a_ref, an_ref,
                         *, lam_init, scale, nh):
    n_rows, width = s_ref.shape
    ch = min(width, 64 * LANES)
    real = (_iota((n_rows, ch), 1) % n_rows) == _iota((n_rows, ch), 0)
    chunks = [slice(i * ch, (i + 1) * ch) for i in range(width // ch)]

    def scores(sl):
        return jnp.where(real, s_ref[:, sl] * scale, NEG_INF)

    s_new = jnp.sum(_rnd(q_ref[...]) * _rnd(kn_ref[...]), axis=1, keepdims=True) * scale
    m = s_new
    for sl in chunks:
        m = jnp.maximum(m, jnp.max(scores(sl), axis=1, keepdims=True))
    e_new = jnp.exp(s_new - m)
    z = e_new
    for sl in chunks:
        z = z + jnp.sum(jnp.exp(scores(sl) - m), axis=1, keepdims=True)
    lam = _lambda(lq1_ref[...], lk1_ref[...], lq2_ref[...], lk2_ref[...], lam_init)
    p_new = e_new / z
    for sl in chunks:
        p = jnp.exp(scores(sl) - m) / z
        p0, p1 = p[:nh], p[nh:]
        a_ref[:, sl] = jnp.concatenate([p0 - lam * pltpu.roll(p1, ch - nh, 1),
                                        pltpu.roll(p0, nh, 1) - lam * p1], axis=0)
    a_new = p_new[:nh] - lam * p_new[nh:]
    an_ref[...] = jnp.broadcast_to(jnp.concatenate([a_new, a_new], axis=0), an_ref.shape)


def _diff_final_kernel(acc_ref, an_ref, vn_ref, g_ref, o_ref, *, lam_init):
    n_rows, wdt = acc_ref.shape
    nh = n_rows // 2
    full = acc_ref[...] + _rnd(an_ref[:, 0:1]) * _rnd(vn_ref[...])
    sq = jnp.sum(full * full, axis=1, keepdims=True)
    ms = (sq + jnp.concatenate([sq[nh:], sq[:nh]], axis=0)) / (2.0 * wdt)
    o_ref[...] = (full * lax.rsqrt(ms + EPS) * g_ref[...] * (1.0 - lam_init)).astype(o_ref.dtype)


def _diff_decode(qb, kb_new, vb_new, cache_k, cache_v, layer, page_table, lams, g_sub, lam_init, nh):
    bd, w = qb.shape
    dv = w // nh
    dh = dv // 2
    assert dh == LANES and 2 * nh == SUBLANES
    n_layers, n_pool, page = cache_k.shape[:3]
    n_pages = page_table.shape[1]
    rows = page * 2 * nh
    width = n_pages * rows
    g = _pages_per_step(n_pages, 8)

    def key_row_view(cache):
        c6 = cache.reshape(n_layers, n_pool, page, nh, 2, LANES)
        return c6.transpose(0, 1, 2, 4, 3, 5).reshape(n_layers, n_pool, rows, LANES)

    q8, k8, v8 = _key_rows(qb, nh), _key_rows(kb_new, nh), _key_rows(vb_new, nh)
    whole = lambda wdt: pl.BlockSpec((None, SUBLANES, wdt), lambda b, p, pt: (b, 0, 0))
    chunk = pl.BlockSpec((None, SUBLANES, g * rows), lambda b, p, pt: (b, 0, p))
    s = _paged_call(
        functools.partial(_diff_qk_kernel, layer=layer), "diff_decode_qk", page_table, key_row_view(cache_k), g,
        [q8], [whole(LANES)], chunk, jax.ShapeDtypeStruct((bd, SUBLANES, width), F32))
    per_b = lambda shp: pl.BlockSpec((None,) + shp, lambda b: (b, 0, 0))
    vec = pl.BlockSpec((1, dh), lambda b: (0, 0))
    a, a_new = pl.pallas_call(
        functools.partial(_diff_softmax_kernel, lam_init=lam_init, scale=dh ** -0.5, nh=nh),
        grid=(bd,),
        in_specs=[per_b((SUBLANES, width)), per_b((SUBLANES, LANES)), per_b((SUBLANES, LANES)), vec, vec, vec, vec],
        out_specs=[per_b((SUBLANES, width)), per_b((SUBLANES, LANES))],
        out_shape=[jax.ShapeDtypeStruct((bd, SUBLANES, width), F32), jax.ShapeDtypeStruct((bd, SUBLANES, LANES), F32)],
        compiler_params=_params(("parallel",)),
        name="diff_decode_softmax",
    )(s, q8, k8, *[x.reshape(1, dh) for x in lams])
    acc = _paged_call(
        functools.partial(_diff_pv_kernel, layer=layer), "diff_decode_pv", page_table, key_row_view(cache_v), g,
        [a], [chunk], whole(LANES), jax.ShapeDtypeStruct((bd, SUBLANES, LANES), F32))
    g8 = jnp.repeat(g_sub.reshape(2, 1, LANES), nh, axis=1).reshape(SUBLANES, LANES)
    out8 = pl.pallas_call(
        functools.partial(_diff_final_kernel, lam_init=lam_init),
        grid=(bd,),
        in_specs=[per_b((SUBLANES, LANES))] * 3 + [pl.BlockSpec((SUBLANES, LANES), lambda b: (0, 0))],
        out_specs=per_b((SUBLANES, LANES)),
        out_shape=jax.ShapeDtypeStruct((bd, SUBLANES, LANES), MXU_DTYPE),
        compiler_params=_params(("parallel",)),
        name="diff_decode_final",
    )(acc, a_new, v8, g8)
    return out8.reshape(bd, 2, nh, LANES).transpose(0, 2, 1, 3).reshape(bd, w)


def _idx_score_kernel(pt_ref, qi_ref, wi_ref, ki_hbm, o_ref, buf, sem, *, layer, wi_scale):
    slot = _paged_step(ki_hbm, layer, pt_ref, buf, sem)
    _, g, page, _ = buf.shape
    w = _rnd(wi_ref[...] * wi_scale)
    for i in range(g):
        sc = jnp.maximum(_dot_nt(qi_ref[...], buf[slot, i]), 0.0)
        o_ref[:, i * page:(i + 1) * page] = jnp.sum(w * _rnd(sc), axis=0, keepdims=True) + 0.0


def _select_kernel(sc_ref, qi_ref, wi_ref, kn_ref, idx_ref, nsel_ref, mnew_ref, pos_s, *, topk, wi_scale):
    npg, page = sc_ref.shape
    sc_new = jnp.maximum(jnp.sum(_rnd(qi_ref[...]) * _rnd(kn_ref[...]), axis=1, keepdims=True), 0.0)
    s_new = jnp.sum(_rnd(wi_ref[...] * wi_scale) * _rnd(sc_new), axis=0, keepdims=True) + 0.0
    keys = _f32_key(sc_ref[...])
    key_new = _f32_key(s_new)

    def total(x):
        return jnp.sum(jnp.sum(x, axis=1, keepdims=True), axis=0, keepdims=True)

    def count(pred):
        return total(jnp.where(pred(keys), 1.0, 0.0)) + jnp.where(pred(key_new), 1.0, 0.0)

    def bit_step(it, prefix):
        bit = jnp.left_shift(jnp.int32(1), 31 - it)
        cand = (prefix | bit) ^ I32_MIN
        return jnp.where(count(lambda kk: kk >= cand) >= float(topk), prefix | bit, prefix)

    thr = lax.fori_loop(0, 32, bit_step, jnp.zeros((1, 1), I32)) ^ I32_MIN
    need = float(topk) - count(lambda kk: kk > thr)
    strict_upper = _strict_upper(page)
    strict_lower = jnp.where(_iota((npg, npg), 1) < _iota((npg, npg), 0), 1.0, 0.0).astype(MXU_DTYPE)

    def count_before(flag):
        in_row = jnp.dot(_mx(flag), strict_upper, preferred_element_type=F32)
        row_tot = jnp.broadcast_to(jnp.sum(flag, axis=1, keepdims=True), (npg, page))
        return in_row + jnp.dot(strict_lower, _mx(row_tot), preferred_element_type=F32)

    eq = keys == thr
    eqf = jnp.where(eq, 1.0, 0.0)
    sel = ((keys > thr) | (eq & (count_before(eqf) < need))) & (keys > KEY_NEG_INF)
    sel_new = ((key_new > thr) | ((key_new == thr) & (total(eqf) < need))) & (key_new > KEY_NEG_INF)
    mnew_ref[...] = jnp.broadcast_to(jnp.where(sel_new, 1.0, 0.0), mnew_ref.shape)
    self = jnp.where(sel, 1.0, 0.0)
    pos_s[...] = jnp.where(sel, count_before(self), -1.0)
    n_slots = idx_ref.shape[0]
    slot_id = _iota((n_slots, page), 0).astype(F32)
    lane = _iota((n_slots, page), 1).astype(F32)

    def gather_page(p, acc):
        hit = pos_s[pl.ds(p, 1), :] == slot_id
        return acc + jnp.where(hit, jnp.asarray(p * page, F32) + lane, 0.0)

    acc = lax.fori_loop(0, npg, gather_page, jnp.zeros((n_slots, page), F32))
    idx_ref[...] = jnp.broadcast_to(jnp.sum(acc, axis=1, keepdims=True).astype(I32), idx_ref.shape)
    nsel_ref[...] = jnp.broadcast_to(total(self), nsel_ref.shape)


def _dsa_gather_kernel(idx_ref, pt_ref, q_ref, kn_ref, vn_ref, mnew_ref, nsel_ref, k_hbm, v_hbm, o_ref,
                       kbuf, vbuf, sem, *, layer, scale):
    b = pl.program_id(0)
    n_slots, nh, dh = kbuf.shape
    page = k_hbm.shape[2]

    def row(cache, i, buf, r, s):
        return pltpu.make_async_copy(cache.at[layer, pt_ref[b, i // page], pl.ds(i % page, 1)],
                                     buf.at[pl.ds(r, 1)], sem.at[s])

    def issue(r, carry):
        i = idx_ref[b, r]
        row(k_hbm, i, kbuf, r, 0).start()
        row(v_hbm, i, vbuf, r, 1).start()
        return carry

    def drain(r, carry):
        row(k_hbm, 0, kbuf, 0, 0).wait()
        row(v_hbm, 0, vbuf, 0, 1).wait()
        return carry

    lax.fori_loop(0, n_slots, issue, 0)
    lax.fori_loop(0, n_slots, drain, 0)
    q = q_ref[...]
    head = _iota((nh, n_slots), 0)
    s = jnp.zeros((nh, n_slots), F32)
    for h in range(nh):
        s = jnp.where(head == h, _dot_nt(q, kbuf[:, h, :]), s)
    live = _iota((nh, n_slots), 1).astype(F32) < nsel_ref[:, 0:1]
    s = jnp.where(live, s * scale, NEG_INF)
    s_new = jnp.sum(_rnd(q) * _rnd(kn_ref[...]), axis=1, keepdims=True) * scale
    s_new = jnp.where(mnew_ref[:, 0:1] > 0.0, s_new, NEG_INF)
    m = jnp.maximum(jnp.max(s, axis=1, keepdims=True), s_new)
    e = jnp.exp(s - m)
    e_new = jnp.exp(s_new - m)
    z = jnp.sum(e, axis=1, keepdims=True) + e_new
    p = e / z
    o = jnp.zeros((nh, dh), F32)
    head = _iota((nh, dh), 0)
    for h in range(nh):
        o = jnp.where(head == h, _dot(p, vbuf[:, h, :]), o)
    o = o + _rnd(e_new / z) * _rnd(vn_ref[...])
    o_ref[...] = o.astype(o_ref.dtype)


def _dsa_decode(qc, kc_new, vc_new, qi, ki_new, wi, cache_k, cache_v, cache_i, layer, page_table, nh, n_idx_heads):
    bd, w = qc.shape
    dh = w // nh
    d_i = ki_new.shape[1]
    page = cache_k.shape[2]
    n_pages = page_table.shape[1]
    past = n_pages * page
    topk = min(TOPK_MAX, (past + 1) // 4)
    n_slots = -(-topk // SUBLANES) * SUBLANES
    qi3 = qi.reshape(bd, n_idx_heads, d_i)
    wi3 = wi[:, :n_idx_heads].reshape(bd, n_idx_heads, 1)
    wi_scale = (n_idx_heads * d_i) ** -0.5
    g = _pages_per_step(n_pages, 16)
    scores = _paged_call(
        functools.partial(_idx_score_kernel, layer=layer, wi_scale=wi_scale), "idx_scores", page_table, cache_i, g,
        [qi3, wi3],
        [pl.BlockSpec((None, n_idx_heads, d_i), lambda b, p, pt: (b, 0, 0)),
         pl.BlockSpec((None, n_idx_heads, 1), lambda b, p, pt: (b, 0, 0))],
        pl.BlockSpec((None, 1, g * page), lambda b, p, pt: (b, 0, p)),
        jax.ShapeDtypeStruct((bd, 1, past), F32))
    per_b = lambda shp: pl.BlockSpec((None,) + shp, lambda b: (b, 0, 0))
    sel_idx, n_sel, mask_new = pl.pallas_call(
        functools.partial(_select_kernel, topk=topk, wi_scale=wi_scale),
        grid=(bd,),
        in_specs=[per_b((n_pages, page)), per_b((n_idx_heads, d_i)), per_b((n_idx_heads, 1)), per_b((1, d_i))],
        out_specs=[per_b((n_slots, LANES)), per_b((1, LANES)), per_b((1, LANES))],
        out_shape=[jax.ShapeDtypeStruct((bd, n_slots, LANES), I32), jax.ShapeDtypeStruct((bd, 1, LANES), F32),
                   jax.ShapeDtypeStruct((bd, 1, LANES), F32)],
        scratch_shapes=[pltpu.VMEM((n_pages, page), F32)],
        compiler_params=_params(("parallel",)),
        name="idx_select",
    )(scores.reshape(bd, n_pages, page), qi3, wi3, ki_new.reshape(bd, 1, d_i))
    per_b2 = lambda shp: pl.BlockSpec((None,) + shp, lambda b, idx, pt: (b, 0, 0))
    hbm = pl.BlockSpec(memory_space=pl.ANY)
    return pl.pallas_call(
        functools.partial(_dsa_gather_kernel, layer=layer, scale=dh ** -0.5),
        grid_spec=pltpu.PrefetchScalarGridSpec(
            num_scalar_prefetch=2, grid=(bd,),
            in_specs=[per_b2((nh, dh)), per_b2((nh, dh)), per_b2((nh, dh)), per_b2((1, LANES)), per_b2((1, LANES)),
                      hbm, hbm],
            out_specs=per_b2((nh, dh)),
            scratch_shapes=[pltpu.VMEM((n_slots, nh, dh), F32), pltpu.VMEM((n_slots, nh, dh), F32),
                            pltpu.SemaphoreType.DMA((2,))]),
        out_shape=jax.ShapeDtypeStruct((bd, nh, dh), MXU_DTYPE),
        compiler_params=_params(("arbitrary",)),
        name="dsa_decode_attn",
    )(sel_idx[:, :, 0], page_table, qc.reshape(bd, nh, dh), kc_new.reshape(bd, nh, dh), vc_new.reshape(bd, nh, dh),
      mask_new, n_sel, cache_k, cache_v).reshape(bd, w)


def _route_kernel(lg_ref, br_ref, e_ref, g_ref, r_ref, cnt_ref, carry_s, *, n_valid, n_groups):
    i = pl.program_id(0)
    ne, tn = lg_ref.shape
    per = ne // n_groups

    @pl.when(i == 0)
    def _():
        carry_s[...] = jnp.zeros(carry_s.shape, F32)

    aff = _sigmoid(lg_ref[...])
    sel = aff + br_ref[...]
    sub = _iota((per, tn), 0).astype(F32)
    best = None
    for gi in range(n_groups):
        s = sel[gi * per:(gi + 1) * per]
        m1 = jnp.max(s, axis=0, keepdims=True)
        i1 = jnp.min(jnp.where(s == m1, sub, float(per)), axis=0, keepdims=True)
        s2 = jnp.where(sub == i1, NEG_INF, s)
        m2 = jnp.max(s2, axis=0, keepdims=True)
        i2 = jnp.min(jnp.where(s2 == m2, sub, float(per)), axis=0, keepdims=True)
        cand = (m1 + m2, float(gi * per) + i1, float(gi * per) + i2)
        if best is None:
            best = cand
        else:
            better = cand[0] > best[0]
            best = tuple(jnp.where(better, c, b) for c, b in zip(cand, best))
    e0, e1 = best[1].astype(I32), best[2].astype(I32)
    eid = _iota((ne, tn), 0)
    valid = (i * tn + _iota((1, tn), 1)) < n_valid
    oh0 = (eid == e0) & valid
    oh1 = (eid == e1) & valid
    a0 = jnp.sum(jnp.where(eid == e0, aff, 0.0), axis=0, keepdims=True)
    a1 = jnp.sum(jnp.where(eid == e1, aff, 0.0), axis=0, keepdims=True)
    tot = a0 + a1
    e_ref[...] = jnp.concatenate([e0, e1], axis=0)
    g_ref[...] = jnp.concatenate([a0 / tot, a1 / tot], axis=0)
    oh = jnp.where(oh0, 1.0, 0.0) + jnp.where(oh1, 1.0, 0.0)
    before = carry_s[:, 0:1] + jnp.dot(_mx(oh), _strict_upper(tn), preferred_element_type=F32)
    r0 = jnp.sum(jnp.where(oh0, before, 0.0), axis=0, keepdims=True)
    r1 = jnp.sum(jnp.where(oh1, before, 0.0), axis=0, keepdims=True)
    r_ref[...] = jnp.concatenate([r0, r1], axis=0).astype(I32)
    carry_s[...] = carry_s[...] + jnp.sum(oh, axis=1, keepdims=True)
    cnt_ref[...] = carry_s[...]


def _route(logits_t, b_router, n_valid):
    ne, mp = logits_t.shape
    tn = min(mp, 256)
    tok = lambda i: (0, i)
    return pl.pallas_call(
        functools.partial(_route_kernel, n_valid=n_valid, n_groups=N_GROUPS),
        grid=(mp // tn,),
        in_specs=[pl.BlockSpec((ne, tn), tok), pl.BlockSpec((ne, 1), lambda i: (0, 0))],
        out_specs=[pl.BlockSpec((TOP_K, tn), tok), pl.BlockSpec((TOP_K, tn), tok), pl.BlockSpec((TOP_K, tn), tok),
                   pl.BlockSpec((ne, LANES), lambda i: (0, 0))],
        out_shape=[jax.ShapeDtypeStruct((TOP_K, mp), I32), jax.ShapeDtypeStruct((TOP_K, mp), F32),
                   jax.ShapeDtypeStruct((TOP_K, mp), I32), jax.ShapeDtypeStruct((ne, LANES), F32)],
        scratch_shapes=[pltpu.VMEM((ne, LANES), F32)],
        compiler_params=_params(("arbitrary",)),
        name="moe_route",
    )(logits_t, b_router.reshape(ne, 1))


def _row_copy(src, s_row, dst, d_row, sem):
    return pltpu.make_async_copy(src.at[pl.ds(s_row, 1)], dst.at[pl.ds(d_row, 1)], sem)


def _expert_kernel(be_ref, nu_ref, src_ref, x_hbm, wg_ref, wu_ref, wd_ref, o_ref, xbuf, wg_s, wu_s, wd_s, sem):
    b = pl.program_id(0)
    n_used = nu_ref[0]
    blk = xbuf.shape[1]

    def fetch(block):
        def issue(r, carry):
            _row_copy(x_hbm, src_ref[block * blk + r], xbuf.at[block % 2], r, sem.at[block % 2]).start()
            return carry
        lax.fori_loop(0, blk, issue, 0, unroll=8)

    @pl.when(b == 0)
    def _():
        fetch(b)

    @pl.when(b + 1 < n_used)
    def _():
        fetch(b + 1)

    @pl.when(b < n_used)
    def _():
        def drain(r, carry):
            _row_copy(x_hbm, 0, xbuf.at[b % 2], 0, sem.at[b % 2]).wait()
            return carry
        lax.fori_loop(0, blk, drain, 0, unroll=8)

        @pl.when((b == 0) | (be_ref[b] != be_ref[jnp.maximum(b - 1, 0)]))
        def _():
            wg_s[...] = _mx(wg_ref[...])
            wu_s[...] = _mx(wu_ref[...])
            wd_s[...] = _mx(wd_ref[...])

        x = _mx(xbuf[b % 2])
        gate = jnp.dot(x, wg_s[...], preferred_element_type=F32)
        up = jnp.dot(x, wu_s[...], preferred_element_type=F32)
        hdn = gate * _sigmoid(gate) * up
        y = jnp.dot(_mx(hdn), wd_s[...], preferred_element_type=F32)
        o_ref[...] = _rnd(y)

    @pl.when(b >= n_used)
    def _():
        o_ref[...] = jnp.zeros(o_ref.shape, F32)


def _experts(blk_e, n_used, src, x, w_gate, w_up, w_down, layer, blk):
    n_rows = src.shape[0]
    d = x.shape[1]
    de = w_gate.shape[3]
    wspec = lambda shp: pl.BlockSpec((None, None) + shp, lambda b, be, nu, sr: (layer, be[b], 0, 0))
    return pl.pallas_call(
        _expert_kernel,
        grid_spec=pltpu.PrefetchScalarGridSpec(
            num_scalar_prefetch=3, grid=(n_rows // blk,),
            in_specs=[pl.BlockSpec(memory_space=pl.ANY), wspec((d, de)), wspec((d, de)), wspec((de, d))],
            out_specs=pl.BlockSpec((blk, d), lambda b, be, nu, sr: (b, 0)),
            scratch_shapes=[pltpu.VMEM((2, blk, d), F32), pltpu.VMEM((d, de), MXU_DTYPE), pltpu.VMEM((d, de), MXU_DTYPE),
                            pltpu.VMEM((de, d), MXU_DTYPE), pltpu.SemaphoreType.DMA((2,))]),
        out_shape=jax.ShapeDtypeStruct((n_rows, d), F32),
        compiler_params=_params(("arbitrary",)),
        name="moe_experts",
    )(blk_e, n_used, src, x, w_gate, w_up, w_down)


def _combine_kernel(dest_ref, y_hbm, x_ref, gt_ref, g_ref, b_ref, xo_ref, xb_ref, rows_s, sem, *, n_tok, alpha):
    tc = x_ref.shape[0]
    base = pl.program_id(0) * tc

    def issue(r, carry):
        for kk in range(TOP_K):
            _row_copy(y_hbm, dest_ref[kk * n_tok + base + r], rows_s.at[kk], r, sem).start()
        return carry

    def drain(r, carry):
        for kk in range(TOP_K):
            _row_copy(y_hbm, 0, rows_s.at[kk], 0, sem).wait()
        return carry

    lax.fori_loop(0, tc, issue, 0, unroll=8)
    lax.fori_loop(0, tc, drain, 0, unroll=8)
    gt = gt_ref[...]
    y = _rnd(gt[:, 0:1]) * rows_s[0] + _rnd(gt[:, 1:2]) * rows_s[1]
    out = _layer_norm(alpha * x_ref[...] + y, g_ref[...], b_ref[...])
    xo_ref[...] = out
    xb_ref[...] = out.astype(xb_ref.dtype)


def _combine(dest_flat, y, x, gates, g, bt, alpha):
    n_tok, d = x.shape
    tc = min(n_tok, 128)
    row = lambda i, dst: (i, 0)
    fixed = lambda i, dst: (0, 0)
    return pl.pallas_call(
        functools.partial(_combine_kernel, n_tok=n_tok, alpha=alpha),
        grid_spec=pltpu.PrefetchScalarGridSpec(
            num_scalar_prefetch=1, grid=(n_tok // tc,),
            in_specs=[pl.BlockSpec(memory_space=pl.ANY), pl.BlockSpec((tc, d), row), pl.BlockSpec((tc, TOP_K), row),
                      pl.BlockSpec((1, d), fixed), pl.BlockSpec((1, d), fixed)],
            out_specs=[pl.BlockSpec((tc, d), row), pl.BlockSpec((tc, d), row)],
            scratch_shapes=[pltpu.VMEM((TOP_K, tc, d), F32), pltpu.SemaphoreType.DMA(())]),
        out_shape=[jax.ShapeDtypeStruct((n_tok, d), F32), jax.ShapeDtypeStruct((n_tok, d), MXU_DTYPE)],
        compiler_params=_params(("arbitrary",)),
        name="moe_combine",
    )(dest_flat, y, x, gates, g.reshape(1, d), bt.reshape(1, d))


def _moe_ln(x, xb, w_router_t, b_router, w_gate, w_up, w_down, layer, g, bt, alpha, n_valid, blk):
    m, d = x.shape
    ne = w_gate.shape[1]
    mp = max(m, LANES)
    logits_t = _mm_nt(w_router_t, xb)
    if mp != m:
        logits_t = jnp.pad(logits_t, ((0, 0), (0, mp - m)))
    eidx, gates, rank, counts = _route(logits_t, b_router, n_valid)
    counts = counts[:, 0].astype(I32)
    padded = (counts + blk - 1) // blk * blk
    pad_end = jnp.cumsum(padded)
    pad_start = pad_end - padded
    n_blocks = -(-(n_valid * TOP_K + ne * (blk - 1)) // blk)
    first_slot = jnp.sum(jnp.where(eidx[:, :m, None] == jnp.arange(ne), pad_start, 0), axis=-1)
    dest = (first_slot + rank[:, :m]).astype(I32)
    n_rows = n_blocks * blk
    live = jnp.arange(m)[None, :] < n_valid
    tok = jnp.broadcast_to(jnp.arange(m, dtype=I32)[None, :], (TOP_K, m))
    src = jnp.zeros((n_rows,), I32).at[jnp.where(live, dest, n_rows).reshape(-1)].set(tok.reshape(-1), mode="drop")
    dest_flat = jnp.where(live, dest, 0).reshape(-1)
    blk_e = jnp.minimum(jnp.searchsorted(pad_end, jnp.arange(n_blocks) * blk, side="right"), ne - 1).astype(I32)
    n_used = (pad_end[-1:] // blk).astype(I32)
    y = _experts(blk_e, n_used, src, x, w_gate, w_up, w_down, layer, blk)
    return _combine(dest_flat, y, x, gates[:, :m].T, g, bt, alpha)


def _rope_tables(pos, half):
    inv = ROPE_THETA ** (-jnp.arange(half, dtype=F32) / half)
    ang = pos.astype(F32)[:, None] * inv[None, :]
    c, s = jnp.cos(ang), jnp.sin(ang)
    reps = LANES // (2 * half)
    return (jnp.tile(jnp.concatenate([c, c], -1), (1, reps)), jnp.tile(jnp.concatenate([-s, s], -1), (1, reps)), half)


def _cols(w, start, width, pad_to=None):
    out = w[:, start:start + width].astype(MXU_DTYPE)
    if pad_to is not None and pad_to > width:
        out = jnp.pad(out, ((0, 0), (0, pad_to - width)))
    return out


def kernel(x_prompt, x_sample, state_mlstm_c, state_mlstm_n, state_mlstm_m, cache_diff_k, cache_diff_v, cache_dsa_k, cache_dsa_v, cache_idx_k, state_ret, page_table, w_in_even, w_out_even, b_igate, b_fgate, g_mlstm, lam_q1, lam_k1, lam_q2, lam_k2, g_subln, w_in_odd, w_out_odd, g_ret, ln_mix_g, ln_mix_b, ln_ffn_g, ln_ffn_b, w_router, b_router, w_gate, w_up, w_down):
    bsz, seq, d_model = x_prompt.shape
    dec_b, dec_seq, _ = x_sample.shape
    assert dec_seq == 1
    depth = w_gate.shape[0]
    alpha = (2 * depth) ** 0.25
    h_a, dh_a = g_mlstm.shape[1:]
    h_b, dv_b = cache_diff_k.shape[3:]
    h_c, dh_c = cache_dsa_k.shape[3:]
    d_i = cache_idx_k.shape[3]
    h_d, dk_d, dv_d = state_ret.shape[2:]
    w_a, w_b, w_c, w_d = h_a * dh_a, h_b * dv_b, h_c * dh_c, h_d * dv_d
    h_i = w_in_odd.shape[2] - (3 * w_c + d_i + 2 * h_d * dk_d + 2 * w_d)
    h_i = h_i // (d_i + 1)
    past_len = page_table.shape[1] * cache_diff_k.shape[2]
    m_p = bsz * seq
    m_s = 2 * SUBLANES

    pos_p = jnp.arange(seq)
    pos_s = jnp.full((m_s,), past_len)
    rope_p = {h: _rope_tables(pos_p, h) for h in (dh_c // 2, d_i // 2)}
    rope_s = {h: _rope_tables(pos_s, h) for h in (dh_c // 2, d_i // 2)}
    log_gamma = jnp.log(1.0 - 2.0 ** (-5.0 - jnp.arange(h_d, dtype=F32)))
    w_router_t = w_router.T.astype(MXU_DTYPE)

    xp = x_prompt.reshape(m_p, d_model)
    xs = jnp.pad(x_sample.reshape(dec_b, d_model), ((0, m_s - dec_b), (0, 0)))
    xp_b, xs_b = xp.astype(MXU_DTYPE), xs.astype(MXU_DTYPE)
    names = ("mlstm_c", "mlstm_n", "mlstm_m", "diff_k", "diff_v", "dsa_k", "dsa_v", "idx_k", "ret")
    new_p = {k: [] for k in names}
    new_s = {k: [] for k in names}

    def pad_tokens(a):
        return jnp.pad(a[:dec_b, None, :], ((0, 0), (0, CHUNK - 1), (0, 0)))

    for l in range(depth):
        j = l // 2
        if l % 2 == 0:
            w = w_in_even[j]
            o_g = 4 * w_a
            o_b = o_g + 2 * h_a
            w_main = _cols(w, 0, 4 * w_a)
            w_gates_t = w[:, o_g:o_b].T.astype(MXU_DTYPE)
            w_qb, w_kb, w_vb = (_cols(w, o_b + i * w_b, w_b) for i in range(3))
            lams = (lam_q1[j], lam_k1[j], lam_q2[j], lam_k2[j])
            lam_init = 0.8 - 0.6 * math.exp(-0.3 * l)
            half = dv_b // 4
            z = _mm(xp_b, w_main).reshape(bsz, seq, 4 * w_a)
            gt = _mm_nt(w_gates_t, xp_b).reshape(2 * h_a, bsz, seq).transpose(1, 0, 2).reshape(bsz, 2 * h_a, 1, seq)
            r3 = lambda a: a.reshape(bsz, seq, a.shape[1])
            qb_m = _mm(xp_b, w_qb, rope=rope_p[half], out_dtype=MXU_DTYPE)
            kb, kb_m = _mm(xp_b, w_kb, rope=rope_p[half], mxu_copy=True)
            vb, vb_m = _mm(xp_b, w_vb, mxu_copy=True)
            zero = lambda *s: jnp.zeros(s, F32)
            h_mix, c_p, n_p, mm_p = _mlstm(z, gt, b_igate[j], b_fgate[j], g_mlstm[j], zero(bsz, h_a, dh_a, dh_a),
                                           zero(bsz, h_a, dh_a), zero(bsz, h_a), CHUNK)
            ob = _diff_attn(r3(qb_m), r3(kb_m), r3(vb_m), lams, g_subln[j], lam_init, h_b)
            mix_a_p, mix_b_p = h_mix.reshape(m_p, w_a), ob.reshape(m_p, w_b)
            st_p = (c_p, n_p.reshape(bsz, h_a, dh_a), mm_p.reshape(bsz, h_a),
                    kb.reshape(bsz, seq, h_b, dv_b), vb.reshape(bsz, seq, h_b, dv_b))
            zs = pad_tokens(_mm(xs_b, w_main))
            gts = _mm_nt(w_gates_t, xs_b)[:, :dec_b].T
            gts = jnp.pad(gts[:, :, None, None], ((0, 0), (0, 0), (0, 0), (0, CHUNK - 1)))
            qbs = _mm(xs_b, w_qb, rope=rope_s[half])[:dec_b]
            kbs = _mm(xs_b, w_kb, rope=rope_s[half])[:dec_b]
            vbs = _mm(xs_b, w_vb)[:dec_b]
            hs_mix, c_s, n_s, mm_s = _mlstm(zs, gts, b_igate[j], b_fgate[j], g_mlstm[j], state_mlstm_c[j],
                                            state_mlstm_n[j], state_mlstm_m[j], 1)
            obs = _diff_decode(qbs, kbs, vbs, cache_diff_k, cache_diff_v, j, page_table, lams, g_subln[j], lam_init, h_b)
            pad_rows = lambda a: jnp.pad(a, ((0, m_s - dec_b), (0, 0)))
            mix_a_s, mix_b_s = pad_rows(hs_mix[:, 0, :]), pad_rows(obs)
            st_s = (c_s, n_s.reshape(dec_b, h_a, dh_a), mm_s.reshape(dec_b, h_a),
                    kbs.reshape(dec_b, 1, h_b, dv_b), vbs.reshape(dec_b, 1, h_b, dv_b))
            w_out = w_out_even[j].astype(MXU_DTYPE)
            keys = names[:5]
        else:
            w = w_in_odd[j]
            o_qi = 3 * w_c
            o_ki = o_qi + h_i * d_i
            o_wi = o_ki + d_i
            o_qd = o_wi + h_i
            o_vd = o_qd + 2 * h_d * dk_d
            w_qc, w_kc, w_vc = (_cols(w, i * w_c, w_c) for i in range(3))
            w_qi = _cols(w, o_qi, h_i * d_i)
            w_ki = _cols(w, o_ki, d_i, pad_to=LANES)
            w_wi = _cols(w, o_wi, h_i, pad_to=LANES)
            w_qkd = _cols(w, o_qd, 2 * h_d * dk_d)
            w_vgd = _cols(w, o_vd, 2 * w_d)
            hc, hi, hd = dh_c // 2, d_i // 2, dk_d // 2

            def project(xb_, rope, q_dtype):
                qc = _mm(xb_, w_qc, rope=rope[hc], out_dtype=q_dtype)
                kc, kc_m = _mm(xb_, w_kc, rope=rope[hc], mxu_copy=True)
                vc, vc_m = _mm(xb_, w_vc, mxu_copy=True)
                qi = _mm(xb_, w_qi, rope=rope[hi], out_dtype=q_dtype)
                ki = _mm(xb_, w_ki, rope=rope[hi], n_out=d_i)
                wi = _mm(xb_, w_wi)
                qkd = _mm(xb_, w_qkd, rope=rope[hd])
                vgd = _mm(xb_, w_vgd)
                return (qc, kc, vc, qi, ki, wi, qkd, vgd), (kc_m, vc_m)

            (qc, kc, vc, qi, ki, wi, qkd, vgd), (kc_m, vc_m) = project(xp_b, rope_p, MXU_DTYPE)
            r3 = lambda a: a.reshape(bsz, seq, a.shape[1])
            oc = _dsa_attn(r3(qc), r3(kc_m), r3(vc_m), r3(qi), r3(ki), r3(wi), h_c, h_i)
            od, s_p = _retention(r3(qkd), r3(vgd), log_gamma, g_ret[j], jnp.zeros((bsz, h_d, dk_d, dv_d), F32), CHUNK)
            mix_a_p, mix_b_p = oc.reshape(m_p, w_c), od.reshape(m_p, w_d)
            st_p = (kc.reshape(bsz, seq, h_c, dh_c), vc.reshape(bsz, seq, h_c, dh_c), ki.reshape(bsz, seq, d_i), s_p)
            qc, kc, vc, qi, ki, wi, qkd, vgd = (a[:dec_b] for a in project(xs_b, rope_s, F32)[0])
            ocs = _dsa_decode(qc, kc, vc, qi, ki, wi, cache_dsa_k, cache_dsa_v, cache_idx_k, j, page_table, h_c, h_i)
            ods, s_s = _retention(pad_tokens(qkd), pad_tokens(vgd), log_gamma, g_ret[j], state_ret[j], 1)
            pad_rows = lambda a: jnp.pad(a, ((0, m_s - dec_b), (0, 0)))
            mix_a_s, mix_b_s = pad_rows(ocs), pad_rows(ods[:, 0, :])
            st_s = (kc.reshape(dec_b, 1, h_c, dh_c), vc.reshape(dec_b, 1, h_c, dh_c), ki.reshape(dec_b, 1, d_i), s_s)
            w_out = w_out_odd[j].astype(MXU_DTYPE)
            keys = names[5:]
        for k, a, b in zip(keys, st_p, st_s):
            new_p[k].append(a)
            new_s[k].append(b)
        xp, xp_b = _proj_ln(mix_a_p, mix_b_p, w_out, xp, ln_mix_g[l], ln_mix_b[l], alpha)
        xs, xs_b = _proj_ln(mix_a_s, mix_b_s, w_out, xs, ln_mix_g[l], ln_mix_b[l], alpha)
        moe_w = (w_router_t, b_router, w_gate, w_up, w_down, l, ln_ffn_g[l], ln_ffn_b[l], alpha)
        xp, xp_b = _moe_ln(xp, xp_b, *moe_w, m_p, 256)
        xs, xs_b = _moe_ln(xs, xs_b, *moe_w, dec_b, 16)

    out = [xp.reshape(bsz, seq, d_model), xs[:dec_b].reshape(dec_b, 1, d_model)]
    for k in names:
        out += [jnp.stack(new_p[k]), jnp.stack(new_s[k])]
    return tuple(out)
```
